```python
import math
import jax, jax.numpy as jnp
from jax import lax
import numpy as np


D_MODEL = 1024
BATCH = 8
SEQ = 8192
DEPTH = 4
DEC_BATCH = 32
DEC_SEQ = 2048
PAST_LEN = 128

HEAD_DIM = 64
A_PATTERNS = ((128, 1), (512, 4), (2048, 16))
A_GROUPS = len(A_PATTERNS)
A_HEADS_PER_GROUP = 4
A_HEADS = A_GROUPS * A_HEADS_PER_GROUP
BAND_BLOCK = 64
B_HEADS = 4
C_Q_HEADS = 16
C_KV_HEADS = 4
C_GROUP = C_Q_HEADS // C_KV_HEADS
QUERY_BLOCK = 128
ROPE_THETA = 500000.0
ROT_DIM = HEAD_DIM // 4
AXIAL_THETA = 10000.0
AXIAL_DIM = HEAD_DIM // 2
GRID_W = 64
D_FF = 2816
CONV_W = 3
NORM_EPS = 1e-6
SUBLN_EPS = 1e-5
NEG_INF = -1e30
N_EVEN = (DEPTH + 1) // 2
N_ODD = DEPTH // 2

A_QKV = A_HEADS * HEAD_DIM
B_QK = B_HEADS * 2 * HEAD_DIM
B_V = B_HEADS * 2 * HEAD_DIM
AB_IN = 3 * A_QKV + 2 * B_QK + B_V
AB_OUT = A_HEADS_PER_GROUP * HEAD_DIM + B_V
C_IN = (C_Q_HEADS + 2 * C_KV_HEADS) * HEAD_DIM
C_OUT = C_Q_HEADS * HEAD_DIM

kernel_name = 'hybrid_dilated_diff_axial_encoder'


def rms_norm(x, g, eps=NORM_EPS):
    xf = x.astype(jnp.float32)
    y = xf * lax.rsqrt(jnp.mean(xf * xf, axis=-1, keepdims=True) + eps)
    return (y * g.astype(jnp.float32)).astype(x.dtype)


def rope_cos_sin(pos, dim, theta):
    inv = theta ** (-jnp.arange(0, dim, 2, dtype=jnp.float32) / dim)
    ang = pos.astype(jnp.float32)[:, None] * inv[None, :]
    return jnp.cos(ang), jnp.sin(ang)


def rotate(x, cos, sin):
    half = x.shape[-1] // 2
    shape = cos.shape[:1] + (1,) * (x.ndim - 3) + cos.shape[1:]
    c, s = cos.reshape(shape), sin.reshape(shape)
    xf = x.astype(jnp.float32)
    x1, x2 = xf[..., :half], xf[..., half:]
    return jnp.concatenate([x1 * c - x2 * s, x2 * c + x1 * s], axis=-1).astype(x.dtype)


def partial_rotary(x, cos, sin):
    return jnp.concatenate([rotate(x[..., :ROT_DIM], cos, sin), x[..., ROT_DIM:]], axis=-1)


def axial_rotary(x, cos_r, sin_r, cos_c, sin_c):
    return jnp.concatenate([rotate(x[..., :AXIAL_DIM], cos_r, sin_r),
                            rotate(x[..., AXIAL_DIM:], cos_c, sin_c)], axis=-1)


def band_attention(q, k, v, half):
    n, L, h, dh = q.shape
    qb_size = math.gcd(L, BAND_BLOCK)
    nb = L // qb_size
    width = qb_size + 2 * half
    pad = ((0, 0), (half, half), (0, 0), (0, 0))
    kidx = jnp.arange(nb)[:, None] * qb_size + jnp.arange(width)[None, :]
    kb = jnp.pad(k, pad)[:, kidx]
    vb = jnp.pad(v, pad)[:, kidx]
    qb = q.reshape(n, nb, qb_size, h, dh)
    s = jnp.einsum('nbqhd,nbkhd->nbhqk', qb, kb).astype(jnp.float32) * (dh ** -0.5)
    rel = jnp.arange(width)[None, :] - half - jnp.arange(qb_size)[:, None]
    kpos = kidx - half
    valid = (jnp.abs(rel) <= half)[None] & ((kpos >= 0) & (kpos < L))[:, None, :]
    s = jnp.where(valid[None, :, None], s, NEG_INF)
    m = jnp.max(s, axis=-1, keepdims=True)
    p = jnp.exp(s - m)
    den = jnp.sum(p, axis=-1, keepdims=True)
    o = jnp.einsum('nbhqk,nbkhd->nbqhd', (p / den).astype(v.dtype), vb)
    lse = (m + jnp.log(den))[..., 0]
    return o.reshape(n, L, h, dh), jnp.swapaxes(lse, 2, 3).reshape(n, L, h)


def dilated_group(q, k, v, window, dilation):
    b, S, h, dh = q.shape
    L = S // dilation

    def strided(t):
        return t.reshape(b, L, dilation, h, dh).transpose(0, 2, 1, 3, 4).reshape(b * dilation, L, h, dh)

    o, lse = band_attention(strided(q), strided(k), strided(v), window // (2 * dilation))
    o = o.reshape(b, dilation, L, h, dh).transpose(0, 2, 1, 3, 4).reshape(b, S, h, dh)
    lse = lse.reshape(b, dilation, L, h).transpose(0, 2, 1, 3).reshape(b, S, h)
    return o, lse


def mixer_a(q, k, v):
    outs, lses = [], []
    for g, (window, dilation) in enumerate(A_PATTERNS):
        o, l = dilated_group(q[:, :, g], k[:, :, g], v[:, :, g], window, dilation)
        outs.append(o)
        lses.append(l)
    wts = jax.nn.softmax(jnp.stack(lses, axis=0), axis=0)
    o = jnp.sum(jnp.stack(outs, axis=0).astype(jnp.float32) * wts[..., None], axis=0)
    return o.astype(q.dtype)


def mixer_b(q, k, v, lam, lam_init, subln_g):
    b, S, h, _, dh = q.shape
    nb = S // QUERY_BLOCK
    qb = jnp.moveaxis(q.reshape(b, nb, QUERY_BLOCK, h, 2, dh), 1, 0)
    scale = dh ** -0.5

    def block(qblk):
        s = jnp.einsum('bqhmd,bkhmd->bhmqk', qblk, k).astype(jnp.float32) * scale
        p = jax.nn.softmax(s, axis=-1)
        a = p[:, :, 0] - lam * p[:, :, 1]
        return jnp.einsum('bhqk,bkhe->bqhe', a.astype(v.dtype), v)

    o = jnp.moveaxis(lax.map(block, qb), 0, 1).reshape(b, S, h, 2 * dh)
    return rms_norm(o, subln_g, SUBLN_EPS) * (1.0 - lam_init)


def mixer_c(q, k, v):
    b, S, n, g, dh = q.shape
    nb = S // QUERY_BLOCK
    qb = jnp.moveaxis(q.reshape(b, nb, QUERY_BLOCK, n, g, dh), 1, 0)
    scale = dh ** -0.5

    def block(qblk):
        s = jnp.einsum('bqngd,bsnd->bngqs', qblk, k).astype(jnp.float32) * scale
        p = jax.nn.softmax(s, axis=-1)
        return jnp.einsum('bngqs,bsnd->bqngd', p.astype(v.dtype), v)

    return jnp.moveaxis(lax.map(block, qb), 0, 1).reshape(b, S, n * g * dh)


def even_layer(h, cos, sin, w_in, qn_a, kn_a, qn_b, kn_b, lq1, lk1, lq2, lk2, subln_g, w_out, lam_init):
    b, S, _ = h.shape
    cuts = [A_QKV, 2 * A_QKV, 3 * A_QKV, 3 * A_QKV + B_QK, 3 * A_QKV + 2 * B_QK]
    aq, ak, av, bq, bk, bv = jnp.split(h @ w_in, cuts, axis=-1)
    shp_a = (b, S, A_GROUPS, A_HEADS_PER_GROUP, HEAD_DIM)
    aq = partial_rotary(rms_norm(aq.reshape(shp_a), qn_a), cos, sin)
    ak = partial_rotary(rms_norm(ak.reshape(shp_a), kn_a), cos, sin)
    oa = mixer_a(aq, ak, av.reshape(shp_a)).reshape(b, S, A_HEADS_PER_GROUP * HEAD_DIM)
    shp_b = (b, S, B_HEADS, 2, HEAD_DIM)
    bq = partial_rotary(rms_norm(bq.reshape(shp_b), qn_b), cos, sin)
    bk = partial_rotary(rms_norm(bk.reshape(shp_b), kn_b), cos, sin)
    bv = bv.reshape(b, S, B_HEADS, 2 * HEAD_DIM)
    f32 = jnp.float32
    lam = (jnp.exp(jnp.sum(lq1.astype(f32) * lk1.astype(f32)))
           - jnp.exp(jnp.sum(lq2.astype(f32) * lk2.astype(f32))) + lam_init)
    ob = mixer_b(bq, bk, bv, lam, lam_init, subln_g).reshape(b, S, B_V)
    return jnp.concatenate([oa, ob], axis=-1) @ w_out


def odd_layer(h, axial, w_in, qn, kn, w_out):
    b, S, _ = h.shape
    q, k, v = jnp.split(h @ w_in, [C_Q_HEADS * HEAD_DIM, (C_Q_HEADS + C_KV_HEADS) * HEAD_DIM], axis=-1)
    q = axial_rotary(rms_norm(q.reshape(b, S, C_KV_HEADS, C_GROUP, HEAD_DIM), qn), *axial)
    k = axial_rotary(rms_norm(k.reshape(b, S, C_KV_HEADS, HEAD_DIM), kn), *axial)
    v = v.reshape(b, S, C_KV_HEADS, HEAD_DIM)
    return mixer_c(q, k, v) @ w_out


def conv_ffn(h, w_up, conv_w, conv_b, w_down):
    u = h @ w_up
    up = jnp.pad(u, ((0, 0), (1, 1), (0, 0)))
    u = up[:, :-2] * conv_w[0] + up[:, 1:-1] * conv_w[1] + up[:, 2:] * conv_w[2] + conv_b
    gate, val = jnp.split(u, 2, axis=-1)
    return (jax.nn.silu(gate) * val) @ w_down


def trunk(x, norm_mix, norm_ffn, w_in_ab, q_norm_a, k_norm_a, q_norm_b, k_norm_b,
          lambda_q1, lambda_k1, lambda_q2, lambda_k2, subln_b, w_out_ab,
          w_in_c, q_norm_c, k_norm_c, w_out_c, w_up, conv_w, conv_b, w_down):
    S = x.shape[1]
    cos, sin = rope_cos_sin(jnp.arange(S), ROT_DIM, ROPE_THETA)
    rows = S // GRID_W
    row = jnp.repeat(jnp.arange(rows), GRID_W)
    col = jnp.tile(jnp.arange(GRID_W), rows)
    axial = rope_cos_sin(row, AXIAL_DIM, AXIAL_THETA) + rope_cos_sin(col, AXIAL_DIM, AXIAL_THETA)
    for i in range(DEPTH):
        h = rms_norm(x, norm_mix[i])
        j = i // 2
        if i % 2 == 0:
            lam_init = 0.8 - 0.6 * math.exp(-0.3 * i)
            x = x + even_layer(h, cos, sin, w_in_ab[j], q_norm_a[j], k_norm_a[j], q_norm_b[j], k_norm_b[j],
                               lambda_q1[j], lambda_k1[j], lambda_q2[j], lambda_k2[j], subln_b[j],
                               w_out_ab[j], lam_init)
        else:
            x = x + odd_layer(h, axial, w_in_c[j], q_norm_c[j], k_norm_c[j], w_out_c[j])
        x = x + conv_ffn(rms_norm(x, norm_ffn[i]), w_up[i], conv_w[i], conv_b[i], w_down[i])
    return x


def setup_inputs(seed: int = 0) -> dict:
    key = jax.random.key(seed)
    ks = jax.random.split(key, 32)
    f32 = jnp.float32

    def dense(k, shape, fan_in):
        return jax.random.normal(k, shape, f32) * (fan_in ** -0.5)

    def gain(k, shape):
        return 1.0 + 0.02 * jax.random.normal(k, shape, f32)

    return {
        'x_prompt': jax.random.normal(ks[0], (BATCH, SEQ, D_MODEL), f32),
        'x_sample': jax.random.normal(ks[1], (DEC_BATCH, DEC_SEQ, D_MODEL), f32),
        'norm_mix': gain(ks[2], (DEPTH, D_MODEL)),
        'norm_ffn': gain(ks[3], (DEPTH, D_MODEL)),
        'w_in_ab': dense(ks[4], (N_EVEN, D_MODEL, AB_IN), D_MODEL),
        'q_norm_a': gain(ks[5], (N_EVEN, HEAD_DIM)),
        'k_norm_a': gain(ks[6], (N_EVEN, HEAD_DIM)),
        'q_norm_b': gain(ks[7], (N_EVEN, HEAD_DIM)),
        'k_norm_b': gain(ks[8], (N_EVEN, HEAD_DIM)),
        'lambda_q1': 0.1 * jax.random.normal(ks[9], (N_EVEN, HEAD_DIM), f32),
        'lambda_k1': 0.1 * jax.random.normal(ks[10], (N_EVEN, HEAD_DIM), f32),
        'lambda_q2': 0.1 * jax.random.normal(ks[11], (N_EVEN, HEAD_DIM), f32),
        'lambda_k2': 0.1 * jax.random.normal(ks[12], (N_EVEN, HEAD_DIM), f32),
        'subln_b': gain(ks[13], (N_EVEN, 2 * HEAD_DIM)),
        'w_out_ab': dense(ks[14], (N_EVEN, AB_OUT, D_MODEL), AB_OUT),
        'w_in_c': dense(ks[15], (N_ODD, D_MODEL, C_IN), D_MODEL),
        'q_norm_c': gain(ks[16], (N_ODD, HEAD_DIM)),
        'k_norm_c': gain(ks[17], (N_ODD, HEAD_DIM)),
        'w_out_c': dense(ks[18], (N_ODD, C_OUT, D_MODEL), C_OUT),
        'w_up': dense(ks[19], (DEPTH, D_MODEL, 2 * D_FF), D_MODEL),
        'conv_w': dense(ks[20], (DEPTH, CONV_W, 2 * D_FF), CONV_W),
        'conv_b': 0.01 * jax.random.normal(ks[21], (DEPTH, 2 * D_FF), f32),
        'w_down': dense(ks[22], (DEPTH, D_FF, D_MODEL), D_FF),
    }


def reference(x_prompt, x_sample, norm_mix, norm_ffn, w_in_ab, q_norm_a, k_norm_a, q_norm_b, k_norm_b,
              lambda_q1, lambda_k1, lambda_q2, lambda_k2, subln_b, w_out_ab,
              w_in_c, q_norm_c, k_norm_c, w_out_c, w_up, conv_w, conv_b, w_down):
    y_prompt = trunk(x_prompt, norm_mix, norm_ffn, w_in_ab, q_norm_a, k_norm_a, q_norm_b, k_norm_b,
                     lambda_q1, lambda_k1, lambda_q2, lambda_k2, subln_b, w_out_ab,
                     w_in_c, q_norm_c, k_norm_c, w_out_c, w_up, conv_w, conv_b, w_down)
    y_sample = trunk(x_sample, norm_mix, norm_ffn, w_in_ab, q_norm_a, k_norm_a, q_norm_b, k_norm_b,
                     lambda_q1, lambda_k1, lambda_q2, lambda_k2, subln_b, w_out_ab,
                     w_in_c, q_norm_c, k_norm_c, w_out_c, w_up, conv_w, conv_b, w_down)
    return (y_prompt, y_sample)
```

```python
import functools
import math

import jax
import jax.numpy as jnp
from jax import lax
from jax.experimental import pallas as pl
from jax.experimental.pallas import tpu as pltpu

F32 = jnp.float32
BF16 = jnp.bfloat16

D_MODEL = 1024
HEAD_DIM = 64
A_PATTERNS = ((128, 1), (512, 4), (2048, 16))
A_GROUPS = 3
A_HEADS_PER_GROUP = 4
A_QKV = A_GROUPS * A_HEADS_PER_GROUP * HEAD_DIM
A_GROUP_W = A_HEADS_PER_GROUP * HEAD_DIM
B_HEADS = 4
B_QK = B_HEADS * 2 * HEAD_DIM
B_V = B_HEADS * 2 * HEAD_DIM
C_Q_HEADS = 16
C_KV_HEADS = 4
C_GROUP = C_Q_HEADS // C_KV_HEADS
ROPE_THETA = 500000.0
ROT_DIM = HEAD_DIM // 4
AXIAL_THETA = 10000.0
AXIAL_DIM = HEAD_DIM // 2
GRID_W = 64
D_FF = 2816
NORM_EPS = 1e-6
SUBLN_EPS = 1e-5
NEG_INF = -1e30
LOG2E = 1.4426950408889634
LN2 = 0.6931471805599453

V7X_LANES = 128
V7X_VMEM_BYTES = 64 * 1024 * 1024
VMEM_LIMIT = V7X_VMEM_BYTES - 8 * 1024 * 1024
BAND_HALF = 64
BAND_SUB = 128
BAND_WIN = BAND_SUB + 2 * V7X_LANES
FF_CHUNK = 256


def _params(*semantics):
    return pltpu.CompilerParams(dimension_semantics=semantics, vmem_limit_bytes=VMEM_LIMIT)


def _rms_rows(x, g):
    ms = jnp.mean(x * x, axis=-1, keepdims=True)
    return x * lax.rsqrt(ms + NORM_EPS) * g


def _head_norm_rot(y, gain, cos, sin, axial):
    ssq = jnp.sum(y * y, axis=0, keepdims=True)
    yn = y * lax.rsqrt(ssq * (1.0 / HEAD_DIM) + NORM_EPS) * gain
    if axial:
        h = AXIAL_DIM // 2
        a1, a2, b1, b2 = yn[0:h], yn[h:2 * h], yn[2 * h:3 * h], yn[3 * h:4 * h]
        cr, cc = cos[0:h], cos[h:2 * h]
        sr, sc = sin[0:h], sin[h:2 * h]
        return jnp.concatenate(
            [a1 * cr - a2 * sr, a2 * cr + a1 * sr, b1 * cc - b2 * sc, b2 * cc + b1 * sc], axis=0)
    h = ROT_DIM // 2
    x1, x2 = yn[0:h], yn[h:2 * h]
    return jnp.concatenate([x1 * cos - x2 * sin, x2 * cos + x1 * sin, yn[2 * h:]], axis=0)


def _proj_kernel(x_ref, g_ref, wt_ref, gain_ref, cos_ref, sin_ref, qt_ref, k_ref, vt_ref,
                 *, nq, nk, axial):
    h = _rms_rows(x_ref[...], g_ref[...]).astype(BF16)
    yt = lax.dot_general(wt_ref[...], h, (((1,), (1,)), ((), ())),
                         preferred_element_type=F32)
    cos = cos_ref[...]
    sin = sin_ref[...]
    gq = gain_ref[0:HEAD_DIM, :]
    gk = gain_ref[HEAD_DIM:2 * HEAD_DIM, :]
    for i in range(nq):
        y = yt[i * HEAD_DIM:(i + 1) * HEAD_DIM, :]
        qt_ref[i * HEAD_DIM:(i + 1) * HEAD_DIM, :] = _head_norm_rot(y, gq, cos, sin, axial).astype(BF16)
    base = nq * HEAD_DIM
    for p in range(nk // 2):
        pair = []
        for i in (2 * p, 2 * p + 1):
            y = yt[base + i * HEAD_DIM:base + (i + 1) * HEAD_DIM, :]
            pair.append(_head_norm_rot(y, gk, cos, sin, axial))
        kt = jnp.concatenate(pair, axis=0)
        k_ref[:, p * 2 * HEAD_DIM:(p + 1) * 2 * HEAD_DIM] = kt.T.astype(BF16)
    base = (nq + nk) * HEAD_DIM
    vt_ref[...] = yt[base:, :].astype(BF16)


def _proj(x2d, g, wt, gains, cos_t, sin_t, *, seq, dil, nq, nk, nv, axial, tm):
    t_total = x2d.shape[0]
    length = seq // dil
    tm = min(tm, length)
    nlb = length // tm
    nrow = t_total // dil // tm
    xv = x2d.reshape(t_total // dil, dil * D_MODEL)
    n_out = (nq + nk) * HEAD_DIM + nv
    rot_rows = cos_t.shape[0]

    def col_block(i, r):
        return ((i // nlb) * dil + r) * nlb + (i % nlb)

    kern = functools.partial(_proj_kernel, nq=nq, nk=nk, axial=axial)
    return pl.pallas_call(
        kern,
        grid=(nrow, dil),
        in_specs=[
            pl.BlockSpec((tm, D_MODEL), lambda i, r: (i, r)),
            pl.BlockSpec((1, D_MODEL), lambda i, r: (0, 0)),
            pl.BlockSpec((n_out, D_MODEL), lambda i, r: (0, 0)),
            pl.BlockSpec((2 * HEAD_DIM, tm), lambda i, r: (0, 0)),
            pl.BlockSpec((rot_rows, tm), lambda i, r: (0, r * nlb + i % nlb)),
            pl.BlockSpec((rot_rows, tm), lambda i, r: (0, r * nlb + i % nlb)),
        ],
        out_specs=[
            pl.BlockSpec((nq * HEAD_DIM, tm), lambda i, r: (0, col_block(i, r))),
            pl.BlockSpec((tm, nk * HEAD_DIM), lambda i, r: (col_block(i, r), 0)),
            pl.BlockSpec((nv, tm), lambda i, r: (0, col_block(i, r))),
        ],
        out_shape=[
            jax.ShapeDtypeStruct((nq * HEAD_DIM, t_total), BF16),
            jax.ShapeDtypeStruct((t_total, nk * HEAD_DIM), BF16),
            jax.ShapeDtypeStruct((nv, t_total), BF16),
        ],
        compiler_params=_params("parallel", "parallel"),
        name="proj",
    )(xv, g, wt, gains[:, :tm], cos_t, sin_t)


def _flash_unit(qpad_ref, k_ref, vt_ref, acc_ref, m_ref, l_ref, *, seq, tkc):
    m_ref[...] = jnp.full(m_ref.shape, NEG_INF, F32)
    l_ref[...] = jnp.zeros(l_ref.shape, F32)
    acc_ref[...] = jnp.zeros(acc_ref.shape, F32)

    def body(c, carry):
        off = pl.multiple_of(c * tkc, tkc)
        s = jnp.dot(k_ref[pl.ds(off, tkc), :], qpad_ref[...], preferred_element_type=F32)
        m_old = m_ref[...]
        m_new = jnp.maximum(m_old, jnp.max(s, axis=0, keepdims=True))
        alpha = jnp.exp2(m_old - m_new)
        p = jnp.exp2(s - m_new)
        l_ref[...] = alpha * l_ref[...] + jnp.sum(p, axis=0, keepdims=True)
        pv = jnp.dot(vt_ref[:, pl.ds(off, tkc)], p.astype(BF16), preferred_element_type=F32)
        acc_ref[...] = alpha * acc_ref[...] + pv
        m_ref[...] = m_new
        return carry

    lax.fori_loop(0, seq // tkc, body, 0)


def _attn_c_kernel(qt_ref, k_ref, vt_ref, o_ref, qpad_ref, acc_ref, m_ref, l_ref, *, seq, tq, tkc):
    n = pl.program_id(1)
    qcat = jnp.concatenate([qt_ref[g * HEAD_DIM:(g + 1) * HEAD_DIM, :] for g in range(C_GROUP)], axis=1)
    zeros = jnp.zeros_like(qcat)

    @pl.when(n % 2 == 0)
    def _():
        qpad_ref[0:HEAD_DIM, :] = qcat
        qpad_ref[HEAD_DIM:, :] = zeros

    @pl.when(n % 2 == 1)
    def _():
        qpad_ref[0:HEAD_DIM, :] = zeros
        qpad_ref[HEAD_DIM:, :] = qcat

    _flash_unit(qpad_ref, k_ref, vt_ref, acc_ref, m_ref, l_ref, seq=seq, tkc=tkc)
    o = acc_ref[...] * (1.0 / l_ref[...])
    for g in range(C_GROUP):
        o_ref[g * HEAD_DIM:(g + 1) * HEAD_DIM, :] = o[:, g * tq:(g + 1) * tq].astype(BF16)


def _attn_c(qt, k, vt, *, batch, seq, tq, tkc):
    t_total = batch * seq
    tq = min(tq, seq)
    tkc = min(tkc, seq)
    nqb = seq // tq
    ncols = C_GROUP * tq
    kern = functools.partial(_attn_c_kernel, seq=seq, tq=tq, tkc=tkc)
    return pl.pallas_call(
        kern,
        grid=(batch, C_KV_HEADS, nqb),
        in_specs=[
            pl.BlockSpec((C_GROUP * HEAD_DIM, tq), lambda b, n, i: (n, b * nqb + i)),
            pl.BlockSpec((seq, 2 * HEAD_DIM), lambda b, n, i: (b, n // 2)),
            pl.BlockSpec((HEAD_DIM, seq), lambda b, n, i: (n, b)),
        ],
        out_specs=pl.BlockSpec((C_GROUP * HEAD_DIM, tq), lambda b, n, i: (n, b * nqb + i)),
        out_shape=jax.ShapeDtypeStruct((C_Q_HEADS * HEAD_DIM, t_total), BF16),
        scratch_shapes=[
            pltpu.VMEM((2 * HEAD_DIM, ncols), BF16),
            pltpu.VMEM((HEAD_DIM, ncols), F32),
            pltpu.VMEM((1, ncols), F32),
            pltpu.VMEM((1, ncols), F32),
        ],
        compiler_params=_params("parallel", "parallel", "parallel"),
        name="attn_c",
    )(qt, k, vt)


def _attn_b_kernel(lam_ref, qt_ref, k_ref, vt_ref, sg_ref, o_ref, qpad_ref, acc_ref, m_ref, l_ref,
                   *, seq, tq, tkc):
    q = qt_ref[...]
    zeros = jnp.zeros((HEAD_DIM, tq), BF16)
    qpad_ref[0:HEAD_DIM, 0:tq] = q[0:HEAD_DIM]
    qpad_ref[0:HEAD_DIM, tq:] = zeros
    qpad_ref[HEAD_DIM:, 0:tq] = zeros
    qpad_ref[HEAD_DIM:, tq:] = q[HEAD_DIM:]
    _flash_unit(qpad_ref, k_ref, vt_ref, acc_ref, m_ref, l_ref, seq=seq, tkc=tkc)
    on = acc_ref[...] * (1.0 / l_ref[...])
    o = on[:, 0:tq] - lam_ref[0] * on[:, tq:]
    ms = jnp.mean(o * o, axis=0, keepdims=True)
    o_ref[...] = (o * lax.rsqrt(ms + SUBLN_EPS) * sg_ref[...]).astype(BF16)


def _attn_b(lam, qt, k, vt, sg, *, batch, seq, tq, tkc):
    t_total = batch * seq
    tq = min(tq, seq)
    tkc = min(tkc, seq)
    nqb = seq // tq
    kern = functools.partial(_attn_b_kernel, seq=seq, tq=tq, tkc=tkc)
    return pl.pallas_call(
        kern,
        grid=(batch, B_HEADS, nqb),
        in_specs=[
            pl.BlockSpec(memory_space=pltpu.SMEM),
            pl.BlockSpec((2 * HEAD_DIM, tq), lambda b, h, i: (h, b * nqb + i)),
            pl.BlockSpec((seq, 2 * HEAD_DIM), lambda b, h, i: (b, h)),
            pl.BlockSpec((2 * HEAD_DIM, seq), lambda b, h, i: (h, b)),
            pl.BlockSpec((2 * HEAD_DIM, tq), lambda b, h, i: (0, 0)),
        ],
        out_specs=pl.BlockSpec((2 * HEAD_DIM, tq), lambda b, h, i: (h, b * nqb + i)),
        out_shape=jax.ShapeDtypeStruct((B_V, t_total), BF16),
        scratch_shapes=[
            pltpu.VMEM((2 * HEAD_DIM, 2 * tq), BF16),
            pltpu.VMEM((2 * HEAD_DIM, 2 * tq), F32),
            pltpu.VMEM((1, 2 * tq), F32),
            pltpu.VMEM((1, 2 * tq), F32),
        ],
        compiler_params=_params("parallel", "parallel", "parallel"),
        name="attn_b",
    )(lam, qt, k, vt, sg[:, :tq])


def _band_kernel(qt_ref, kp_ref, km_ref, kn_ref, vp_ref, vm_ref, vn_ref, o_ref, lse_ref, *, length, tqs):
    l0 = pl.program_id(1) * tqs
    kwin = jnp.concatenate([kp_ref[...], km_ref[...], kn_ref[...]], axis=0)
    vwin = jnp.concatenate([vp_ref[...], vm_ref[...], vn_ref[...]], axis=1)
    row = lax.broadcasted_iota(jnp.int32, (BAND_WIN, BAND_SUB), 0)
    col = lax.broadcasted_iota(jnp.int32, (BAND_WIN, BAND_SUB), 1)
    rel = row - V7X_LANES - col
    in_band = jnp.abs(rel) <= BAND_HALF
    zeros = jnp.zeros((HEAD_DIM, BAND_SUB), BF16)
    for j in range(tqs // BAND_SUB):
        c0 = j * BAND_SUB
        q = qt_ref[:, c0:c0 + BAND_SUB]
        kw = kwin[c0:c0 + BAND_WIN]
        vw = vwin[:, c0:c0 + BAND_WIN]
        kpos = l0 + (c0 - V7X_LANES) + row
        valid = in_band & (kpos >= 0) & (kpos < length)
        valid2 = jnp.concatenate([valid, valid], axis=1)
        outs, lses = [], []
        for p in range(A_HEADS_PER_GROUP // 2):
            ha, hb = 2 * p, 2 * p + 1
            qa = q[ha * HEAD_DIM:(ha + 1) * HEAD_DIM]
            qb = q[hb * HEAD_DIM:(hb + 1) * HEAD_DIM]
            qpad = jnp.concatenate([jnp.concatenate([qa, zeros], axis=1),
                                    jnp.concatenate([zeros, qb], axis=1)], axis=0)
            s = jnp.dot(kw[:, p * 2 * HEAD_DIM:(p + 1) * 2 * HEAD_DIM], qpad,
                        preferred_element_type=F32)
            s = jnp.where(valid2, s, NEG_INF)
            m = jnp.max(s, axis=0, keepdims=True)
            pr = jnp.exp2(s - m)
            l = jnp.sum(pr, axis=0, keepdims=True)
            pb = pr.astype(BF16)
            oa = jnp.dot(vw[ha * HEAD_DIM:(ha + 1) * HEAD_DIM], pb[:, :BAND_SUB], preferred_element_type=F32)
            ob = jnp.dot(vw[hb * HEAD_DIM:(hb + 1) * HEAD_DIM], pb[:, BAND_SUB:], preferred_element_type=F32)
            inv = 1.0 / l
            lse = m * LN2 + jnp.log(l)
            outs += [oa * inv[:, :BAND_SUB], ob * inv[:, BAND_SUB:]]
            lses += [jnp.broadcast_to(lse[:, :BAND_SUB], (HEAD_DIM, BAND_SUB)),
                     jnp.broadcast_to(lse[:, BAND_SUB:], (HEAD_DIM, BAND_SUB))]
        o_ref[c0:c0 + BAND_SUB, :] = jnp.concatenate(outs, axis=0).T
        lse_ref[c0:c0 + BAND_SUB, :] = jnp.concatenate(lses, axis=0).T


def _band(qt, k, vt, *, batch, seq, dil, tqs):
    t_total = batch * seq
    length = seq // dil
    tqs = min(tqs, length)
    nqb = length // tqs
    ncls = batch * dil
    nkb = length // V7X_LANES
    r128 = tqs // V7X_LANES

    def prev_blk(c, i):
        return c * nkb + jnp.maximum(i * r128 - 1, 0)

    def next_blk(c, i):
        return c * nkb + jnp.minimum((i + 1) * r128, nkb - 1)

    kern = functools.partial(_band_kernel, length=length, tqs=tqs)
    out_spec = pl.BlockSpec((tqs, A_GROUP_W), lambda c, i: ((c // dil) * nqb + i, c % dil))
    o, lse = pl.pallas_call(
        kern,
        grid=(ncls, nqb),
        in_specs=[
            pl.BlockSpec((A_GROUP_W, tqs), lambda c, i: (0, c * nqb + i)),
            pl.BlockSpec((V7X_LANES, A_GROUP_W), lambda c, i: (prev_blk(c, i), 0)),
            pl.BlockSpec((tqs, A_GROUP_W), lambda c, i: (c * nqb + i, 0)),
            pl.BlockSpec((V7X_LANES, A_GROUP_W), lambda c, i: (next_blk(c, i), 0)),
            pl.BlockSpec((A_GROUP_W, V7X_LANES), lambda c, i: (0, prev_blk(c, i))),
            pl.BlockSpec((A_GROUP_W, tqs), lambda c, i: (0, c * nqb + i)),
            pl.BlockSpec((A_GROUP_W, V7X_LANES), lambda c, i: (0, next_blk(c, i))),
        ],
        out_specs=[out_spec, out_spec],
        out_shape=[jax.ShapeDtypeStruct((t_total // dil, dil * A_GROUP_W), F32)] * 2,
        compiler_params=_params("parallel", "parallel"),
        name="band",
    )(qt, k, k, k, vt, vt, vt)
    return o.reshape(t_total, A_GROUP_W), lse.reshape(t_total, A_GROUP_W)


def _outproj_even_kernel(x_ref, o0_ref, o1_ref, o2_ref, l0_ref, l1_ref, l2_ref, obt_ref, wa_ref, wb_ref,
                         out_ref):
    l0, l1, l2 = l0_ref[...], l1_ref[...], l2_ref[...]
    m = jnp.maximum(jnp.maximum(l0, l1), l2)
    w0, w1, w2 = jnp.exp(l0 - m), jnp.exp(l1 - m), jnp.exp(l2 - m)
    oa = (w0 * o0_ref[...] + w1 * o1_ref[...] + w2 * o2_ref[...]) / (w0 + w1 + w2)
    acc = jnp.dot(oa.astype(BF16), wa_ref[...], preferred_element_type=F32)
    acc += lax.dot_general(obt_ref[...], wb_ref[...], (((0,), (0,)), ((), ())),
                           preferred_element_type=F32)
    out_ref[...] = x_ref[...] + acc


def _outproj_even(x2d, oa, lse, obt, wa, wb, *, tm):
    t_total = x2d.shape[0]
    row = lambda i: (i, 0)
    const = lambda i: (0, 0)
    a_spec = pl.BlockSpec((tm, A_GROUP_W), row)
    return pl.pallas_call(
        _outproj_even_kernel,
        grid=(t_total // tm,),
        in_specs=[pl.BlockSpec((tm, D_MODEL), row)] + [a_spec] * 6 + [
            pl.BlockSpec((B_V, tm), lambda i: (0, i)),
            pl.BlockSpec((A_GROUP_W, D_MODEL), const),
            pl.BlockSpec((B_V, D_MODEL), const),
        ],
        out_specs=pl.BlockSpec((tm, D_MODEL), row),
        out_shape=jax.ShapeDtypeStruct((t_total, D_MODEL), F32),
        compiler_params=_params("parallel"),
        name="outproj_even",
    )(x2d, *oa, *lse, obt, wa, wb)


def _outproj_odd_kernel(x_ref, ot_ref, w_ref, out_ref):
    acc = lax.dot_general(ot_ref[...], w_ref[...], (((0,), (0,)), ((), ())), preferred_element_type=F32)
    out_ref[...] = x_ref[...] + acc


def _outproj_odd(x2d, ot, w, *, tm):
    t_total = x2d.shape[0]
    n_in = ot.shape[0]
    return pl.pallas_call(
        _outproj_odd_kernel,
        grid=(t_total // tm,),
        in_specs=[
            pl.BlockSpec((tm, D_MODEL), lambda i: (i, 0)),
            pl.BlockSpec((n_in, tm), lambda i: (0, i)),
            pl.BlockSpec((n_in, D_MODEL), lambda i: (0, 0)),
        ],
        out_specs=pl.BlockSpec((tm, D_MODEL), lambda i: (i, 0)),
        out_shape=jax.ShapeDtypeStruct((t_total, D_MODEL), F32),
        compiler_params=_params("parallel"),
        name="outproj_odd",
    )(x2d, ot, w)


FF_HALO = 16


def _ffn_kernel(xp_ref, x_ref, xn_ref, g_ref, wup_ref, cw_ref, cb_ref, wd_ref, out_ref, hs_ref, acc_ref,
                *, seq, tm):
    i = pl.program_id(0)
    g = g_ref[...]
    has_prev = ((i * tm) % seq != 0).astype(F32)
    has_next = (((i + 1) * tm) % seq != 0).astype(F32)
    hs_ref[0:FF_HALO, :] = (_rms_rows(xp_ref[...], g) * has_prev).astype(BF16)
    hs_ref[FF_HALO:FF_HALO + tm, :] = _rms_rows(x_ref[...], g).astype(BF16)
    hs_ref[FF_HALO + tm:, :] = (_rms_rows(xn_ref[...], g) * has_next).astype(BF16)
    rows = tm + 2 * FF_HALO

    def conv(u, c0):
        w = cw_ref[:, c0:c0 + FF_CHUNK]
        b = cb_ref[:, c0:c0 + FF_CHUNK]
        um = pltpu.roll(u, 1, 0)[FF_HALO:FF_HALO + tm]
        up = pltpu.roll(u, rows - 1, 0)[FF_HALO:FF_HALO + tm]
        return um * w[0:1] + u[FF_HALO:FF_HALO + tm] * w[1:2] + up * w[2:3] + b

    for c in range(D_FF // FF_CHUNK):
        cg = c * FF_CHUNK
        cv = D_FF + c * FF_CHUNK
        hs = hs_ref[...]
        ug = jnp.dot(hs, wup_ref[:, cg:cg + FF_CHUNK], preferred_element_type=F32)
        uv = jnp.dot(hs, wup_ref[:, cv:cv + FF_CHUNK], preferred_element_type=F32)
        gate = conv(ug, cg)
        act = (gate * jax.nn.sigmoid(gate) * conv(uv, cv)).astype(BF16)
        part = jnp.dot(act, wd_ref[cg:cg + FF_CHUNK, :], preferred_element_type=F32)
        if c == 0:
            acc_ref[...] = part
        else:
            acc_ref[...] += part
    out_ref[...] = x_ref[...] + acc_ref[...]


def _ffn(x2d, g, wup, cw, cb, wd, *, seq, tm):
    t_total = x2d.shape[0]
    tm = min(tm, seq)
    r = tm // FF_HALO
    last = t_total // FF_HALO - 1
    const = lambda i: (0, 0)
    resident = dict(pipeline_mode=pl.Buffered(1))
    kern = functools.partial(_ffn_kernel, seq=seq, tm=tm)
    return pl.pallas_call(
        kern,
        grid=(t_total // tm,),
        in_specs=[
            pl.BlockSpec((FF_HALO, D_MODEL), lambda i: (jnp.maximum(i * r - 1, 0), 0)),
            pl.BlockSpec((tm, D_MODEL), lambda i: (i, 0)),
            pl.BlockSpec((FF_HALO, D_MODEL), lambda i: (jnp.minimum((i + 1) * r, last), 0)),
            pl.BlockSpec((1, D_MODEL), const),
            pl.BlockSpec((D_MODEL, 2 * D_FF), const, **resident),
            pl.BlockSpec((3, 2 * D_FF), const),
            pl.BlockSpec((1, 2 * D_FF), const),
            pl.BlockSpec((D_FF, D_MODEL), const, **resident),
        ],
        out_specs=pl.BlockSpec((tm, D_MODEL), lambda i: (i, 0)),
        out_shape=jax.ShapeDtypeStruct((t_total, D_MODEL), F32),
        scratch_shapes=[
            pltpu.VMEM((tm + 2 * FF_HALO, D_MODEL), BF16),
            pltpu.VMEM((tm, D_MODEL), F32),
        ],
        compiler_params=_params("parallel"),
        name="ffn",
    )(x2d, x2d, x2d, g, wup, cw, cb, wd)


def _rope_tables(pos, dim, theta):
    inv = theta ** (-jnp.arange(0, dim, 2, dtype=F32) / dim)
    ang = pos.astype(F32)[:, None] * inv[None, :]
    return jnp.cos(ang).T, jnp.sin(ang).T


def _class_order(table, dil):
    rows, seq = table.shape
    return table.reshape(rows, seq // dil, dil).transpose(0, 2, 1).reshape(rows, seq)


def _lane_bcast(v, width):
    return jnp.broadcast_to(v.astype(F32)[:, None], (v.shape[0], width))


TM = 512
TQ_C = 256
TQ_B = 512
TKC = 512
TQ_BAND = 512
Q_SCALE = HEAD_DIM ** -0.5 * LOG2E


def _trunk(x, norm_mix, norm_ffn, w_in_ab, q_norm_a, k_norm_a, q_norm_b, k_norm_b,
           lambda_q1, lambda_k1, lambda_q2, lambda_k2, subln_b, w_out_ab,
           w_in_c, q_norm_c, k_norm_c, w_out_c, w_up, conv_w, conv_b, w_down):
    batch, seq, _ = x.shape
    t_total = batch * seq
    depth = norm_mix.shape[0]
    x2d = x.reshape(t_total, D_MODEL)

    cos, sin = _rope_tables(jnp.arange(seq), ROT_DIM, ROPE_THETA)
    rows = seq // GRID_W
    row = jnp.repeat(jnp.arange(rows), GRID_W)
    col = jnp.tile(jnp.arange(GRID_W), rows)
    cr, sr = _rope_tables(row, AXIAL_DIM, AXIAL_THETA)
    cc, sc = _rope_tables(col, AXIAL_DIM, AXIAL_THETA)
    cos_ax = jnp.concatenate([cr, cc], axis=0)
    sin_ax = jnp.concatenate([sr, sc], axis=0)

    for i in range(depth):
        j = i // 2
        g_mix = norm_mix[i].reshape(1, D_MODEL)
        if i % 2 == 0:
            lam_init = 0.8 - 0.6 * math.exp(-0.3 * i)
            w = w_in_ab[j]
            oa, lse = [], []
            gains_a = jnp.concatenate([_lane_bcast(q_norm_a[j] * Q_SCALE, TM), _lane_bcast(k_norm_a[j], TM)], 0)
            for gi, (_, dil) in enumerate(A_PATTERNS):
                sl = slice(gi * A_GROUP_W, (gi + 1) * A_GROUP_W)
                wt = jnp.concatenate([w[:, sl], w[:, A_QKV:2 * A_QKV][:, sl], w[:, 2 * A_QKV:3 * A_QKV][:, sl]],
                                     axis=1).T.astype(BF16)
                qt, k, vt = _proj(x2d, g_mix, wt, gains_a, _class_order(cos, dil), _class_order(sin, dil),
                                  seq=seq, dil=dil, nq=A_HEADS_PER_GROUP, nk=A_HEADS_PER_GROUP, nv=A_GROUP_W,
                                  axial=False, tm=TM)
                o_g, lse_g = _band(qt, k, vt, batch=batch, seq=seq, dil=dil, tqs=TQ_BAND)
                oa.append(o_g)
                lse.append(lse_g)
            wt_b = w[:, 3 * A_QKV:].T.astype(BF16)
            gains_b = jnp.concatenate([_lane_bcast(q_norm_b[j] * Q_SCALE, TM), _lane_bcast(k_norm_b[j], TM)], 0)
            qt, k, vt = _proj(x2d, g_mix, wt_b, gains_b, cos, sin, seq=seq, dil=1,
                              nq=2 * B_HEADS, nk=2 * B_HEADS, nv=B_V, axial=False, tm=TM)
            lam = (jnp.exp(jnp.sum(lambda_q1[j].astype(F32) * lambda_k1[j].astype(F32)))
                   - jnp.exp(jnp.sum(lambda_q2[j].astype(F32) * lambda_k2[j].astype(F32))) + lam_init)
            sg = _lane_bcast(subln_b[j] * (1.0 - lam_init), TQ_B)
            obt = _attn_b(lam.reshape(1).astype(F32), qt, k, vt, sg, batch=batch, seq=seq, tq=TQ_B, tkc=TKC)
            wo = w_out_ab[j].astype(BF16)
            x2d = _outproj_even(x2d, oa, lse, obt, wo[:A_GROUP_W], wo[A_GROUP_W:], tm=TM)
        else:
            wt = w_in_c[j].T.astype(BF16)
            gains_c = jnp.concatenate([_lane_bcast(q_norm_c[j] * Q_SCALE, TM), _lane_bcast(k_norm_c[j], TM)], 0)
            qt, k, vt = _proj(x2d, g_mix, wt, gains_c, cos_ax, sin_ax, seq=seq, dil=1,
                              nq=C_Q_HEADS, nk=C_KV_HEADS, nv=C_KV_HEADS * HEAD_DIM, axial=True, tm=TM)
            ot = _attn_c(qt, k, vt, batch=batch, seq=seq, tq=TQ_C, tkc=TKC)
            x2d = _outproj_odd(x2d, ot, w_out_c[j].astype(BF16), tm=TM)
        x2d = _ffn(x2d, norm_ffn[i].reshape(1, D_MODEL), w_up[i].astype(BF16), conv_w[i],
                   conv_b[i].reshape(1, 2 * D_FF), w_down[i].astype(BF16), seq=seq, tm=TM)
    return x2d.reshape(batch, seq, D_MODEL)


def kernel(x_prompt, x_sample, norm_mix, norm_ffn, w_in_ab, q_norm_a, k_norm_a, q_norm_b, k_norm_b,
           lambda_q1, lambda_k1, lambda_q2, lambda_k2, subln_b, w_out_ab, w_in_c, q_norm_c, k_norm_c,
           w_out_c, w_up, conv_w, conv_b, w_down):
    params = (norm_mix, norm_ffn, w_in_ab, q_norm_a, k_norm_a, q_norm_b, k_norm_b,
              lambda_q1, lambda_k1, lambda_q2, lambda_k2, subln_b, w_out_ab,
              w_in_c, q_norm_c, k_norm_c, w_out_c, w_up, conv_w, conv_b, w_down)
    return (_trunk(x_prompt, *params), _trunk(x_sample, *params))
```

```python
import functools
import math

import jax
import jax.numpy as jnp
from jax import lax
from jax.experimental import pallas as pl
from jax.experimental.pallas import tpu as pltpu

F32 = jnp.float32
BF16 = jnp.bfloat16

D_MODEL = 1024
HEAD_DIM = 64
A_PATTERNS = ((128, 1), (512, 4), (2048, 16))
A_GROUPS = 3
A_HEADS_PER_GROUP = 4
A_QKV = A_GROUPS * A_HEADS_PER_GROUP * HEAD_DIM
A_GROUP_W = A_HEADS_PER_GROUP * HEAD_DIM
B_HEADS = 4
B_QK = B_HEADS * 2 * HEAD_DIM
B_V = B_HEADS * 2 * HEAD_DIM
C_Q_HEADS = 16
C_KV_HEADS = 4
C_GROUP = C_Q_HEADS // C_KV_HEADS
ROPE_THETA = 500000.0
ROT_DIM = HEAD_DIM // 4
AXIAL_THETA = 10000.0
AXIAL_DIM = HEAD_DIM // 2
GRID_W = 64
D_FF = 2816
NORM_EPS = 1e-6
SUBLN_EPS = 1e-5
NEG_INF = -1e30
LOG2E = 1.4426950408889634
LN2 = 0.6931471805599453

V7X_LANES = 128
V7X_VMEM_BYTES = 64 * 1024 * 1024
VMEM_LIMIT = V7X_VMEM_BYTES - 8 * 1024 * 1024
BAND_HALF = 64
BAND_SUB = 128
BAND_WIN = BAND_SUB + 2 * V7X_LANES
FF_CHUNK = 256


def _params(*semantics):
    return pltpu.CompilerParams(dimension_semantics=semantics, vmem_limit_bytes=VMEM_LIMIT)


def _rms_rows(x, g):
    ms = jnp.mean(x * x, axis=-1, keepdims=True)
    return x * lax.rsqrt(ms + NORM_EPS) * g


def _head_norm_rot(y, gain, cos, sin, axial):
    ssq = jnp.sum(y * y, axis=0, keepdims=True)
    yn = y * lax.rsqrt(ssq * (1.0 / HEAD_DIM) + NORM_EPS) * gain
    if axial:
        h = AXIAL_DIM // 2
        a1, a2, b1, b2 = yn[0:h], yn[h:2 * h], yn[2 * h:3 * h], yn[3 * h:4 * h]
        cr, cc = cos[0:h], cos[h:2 * h]
        sr, sc = sin[0:h], sin[h:2 * h]
        return jnp.concatenate(
            [a1 * cr - a2 * sr, a2 * cr + a1 * sr, b1 * cc - b2 * sc, b2 * cc + b1 * sc], axis=0)
    h = ROT_DIM // 2
    x1, x2 = yn[0:h], yn[h:2 * h]
    return jnp.concatenate([x1 * cos - x2 * sin, x2 * cos + x1 * sin, yn[2 * h:]], axis=0)


def _proj_kernel(x_ref, g_ref, wt_ref, gain_ref, cos_ref, sin_ref, qt_ref, k_ref, vt_ref,
                 *, nq, nk, axial):
    h = _rms_rows(x_ref[...], g_ref[...]).astype(BF16)
    yt = lax.dot_general(wt_ref[...], h, (((1,), (1,)), ((), ())),
                         preferred_element_type=F32)
    cos = cos_ref[...]
    sin = sin_ref[...]
    gq = gain_ref[0:HEAD_DIM, :]
    gk = gain_ref[HEAD_DIM:2 * HEAD_DIM, :]
    for i in range(nq):
        y = yt[i * HEAD_DIM:(i + 1) * HEAD_DIM, :]
        qt_ref[i * HEAD_DIM:(i + 1) * HEAD_DIM, :] = _head_norm_rot(y, gq, cos, sin, axial).astype(BF16)
    base = nq * HEAD_DIM
    for p in range(nk // 2):
        pair = []
        for i in (2 * p, 2 * p + 1):
            y = yt[base + i * HEAD_DIM:base + (i + 1) * HEAD_DIM, :]
            pair.append(_head_norm_rot(y, gk, cos, sin, axial))
        kt = jnp.concatenate(pair, axis=0)
        k_ref[:, p * 2 * HEAD_DIM:(p + 1) * 2 * HEAD_DIM] = kt.T.astype(BF16)
    base = (nq + nk) * HEAD_DIM
    vt_ref[...] = yt[base:, :].astype(BF16)


def _proj(x2d, g, wt, gains, cos_t, sin_t, *, seq, dil, nq, nk, nv, axial, tm):
    t_total = x2d.shape[0]
    length = seq // dil
    tm = min(tm, length)
    nlb = length // tm
    nrow = t_total // dil // tm
    xv = x2d.reshape(t_total // dil, dil * D_MODEL)
    n_out = (nq + nk) * HEAD_DIM + nv
    rot_rows = cos_t.shape[0]

    def col_block(i, r):
        return ((i // nlb) * dil + r) * nlb + (i % nlb)

    kern = functools.partial(_proj_kernel, nq=nq, nk=nk, axial=axial)
    return pl.pallas_call(
        kern,
        grid=(nrow, dil),
        in_specs=[
            pl.BlockSpec((tm, D_MODEL), lambda i, r: (i, r)),
            pl.BlockSpec((1, D_MODEL), lambda i, r: (0, 0)),
            pl.BlockSpec((n_out, D_MODEL), lambda i, r: (0, 0)),
            pl.BlockSpec((2 * HEAD_DIM, tm), lambda i, r: (0, 0)),
            pl.BlockSpec((rot_rows, tm), lambda i, r: (0, r * nlb + i % nlb)),
            pl.BlockSpec((rot_rows, tm), lambda i, r: (0, r * nlb + i % nlb)),
        ],
        out_specs=[
            pl.BlockSpec((nq * HEAD_DIM, tm), lambda i, r: (0, col_block(i, r))),
            pl.BlockSpec((tm, nk * HEAD_DIM), lambda i, r: (col_block(i, r), 0)),
            pl.BlockSpec((nv, tm), lambda i, r: (0, col_block(i, r))),
        ],
        out_shape=[
            jax.ShapeDtypeStruct((nq * HEAD_DIM, t_total), BF16),
            jax.ShapeDtypeStruct((t_total, nk * HEAD_DIM), BF16),
            jax.ShapeDtypeStruct((nv, t_total), BF16),
        ],
        compiler_params=_params("parallel", "parallel"),
        name="proj",
    )(xv, g, wt, gains[:, :tm], cos_t, sin_t)


FLASH_COL_TILE = 256
V_PAD = 16


def _flash_scratch(dv, ncols, seq, tkc):
    return [
        pltpu.VMEM((2 * HEAD_DIM, ncols), BF16),
        pltpu.VMEM((dv + V_PAD, seq), BF16),
        pltpu.VMEM((dv + V_PAD, ncols), F32),
        pltpu.VMEM((1, ncols), F32),
        pltpu.VMEM((tkc, ncols), F32), pltpu.VMEM((tkc, ncols), F32),
        pltpu.VMEM((tkc, ncols), BF16), pltpu.VMEM((tkc, ncols), BF16),
        pltpu.VMEM((1, ncols), F32), pltpu.VMEM((1, ncols), F32),
        pltpu.VMEM((1, ncols), F32), pltpu.VMEM((1, ncols), F32),
    ]


def _flash_unit(k_ref, vt_ref, scratch, *, dv, seq, tkc):
    qpad_ref, vext_ref, acc_ref, m_ref, s0, s1, p0, p1, a0, a1, c0, c1 = scratch
    s_bufs, p_bufs, a_bufs, cm_bufs = (s0, s1), (p0, p1), (a0, a1), (c0, c1)
    nc = seq // tkc

    vext_ref[0:dv, :] = vt_ref[...]
    ones_row = lax.broadcasted_iota(jnp.int32, (V_PAD, seq), 0) == 0
    vext_ref[dv:, :] = jnp.where(ones_row, 1.0, 0.0).astype(BF16)
    m_ref[...] = jnp.full(m_ref.shape, NEG_INF, F32)
    acc_ref[...] = jnp.zeros(acc_ref.shape, F32)

    def chunk(c):
        return pl.ds(c * tkc if isinstance(c, int) else pl.multiple_of(c * tkc, tkc), tkc)

    ncols = qpad_ref.shape[1]
    tiles = [slice(j * FLASH_COL_TILE, (j + 1) * FLASH_COL_TILE) for j in range(ncols // FLASH_COL_TILE)]

    def scores(c, slot, t):
        s = jnp.dot(k_ref[chunk(c), :], qpad_ref[:, t], preferred_element_type=F32)
        s_bufs[slot][:, t] = s
        cm_bufs[slot][:, t] = jnp.max(s, axis=0, keepdims=True)

    def softmax(slot, t):
        m_old = m_ref[:, t]
        m_new = jnp.maximum(m_old, cm_bufs[slot][:, t])
        a_bufs[slot][:, t] = jnp.exp2(m_old - m_new)
        m_ref[:, t] = m_new
        p_bufs[slot][:, t] = jnp.exp2((s_bufs[slot][:, t] - m_new).astype(BF16))

    def accumulate(c, slot, t):
        pv = jnp.dot(vext_ref[:, chunk(c)], p_bufs[slot][:, t], preferred_element_type=F32)
        acc_ref[:, t] = a_bufs[slot][:, t] * acc_ref[:, t] + pv

    def step(c, slot, first=False, last=False):
        for t in tiles:
            if not last:
                scores(c + 1, 1 - slot, t)
            softmax(slot, t)
            if not first:
                accumulate(c - 1, 1 - slot, t)

    for t in tiles:
        scores(0, 0, t)
    if nc <= 4:
        for c in range(nc):
            step(c, c % 2, first=(c == 0), last=(c == nc - 1))
    else:
        step(0, 0, first=True)
        step(1, 1)

        def pair(t, carry):
            step(2 * t, 0)
            step(2 * t + 1, 1)
            return carry

        lax.fori_loop(1, nc // 2 - 1, pair, 0)
        step(nc - 2, 0)
        step(nc - 1, 1, last=True)
    for t in tiles:
        accumulate(nc - 1, (nc - 1) % 2, t)


def _attn_c_kernel(qt_ref, k_ref, vt_ref, o_ref, *scratch, seq, tq, tkc):
    n = pl.program_id(1)
    qpad_ref, acc_ref = scratch[0], scratch[2]
    qcat = jnp.concatenate([qt_ref[g * HEAD_DIM:(g + 1) * HEAD_DIM, :] for g in range(C_GROUP)], axis=1)
    zeros = jnp.zeros_like(qcat)

    @pl.when(n % 2 == 0)
    def _():
        qpad_ref[0:HEAD_DIM, :] = qcat
        qpad_ref[HEAD_DIM:, :] = zeros

    @pl.when(n % 2 == 1)
    def _():
        qpad_ref[0:HEAD_DIM, :] = zeros
        qpad_ref[HEAD_DIM:, :] = qcat

    _flash_unit(k_ref, vt_ref, scratch, dv=HEAD_DIM, seq=seq, tkc=tkc)
    o = acc_ref[0:HEAD_DIM, :] * (1.0 / acc_ref[HEAD_DIM:HEAD_DIM + 1, :])
    for g in range(C_GROUP):
        o_ref[g * HEAD_DIM:(g + 1) * HEAD_DIM, :] = o[:, g * tq:(g + 1) * tq].astype(BF16)


def _attn_c(qt, k, vt, *, batch, seq, tq, tkc):
    t_total = batch * seq
    tq = min(tq, seq)
    tkc = min(tkc, seq)
    nqb = seq // tq
    ncols = C_GROUP * tq
    kern = functools.partial(_attn_c_kernel, seq=seq, tq=tq, tkc=tkc)
    return pl.pallas_call(
        kern,
        grid=(batch, C_KV_HEADS, nqb),
        in_specs=[
            pl.BlockSpec((C_GROUP * HEAD_DIM, tq), lambda b, n, i: (n, b * nqb + i)),
            pl.BlockSpec((seq, 2 * HEAD_DIM), lambda b, n, i: (b, n // 2)),
            pl.BlockSpec((HEAD_DIM, seq), lambda b, n, i: (n, b)),
        ],
        out_specs=pl.BlockSpec((C_GROUP * HEAD_DIM, tq), lambda b, n, i: (n, b * nqb + i)),
        out_shape=jax.ShapeDtypeStruct((C_Q_HEADS * HEAD_DIM, t_total), BF16),
        scratch_shapes=_flash_scratch(HEAD_DIM, ncols, seq, tkc),
        compiler_params=_params("parallel", "parallel", "parallel"),
        name="attn_c",
    )(qt, k, vt)


def _attn_b_kernel(lam_ref, qt_ref, k_ref, vt_ref, sg_ref, o_ref, *scratch, seq, tq, tkc):
    dv = 2 * HEAD_DIM
    qpad_ref, acc_ref = scratch[0], scratch[2]
    q = qt_ref[...]
    zeros = jnp.zeros((HEAD_DIM, tq), BF16)
    qpad_ref[0:HEAD_DIM, 0:tq] = q[0:HEAD_DIM]
    qpad_ref[0:HEAD_DIM, tq:] = zeros
    qpad_ref[HEAD_DIM:, 0:tq] = zeros
    qpad_ref[HEAD_DIM:, tq:] = q[HEAD_DIM:]
    _flash_unit(k_ref, vt_ref, scratch, dv=dv, seq=seq, tkc=tkc)
    on = acc_ref[0:dv, :] * (1.0 / acc_ref[dv:dv + 1, :])
    o = on[:, 0:tq] - lam_ref[0] * on[:, tq:]
    ms = jnp.mean(o * o, axis=0, keepdims=True)
    o_ref[...] = (o * lax.rsqrt(ms + SUBLN_EPS) * sg_ref[...]).astype(BF16)


def _attn_b(lam, qt, k, vt, sg, *, batch, seq, tq, tkc):
    t_total = batch * seq
    tq = min(tq, seq)
    tkc = min(tkc, seq)
    nqb = seq // tq
    kern = functools.partial(_attn_b_kernel, seq=seq, tq=tq, tkc=tkc)
    return pl.pallas_call(
        kern,
        grid=(batch, B_HEADS, nqb),
        in_specs=[
            pl.BlockSpec(memory_space=pltpu.SMEM),
            pl.BlockSpec((2 * HEAD_DIM, tq), lambda b, h, i: (h, b * nqb + i)),
            pl.BlockSpec((seq, 2 * HEAD_DIM), lambda b, h, i: (b, h)),
            pl.BlockSpec((2 * HEAD_DIM, seq), lambda b, h, i: (h, b)),
            pl.BlockSpec((2 * HEAD_DIM, tq), lambda b, h, i: (0, 0)),
        ],
        out_specs=pl.BlockSpec((2 * HEAD_DIM, tq), lambda b, h, i: (h, b * nqb + i)),
        out_shape=jax.ShapeDtypeStruct((B_V, t_total), BF16),
        scratch_shapes=_flash_scratch(2 * HEAD_DIM, 2 * tq, seq, tkc),
        compiler_params=_params("parallel", "parallel", "parallel"),
        name="attn_b",
    )(lam, qt, k, vt, sg[:, :tq])


def _band_kernel(qt_ref, kp_ref, km_ref, kn_ref, vp_ref, vm_ref, vn_ref, o_ref, lse_ref, *, length, tqs):
    l0 = pl.program_id(1) * tqs
    kwin = jnp.concatenate([kp_ref[...], km_ref[...], kn_ref[...]], axis=0)
    vwin = jnp.concatenate([vp_ref[...], vm_ref[...], vn_ref[...]], axis=1)
    row = lax.broadcasted_iota(jnp.int32, (BAND_WIN, BAND_SUB), 0)
    col = lax.broadcasted_iota(jnp.int32, (BAND_WIN, BAND_SUB), 1)
    rel = row - V7X_LANES - col
    in_band = jnp.abs(rel) <= BAND_HALF
    zeros = jnp.zeros((HEAD_DIM, BAND_SUB), BF16)
    for j in range(tqs // BAND_SUB):
        c0 = j * BAND_SUB
        q = qt_ref[:, c0:c0 + BAND_SUB]
        kw = kwin[c0:c0 + BAND_WIN]
        vw = vwin[:, c0:c0 + BAND_WIN]
        kpos = l0 + (c0 - V7X_LANES) + row
        valid = in_band & (kpos >= 0) & (kpos < length)
        valid2 = jnp.concatenate([valid, valid], axis=1)
        outs, lses = [], []
        for p in range(A_HEADS_PER_GROUP // 2):
            ha, hb = 2 * p, 2 * p + 1
            qa = q[ha * HEAD_DIM:(ha + 1) * HEAD_DIM]
            qb = q[hb * HEAD_DIM:(hb + 1) * HEAD_DIM]
            qpad = jnp.concatenate([jnp.concatenate([qa, zeros], axis=1),
                                    jnp.concatenate([zeros, qb], axis=1)], axis=0)
            s = jnp.dot(kw[:, p * 2 * HEAD_DIM:(p + 1) * 2 * HEAD_DIM], qpad,
                        preferred_element_type=F32)
            s = jnp.where(valid2, s, NEG_INF)
            m = jnp.max(s, axis=0, keepdims=True)
            pr = jnp.exp2(s - m)
            l = jnp.sum(pr, axis=0, keepdims=True)
            pb = pr.astype(BF16)
            oa = jnp.dot(vw[ha * HEAD_DIM:(ha + 1) * HEAD_DIM], pb[:, :BAND_SUB], preferred_element_type=F32)
            ob = jnp.dot(vw[hb * HEAD_DIM:(hb + 1) * HEAD_DIM], pb[:, BAND_SUB:], preferred_element_type=F32)
            inv = 1.0 / l
            lse = m * LN2 + jnp.log(l)
            outs += [oa * inv[:, :BAND_SUB], ob * inv[:, BAND_SUB:]]
            lses += [jnp.broadcast_to(lse[:, :BAND_SUB], (HEAD_DIM, BAND_SUB)),
                     jnp.broadcast_to(lse[:, BAND_SUB:], (HEAD_DIM, BAND_SUB))]
        o_ref[c0:c0 + BAND_SUB, :] = jnp.concatenate(outs, axis=0).T
        lse_ref[c0:c0 + BAND_SUB, :] = jnp.concatenate(lses, axis=0).T


def _band(qt, k, vt, *, batch, seq, dil, tqs):
    t_total = batch * seq
    length = seq // dil
    tqs = min(tqs, length)
    nqb = length // tqs
    ncls = batch * dil
    nkb = length // V7X_LANES
    r128 = tqs // V7X_LANES

    def prev_blk(c, i):
        return c * nkb + jnp.maximum(i * r128 - 1, 0)

    def next_blk(c, i):
        return c * nkb + jnp.minimum((i + 1) * r128, nkb - 1)

    kern = functools.partial(_band_kernel, length=length, tqs=tqs)
    out_spec = pl.BlockSpec((tqs, A_GROUP_W), lambda c, i: ((c // dil) * nqb + i, c % dil))
    o, lse = pl.pallas_call(
        kern,
        grid=(ncls, nqb),
        in_specs=[
            pl.BlockSpec((A_GROUP_W, tqs), lambda c, i: (0, c * nqb + i)),
            pl.BlockSpec((V7X_LANES, A_GROUP_W), lambda c, i: (prev_blk(c, i), 0)),
            pl.BlockSpec((tqs, A_GROUP_W), lambda c, i: (c * nqb + i, 0)),
            pl.BlockSpec((V7X_LANES, A_GROUP_W), lambda c, i: (next_blk(c, i), 0)),
            pl.BlockSpec((A_GROUP_W, V7X_LANES), lambda c, i: (0, prev_blk(c, i))),
            pl.BlockSpec((A_GROUP_W, tqs), lambda c, i: (0, c * nqb + i)),
            pl.BlockSpec((A_GROUP_W, V7X_LANES), lambda c, i: (0, next_blk(c, i))),
        ],
        out_specs=[out_spec, out_spec],
        out_shape=[jax.ShapeDtypeStruct((t_total // dil, dil * A_GROUP_W), F32)] * 2,
        compiler_params=_params("parallel", "parallel"),
        name="band",
    )(qt, k, k, k, vt, vt, vt)
    return o.reshape(t_total, A_GROUP_W), lse.reshape(t_total, A_GROUP_W)


def _outproj_even_kernel(x_ref, o0_ref, o1_ref, o2_ref, l0_ref, l1_ref, l2_ref, obt_ref, wa_ref, wb_ref,
                         out_ref):
    l0, l1, l2 = l0_ref[...], l1_ref[...], l2_ref[...]
    m = jnp.maximum(jnp.maximum(l0, l1), l2)
    w0, w1, w2 = jnp.exp(l0 - m), jnp.exp(l1 - m), jnp.exp(l2 - m)
    oa = (w0 * o0_ref[...] + w1 * o1_ref[...] + w2 * o2_ref[...]) / (w0 + w1 + w2)
    acc = jnp.dot(oa.astype(BF16), wa_ref[...], preferred_element_type=F32)
    acc += lax.dot_general(obt_ref[...], wb_ref[...], (((0,), (0,)), ((), ())),
                           preferred_element_type=F32)
    out_ref[...] = x_ref[...] + acc


def _outproj_even(x2d, oa, lse, obt, wa, wb, *, tm):
    t_total = x2d.shape[0]
    row = lambda i: (i, 0)
    const = lambda i: (0, 0)
    a_spec = pl.BlockSpec((tm, A_GROUP_W), row)
    return pl.pallas_call(
        _outproj_even_kernel,
        grid=(t_total // tm,),
        in_specs=[pl.BlockSpec((tm, D_MODEL), row)] + [a_spec] * 6 + [
            pl.BlockSpec((B_V, tm), lambda i: (0, i)),
            pl.BlockSpec((A_GROUP_W, D_MODEL), const),
            pl.BlockSpec((B_V, D_MODEL), const),
        ],
        out_specs=pl.BlockSpec((tm, D_MODEL), row),
        out_shape=jax.ShapeDtypeStruct((t_total, D_MODEL), F32),
        compiler_params=_params("parallel"),
        name="outproj_even",
    )(x2d, *oa, *lse, obt, wa, wb)


def _outproj_odd_kernel(x_ref, ot_ref, w_ref, out_ref):
    acc = lax.dot_general(ot_ref[...], w_ref[...], (((0,), (0,)), ((), ())), preferred_element_type=F32)
    out_ref[...] = x_ref[...] + acc


def _outproj_odd(x2d, ot, w, *, tm):
    t_total = x2d.shape[0]
    n_in = ot.shape[0]
    return pl.pallas_call(
        _outproj_odd_kernel,
        grid=(t_total // tm,),
        in_specs=[
            pl.BlockSpec((tm, D_MODEL), lambda i: (i, 0)),
            pl.BlockSpec((n_in, tm), lambda i: (0, i)),
            pl.BlockSpec((n_in, D_MODEL), lambda i: (0, 0)),
        ],
        out_specs=pl.BlockSpec((tm, D_MODEL), lambda i: (i, 0)),
        out_shape=jax.ShapeDtypeStruct((t_total, D_MODEL), F32),
        compiler_params=_params("parallel"),
        name="outproj_odd",
    )(x2d, ot, w)


FF_HALO = 16


def _ffn_kernel(xp_ref, x_ref, xn_ref, g_ref, wup_ref, cw_ref, cb_ref, wd_ref, out_ref, hs_ref, acc_ref,
                *, seq, tm):
    i = pl.program_id(0)
    g = g_ref[...]
    has_prev = ((i * tm) % seq != 0).astype(F32)
    has_next = (((i + 1) * tm) % seq != 0).astype(F32)
    hs_ref[0:FF_HALO, :] = (_rms_rows(xp_ref[...], g) * has_prev).astype(BF16)
    hs_ref[FF_HALO:FF_HALO + tm, :] = _rms_rows(x_ref[...], g).astype(BF16)
    hs_ref[FF_HALO + tm:, :] = (_rms_rows(xn_ref[...], g) * has_next).astype(BF16)
    rows = tm + 2 * FF_HALO

    def conv(u, c0):
        w = cw_ref[:, c0:c0 + FF_CHUNK]
        b = cb_ref[:, c0:c0 + FF_CHUNK]
        um = pltpu.roll(u, 1, 0)[FF_HALO:FF_HALO + tm]
        up = pltpu.roll(u, rows - 1, 0)[FF_HALO:FF_HALO + tm]
        return um * w[0:1] + u[FF_HALO:FF_HALO + tm] * w[1:2] + up * w[2:3] + b

    for c in range(D_FF // FF_CHUNK):
        cg = c * FF_CHUNK
        cv = D_FF + c * FF_CHUNK
        hs = hs_ref[...]
        ug = jnp.dot(hs, wup_ref[:, cg:cg + FF_CHUNK], preferred_element_type=F32)
        uv = jnp.dot(hs, wup_ref[:, cv:cv + FF_CHUNK], preferred_element_type=F32)
        gate = conv(ug, cg)
        act = (gate * jax.nn.sigmoid(gate) * conv(uv, cv)).astype(BF16)
        part = jnp.dot(act, wd_ref[cg:cg + FF_CHUNK, :], preferred_element_type=F32)
        if c == 0:
            acc_ref[...] = part
        else:
            acc_ref[...] += part
    out_ref[...] = x_ref[...] + acc_ref[...]


def _ffn(x2d, g, wup, cw, cb, wd, *, seq, tm):
    t_total = x2d.shape[0]
    tm = min(tm, seq)
    r = tm // FF_HALO
    last = t_total // FF_HALO - 1
    const = lambda i: (0, 0)
    resident = dict(pipeline_mode=pl.Buffered(1))
    kern = functools.partial(_ffn_kernel, seq=seq, tm=tm)
    return pl.pallas_call(
        kern,
        grid=(t_total // tm,),
        in_specs=[
            pl.BlockSpec((FF_HALO, D_MODEL), lambda i: (jnp.maximum(i * r - 1, 0), 0)),
            pl.BlockSpec((tm, D_MODEL), lambda i: (i, 0)),
            pl.BlockSpec((FF_HALO, D_MODEL), lambda i: (jnp.minimum((i + 1) * r, last), 0)),
            pl.BlockSpec((1, D_MODEL), const),
            pl.BlockSpec((D_MODEL, 2 * D_FF), const, **resident),
            pl.BlockSpec((3, 2 * D_FF), const),
            pl.BlockSpec((1, 2 * D_FF), const),
            pl.BlockSpec((D_FF, D_MODEL), const, **resident),
        ],
        out_specs=pl.BlockSpec((tm, D_MODEL), lambda i: (i, 0)),
        out_shape=jax.ShapeDtypeStruct((t_total, D_MODEL), F32),
        scratch_shapes=[
            pltpu.VMEM((tm + 2 * FF_HALO, D_MODEL), BF16),
            pltpu.VMEM((tm, D_MODEL), F32),
        ],
        compiler_params=_params("parallel"),
        name="ffn",
    )(x2d, x2d, x2d, g, wup, cw, cb, wd)


def _rope_tables(pos, dim, theta):
    inv = theta ** (-jnp.arange(0, dim, 2, dtype=F32) / dim)
    ang = pos.astype(F32)[:, None] * inv[None, :]
    return jnp.cos(ang).T, jnp.sin(ang).T


def _class_order(table, dil):
    rows, seq = table.shape
    return table.reshape(rows, seq // dil, dil).transpose(0, 2, 1).reshape(rows, seq)


def _lane_bcast(v, width):
    return jnp.broadcast_to(v.astype(F32)[:, None], (v.shape[0], width))


TM = 512
TQ_C = 256
TQ_B = 512
TKC = 512
TQ_BAND = 512
Q_SCALE = HEAD_DIM ** -0.5 * LOG2E


def _trunk(x, norm_mix, norm_ffn, w_in_ab, q_norm_a, k_norm_a, q_norm_b, k_norm_b,
           lambda_q1, lambda_k1, lambda_q2, lambda_k2, subln_b, w_out_ab,
           w_in_c, q_norm_c, k_norm_c, w_out_c, w_up, conv_w, conv_b, w_down):
    batch, seq, _ = x.shape
    t_total = batch * seq
    depth = norm_mix.shape[0]
    x2d = x.reshape(t_total, D_MODEL)

    cos, sin = _rope_tables(jnp.arange(seq), ROT_DIM, ROPE_THETA)
    rows = seq // GRID_W
    row = jnp.repeat(jnp.arange(rows), GRID_W)
    col = jnp.tile(jnp.arange(GRID_W), rows)
    cr, sr = _rope_tables(row, AXIAL_DIM, AXIAL_THETA)
    cc, sc = _rope_tables(col, AXIAL_DIM, AXIAL_THETA)
    cos_ax = jnp.concatenate([cr, cc], axis=0)
    sin_ax = jnp.concatenate([sr, sc], axis=0)

    for i in range(depth):
        j = i // 2
        g_mix = norm_mix[i].reshape(1, D_MODEL)
        if i % 2 == 0:
            lam_init = 0.8 - 0.6 * math.exp(-0.3 * i)
            w = w_in_ab[j]
            oa, lse = [], []
            gains_a = jnp.concatenate([_lane_bcast(q_norm_a[j] * Q_SCALE, TM), _lane_bcast(k_norm_a[j], TM)], 0)
            for gi, (_, dil) in enumerate(A_PATTERNS):
                sl = slice(gi * A_GROUP_W, (gi + 1) * A_GROUP_W)
                wt = jnp.concatenate([w[:, sl], w[:, A_QKV:2 * A_QKV][:, sl], w[:, 2 * A_QKV:3 * A_QKV][:, sl]],
                                     axis=1).T.astype(BF16)
                qt, k, vt = _proj(x2d, g_mix, wt, gains_a, _class_order(cos, dil), _class_order(sin, dil),
                                  seq=seq, dil=dil, nq=A_HEADS_PER_GROUP, nk=A_HEADS_PER_GROUP, nv=A_GROUP_W,
                                  axial=False, tm=TM)
                o_g, lse_g = _band(qt, k, vt, batch=batch, seq=seq, dil=dil, tqs=TQ_BAND)
                oa.append(o_g)
                lse.append(lse_g)
            wt_b = w[:, 3 * A_QKV:].T.astype(BF16)
            gains_b = jnp.concatenate([_lane_bcast(q_norm_b[j] * Q_SCALE, TM), _lane_bcast(k_norm_b[j], TM)], 0)
            qt, k, vt = _proj(x2d, g_mix, wt_b, gains_b, cos, sin, seq=seq, dil=1,
                              nq=2 * B_HEADS, nk=2 * B_HEADS, nv=B_V, axial=False, tm=TM)
            lam = (jnp.exp(jnp.sum(lambda_q1[j].astype(F32) * lambda_k1[j].astype(F32)))
                   - jnp.exp(jnp.sum(lambda_q2[j].astype(F32) * lambda_k2[j].astype(F32))) + lam_init)
            sg = _lane_bcast(subln_b[j] * (1.0 - lam_init), TQ_B)
            obt = _attn_b(lam.reshape(1).astype(F32), qt, k, vt, sg, batch=batch, seq=seq, tq=TQ_B, tkc=TKC)
            wo = w_out_ab[j].astype(BF16)
            x2d = _outproj_even(x2d, oa, lse, obt, wo[:A_GROUP_W], wo[A_GROUP_W:], tm=TM)
        else:
            wt = w_in_c[j].T.astype(BF16)
            gains_c = jnp.concatenate([_lane_bcast(q_norm_c[j] * Q_SCALE, TM), _lane_bcast(k_norm_c[j], TM)], 0)
            qt, k, vt = _proj(x2d, g_mix, wt, gains_c, cos_ax, sin_ax, seq=seq, dil=1,
                              nq=C_Q_HEADS, nk=C_KV_HEADS, nv=C_KV_HEADS * HEAD_DIM, axial=True, tm=TM)
            ot = _attn_c(qt, k, vt, batch=batch, seq=seq, tq=TQ_C, tkc=TKC)
            x2d = _outproj_odd(x2d, ot, w_out_c[j].astype(BF16), tm=TM)
        x2d = _ffn(x2d, norm_ffn[i].reshape(1, D_MODEL), w_up[i].astype(BF16), conv_w[i],
                   conv_b[i].reshape(1, 2 * D_FF), w_down[i].astype(BF16), seq=seq, tm=TM)
    return x2d.reshape(batch, seq, D_MODEL)


def kernel(x_prompt, x_sample, norm_mix, norm_ffn, w_in_ab, q_norm_a, k_norm_a, q_norm_b, k_norm_b,
           lambda_q1, lambda_k1, lambda_q2, lambda_k2, subln_b, w_out_ab, w_in_c, q_norm_c, k_norm_c,
           w_out_c, w_up, conv_w, conv_b, w_down):
    params = (norm_mix, norm_ffn, w_in_ab, q_norm_a, k_norm_a, q_norm_b, k_norm_b,
              lambda_q1, lambda_k1, lambda_q2, lambda_k2, subln_b, w_out_ab,
              w_in_c, q_norm_c, k_norm_c, w_out_c, w_up, conv_w, conv_b, w_down)
    return (_trunk(x_prompt, *params), _trunk(x_sample, *params))
```

```python
import functools
import math

import jax
import jax.numpy as jnp
from jax import lax
from jax.experimental import pallas as pl
from jax.experimental.pallas import tpu as pltpu

F32 = jnp.float32
BF16 = jnp.bfloat16

D_MODEL = 1024
HEAD_DIM = 64
A_PATTERNS = ((128, 1), (512, 4), (2048, 16))
A_GROUPS = 3
A_HEADS_PER_GROUP = 4
A_QKV = A_GROUPS * A_HEADS_PER_GROUP * HEAD_DIM
A_GROUP_W = A_HEADS_PER_GROUP * HEAD_DIM
B_HEADS = 4
B_QK = B_HEADS * 2 * HEAD_DIM
B_V = B_HEADS * 2 * HEAD_DIM
C_Q_HEADS = 16
C_KV_HEADS = 4
C_GROUP = C_Q_HEADS // C_KV_HEADS
ROPE_THETA = 500000.0
ROT_DIM = HEAD_DIM // 4
AXIAL_THETA = 10000.0
AXIAL_DIM = HEAD_DIM // 2
GRID_W = 64
D_FF = 2816
NORM_EPS = 1e-6
SUBLN_EPS = 1e-5
NEG_INF = -1e30
LOG2E = 1.4426950408889634
LN2 = 0.6931471805599453

V7X_LANES = 128
V7X_VMEM_BYTES = 64 * 1024 * 1024
VMEM_LIMIT = V7X_VMEM_BYTES - 8 * 1024 * 1024
BAND_HALF = 64
BAND_SUB = 128
BAND_WIN = BAND_SUB + 2 * V7X_LANES
FF_CHUNK = 256


def _params(*semantics):
    return pltpu.CompilerParams(dimension_semantics=semantics, vmem_limit_bytes=VMEM_LIMIT)


def _rms_rows(x, g):
    ms = jnp.mean(x * x, axis=-1, keepdims=True)
    return x * lax.rsqrt(ms + NORM_EPS) * g


def _head_norm_rot(y, gain, cos, sin, axial):
    ssq = jnp.sum(y * y, axis=0, keepdims=True)
    yn = y * lax.rsqrt(ssq * (1.0 / HEAD_DIM) + NORM_EPS) * gain
    if axial:
        h = AXIAL_DIM // 2
        a1, a2, b1, b2 = yn[0:h], yn[h:2 * h], yn[2 * h:3 * h], yn[3 * h:4 * h]
        cr, cc = cos[0:h], cos[h:2 * h]
        sr, sc = sin[0:h], sin[h:2 * h]
        return jnp.concatenate(
            [a1 * cr - a2 * sr, a2 * cr + a1 * sr, b1 * cc - b2 * sc, b2 * cc + b1 * sc], axis=0)
    h = ROT_DIM // 2
    x1, x2 = yn[0:h], yn[h:2 * h]
    return jnp.concatenate([x1 * cos - x2 * sin, x2 * cos + x1 * sin, yn[2 * h:]], axis=0)


def _proj_kernel(x_ref, g_ref, wt_ref, gain_ref, cos_ref, sin_ref, qt_ref, k_ref, vt_ref,
                 *, nq, nk, axial):
    h = _rms_rows(x_ref[...], g_ref[...]).astype(BF16)
    yt = lax.dot_general(wt_ref[...], h, (((1,), (1,)), ((), ())),
                         preferred_element_type=F32)
    cos = cos_ref[...]
    sin = sin_ref[...]
    gq = gain_ref[0:HEAD_DIM, :]
    gk = gain_ref[HEAD_DIM:2 * HEAD_DIM, :]
    for i in range(nq):
        y = yt[i * HEAD_DIM:(i + 1) * HEAD_DIM, :]
        qt_ref[i * HEAD_DIM:(i + 1) * HEAD_DIM, :] = _head_norm_rot(y, gq, cos, sin, axial).astype(BF16)
    base = nq * HEAD_DIM
    for p in range(nk // 2):
        pair = []
        for i in (2 * p, 2 * p + 1):
            y = yt[base + i * HEAD_DIM:base + (i + 1) * HEAD_DIM, :]
            pair.append(_head_norm_rot(y, gk, cos, sin, axial))
        kt = jnp.concatenate(pair, axis=0)
        k_ref[:, p * 2 * HEAD_DIM:(p + 1) * 2 * HEAD_DIM] = kt.T.astype(BF16)
    base = (nq + nk) * HEAD_DIM
    vt_ref[...] = yt[base:, :].astype(BF16)


def _proj(x2d, g, wt, gains, cos_t, sin_t, *, seq, dil, nq, nk, nv, axial, tm):
    t_total = x2d.shape[0]
    length = seq // dil
    tm = min(tm, length)
    nlb = length // tm
    nrow = t_total // dil // tm
    xv = x2d.reshape(t_total // dil, dil * D_MODEL)
    n_out = (nq + nk) * HEAD_DIM + nv
    rot_rows = cos_t.shape[0]

    def col_block(i, r):
        return ((i // nlb) * dil + r) * nlb + (i % nlb)

    kern = functools.partial(_proj_kernel, nq=nq, nk=nk, axial=axial)
    return pl.pallas_call(
        kern,
        grid=(nrow, dil),
        in_specs=[
            pl.BlockSpec((tm, D_MODEL), lambda i, r: (i, r)),
            pl.BlockSpec((1, D_MODEL), lambda i, r: (0, 0)),
            pl.BlockSpec((n_out, D_MODEL), lambda i, r: (0, 0)),
            pl.BlockSpec((2 * HEAD_DIM, tm), lambda i, r: (0, 0)),
            pl.BlockSpec((rot_rows, tm), lambda i, r: (0, r * nlb + i % nlb)),
            pl.BlockSpec((rot_rows, tm), lambda i, r: (0, r * nlb + i % nlb)),
        ],
        out_specs=[
            pl.BlockSpec((nq * HEAD_DIM, tm), lambda i, r: (0, col_block(i, r))),
            pl.BlockSpec((tm, nk * HEAD_DIM), lambda i, r: (col_block(i, r), 0)),
            pl.BlockSpec((nv, tm), lambda i, r: (0, col_block(i, r))),
        ],
        out_shape=[
            jax.ShapeDtypeStruct((nq * HEAD_DIM, t_total), BF16),
            jax.ShapeDtypeStruct((t_total, nk * HEAD_DIM), BF16),
            jax.ShapeDtypeStruct((nv, t_total), BF16),
        ],
        compiler_params=_params("parallel", "parallel"),
        name="proj",
    )(xv, g, wt, gains[:, :tm], cos_t, sin_t)


FLASH_UNROLL = 4
FLASH_COL_TILE = 256
V_PAD = 16


def _flash_scratch(dv, ncols, seq, tkc):
    return [
        pltpu.VMEM((2 * HEAD_DIM, ncols), BF16),
        pltpu.VMEM((dv + V_PAD, seq), BF16),
        pltpu.VMEM((dv + V_PAD, ncols), F32),
        pltpu.VMEM((1, ncols), F32),
        pltpu.VMEM((tkc, ncols), F32), pltpu.VMEM((tkc, ncols), F32),
        pltpu.VMEM((1, ncols), F32), pltpu.VMEM((1, ncols), F32),
    ]


def _flash_unit(k_ref, vt_ref, scratch, *, dv, seq, tkc):
    qpad_ref, vext_ref, acc_ref, m_ref, s0, s1, c0, c1 = scratch
    s_bufs, cm_bufs = (s0, s1), (c0, c1)
    nc = seq // tkc

    vext_ref[0:dv, :] = vt_ref[...]
    ones_row = lax.broadcasted_iota(jnp.int32, (V_PAD, seq), 0) == 0
    vext_ref[dv:, :] = jnp.where(ones_row, 1.0, 0.0).astype(BF16)
    m_ref[...] = jnp.full(m_ref.shape, NEG_INF, F32)
    acc_ref[...] = jnp.zeros(acc_ref.shape, F32)

    def chunk(c):
        return pl.ds(c * tkc if isinstance(c, int) else pl.multiple_of(c * tkc, tkc), tkc)

    ncols = qpad_ref.shape[1]
    tiles = [slice(j * FLASH_COL_TILE, (j + 1) * FLASH_COL_TILE) for j in range(ncols // FLASH_COL_TILE)]

    def scores(c, slot, t):
        s = jnp.dot(k_ref[chunk(c), :], qpad_ref[:, t], preferred_element_type=F32)
        s_bufs[slot][:, t] = s
        cm_bufs[slot][:, t] = jnp.max(s, axis=0, keepdims=True)

    def softmax_pv(c, slot, t):
        m_old = m_ref[:, t]
        m_new = jnp.maximum(m_old, cm_bufs[slot][:, t])
        alpha = jnp.exp2(m_old - m_new)
        m_ref[:, t] = m_new
        p = jnp.exp2((s_bufs[slot][:, t] - m_new).astype(BF16))
        pv = jnp.dot(vext_ref[:, chunk(c)], p, preferred_element_type=F32)
        acc_ref[:, t] = alpha * acc_ref[:, t] + pv

    def step(c, slot, last=False):
        for t in tiles:
            if not last:
                scores(c + 1, 1 - slot, t)
            softmax_pv(c, slot, t)

    for t in tiles:
        scores(0, 0, t)
    unroll = FLASH_UNROLL if nc % FLASH_UNROLL == 0 else 2
    if nc > unroll:
        def group(t, carry):
            for u in range(unroll):
                step(unroll * t + u, u % 2)
            return carry

        lax.fori_loop(0, nc // unroll - 1, group, 0)
    tail = min(nc, unroll)
    for c in range(nc - tail, nc):
        step(c, c % 2, last=(c == nc - 1))


def _attn_c_kernel(qt_ref, k_ref, vt_ref, o_ref, *scratch, seq, tq, tkc):
    n = pl.program_id(1)
    qpad_ref, acc_ref = scratch[0], scratch[2]
    qcat = jnp.concatenate([qt_ref[g * HEAD_DIM:(g + 1) * HEAD_DIM, :] for g in range(C_GROUP)], axis=1)
    zeros = jnp.zeros_like(qcat)

    @pl.when(n % 2 == 0)
    def _():
        qpad_ref[0:HEAD_DIM, :] = qcat
        qpad_ref[HEAD_DIM:, :] = zeros

    @pl.when(n % 2 == 1)
    def _():
        qpad_ref[0:HEAD_DIM, :] = zeros
        qpad_ref[HEAD_DIM:, :] = qcat

    _flash_unit(k_ref, vt_ref, scratch, dv=HEAD_DIM, seq=seq, tkc=tkc)
    o = acc_ref[0:HEAD_DIM, :] * (1.0 / acc_ref[HEAD_DIM:HEAD_DIM + 1, :])
    for g in range(C_GROUP):
        o_ref[g * HEAD_DIM:(g + 1) * HEAD_DIM, :] = o[:, g * tq:(g + 1) * tq].astype(BF16)


def _attn_c(qt, k, vt, *, batch, seq, tq, tkc):
    t_total = batch * seq
    tq = min(tq, seq)
    tkc = min(tkc, seq)
    nqb = seq // tq
    ncols = C_GROUP * tq
    kern = functools.partial(_attn_c_kernel, seq=seq, tq=tq, tkc=tkc)
    return pl.pallas_call(
        kern,
        grid=(batch, C_KV_HEADS, nqb),
        in_specs=[
            pl.BlockSpec((C_GROUP * HEAD_DIM, tq), lambda b, n, i: (n, b * nqb + i)),
            pl.BlockSpec((seq, 2 * HEAD_DIM), lambda b, n, i: (b, n // 2)),
            pl.BlockSpec((HEAD_DIM, seq), lambda b, n, i: (n, b)),
        ],
        out_specs=pl.BlockSpec((C_GROUP * HEAD_DIM, tq), lambda b, n, i: (n, b * nqb + i)),
        out_shape=jax.ShapeDtypeStruct((C_Q_HEADS * HEAD_DIM, t_total), BF16),
        scratch_shapes=_flash_scratch(HEAD_DIM, ncols, seq, tkc),
        compiler_params=_params("parallel", "parallel", "parallel"),
        name="attn_c",
    )(qt, k, vt)


def _attn_b_kernel(lam_ref, qt_ref, k_ref, vt_ref, sg_ref, o_ref, *scratch, seq, tq, tkc):
    dv = 2 * HEAD_DIM
    qpad_ref, acc_ref = scratch[0], scratch[2]
    q = qt_ref[...]
    zeros = jnp.zeros((HEAD_DIM, tq), BF16)
    qpad_ref[0:HEAD_DIM, 0:tq] = q[0:HEAD_DIM]
    qpad_ref[0:HEAD_DIM, tq:] = zeros
    qpad_ref[HEAD_DIM:, 0:tq] = zeros
    qpad_ref[HEAD_DIM:, tq:] = q[HEAD_DIM:]
    _flash_unit(k_ref, vt_ref, scratch, dv=dv, seq=seq, tkc=tkc)
    on = acc_ref[0:dv, :] * (1.0 / acc_ref[dv:dv + 1, :])
    o = on[:, 0:tq] - lam_ref[0] * on[:, tq:]
    ms = jnp.mean(o * o, axis=0, keepdims=True)
    o_ref[...] = (o * lax.rsqrt(ms + SUBLN_EPS) * sg_ref[...]).astype(BF16)


def _attn_b(lam, qt, k, vt, sg, *, batch, seq, tq, tkc):
    t_total = batch * seq
    tq = min(tq, seq)
    tkc = min(tkc, seq)
    nqb = seq // tq
    kern = functools.partial(_attn_b_kernel, seq=seq, tq=tq, tkc=tkc)
    return pl.pallas_call(
        kern,
        grid=(batch, B_HEADS, nqb),
        in_specs=[
            pl.BlockSpec(memory_space=pltpu.SMEM),
            pl.BlockSpec((2 * HEAD_DIM, tq), lambda b, h, i: (h, b * nqb + i)),
            pl.BlockSpec((seq, 2 * HEAD_DIM), lambda b, h, i: (b, h)),
            pl.BlockSpec((2 * HEAD_DIM, seq), lambda b, h, i: (h, b)),
            pl.BlockSpec((2 * HEAD_DIM, tq), lambda b, h, i: (0, 0)),
        ],
        out_specs=pl.BlockSpec((2 * HEAD_DIM, tq), lambda b, h, i: (h, b * nqb + i)),
        out_shape=jax.ShapeDtypeStruct((B_V, t_total), BF16),
        scratch_shapes=_flash_scratch(2 * HEAD_DIM, 2 * tq, seq, tkc),
        compiler_params=_params("parallel", "parallel", "parallel"),
        name="attn_b",
    )(lam, qt, k, vt, sg[:, :tq])


def _band_kernel(qt_ref, kp_ref, km_ref, kn_ref, vp_ref, vm_ref, vn_ref, o_ref, lse_ref, *, length, tqs):
    l0 = pl.program_id(1) * tqs
    kwin = jnp.concatenate([kp_ref[...], km_ref[...], kn_ref[...]], axis=0)
    vwin = jnp.concatenate([vp_ref[...], vm_ref[...], vn_ref[...]], axis=1)
    row = lax.broadcasted_iota(jnp.int32, (BAND_WIN, BAND_SUB), 0)
    col = lax.broadcasted_iota(jnp.int32, (BAND_WIN, BAND_SUB), 1)
    rel = row - V7X_LANES - col
    in_band = jnp.abs(rel) <= BAND_HALF
    zeros = jnp.zeros((HEAD_DIM, BAND_SUB), BF16)
    for j in range(tqs // BAND_SUB):
        c0 = j * BAND_SUB
        q = qt_ref[:, c0:c0 + BAND_SUB]
        kw = kwin[c0:c0 + BAND_WIN]
        vw = vwin[:, c0:c0 + BAND_WIN]
        kpos = l0 + (c0 - V7X_LANES) + row
        valid = in_band & (kpos >= 0) & (kpos < length)
        valid2 = jnp.concatenate([valid, valid], axis=1)
        outs, lses = [], []
        for p in range(A_HEADS_PER_GROUP // 2):
            ha, hb = 2 * p, 2 * p + 1
            qa = q[ha * HEAD_DIM:(ha + 1) * HEAD_DIM]
            qb = q[hb * HEAD_DIM:(hb + 1) * HEAD_DIM]
            qpad = jnp.concatenate([jnp.concatenate([qa, zeros], axis=1),
                                    jnp.concatenate([zeros, qb], axis=1)], axis=0)
            s = jnp.dot(kw[:, p * 2 * HEAD_DIM:(p + 1) * 2 * HEAD_DIM], qpad,
                        preferred_element_type=F32)
            s = jnp.where(valid2, s, NEG_INF)
            m = jnp.max(s, axis=0, keepdims=True)
            pr = jnp.exp2(s - m)
            l = jnp.sum(pr, axis=0, keepdims=True)
            pb = pr.astype(BF16)
            oa = jnp.dot(vw[ha * HEAD_DIM:(ha + 1) * HEAD_DIM], pb[:, :BAND_SUB], preferred_element_type=F32)
            ob = jnp.dot(vw[hb * HEAD_DIM:(hb + 1) * HEAD_DIM], pb[:, BAND_SUB:], preferred_element_type=F32)
            inv = 1.0 / l
            lse = m * LN2 + jnp.log(l)
            outs += [oa * inv[:, :BAND_SUB], ob * inv[:, BAND_SUB:]]
            lses += [jnp.broadcast_to(lse[:, :BAND_SUB], (HEAD_DIM, BAND_SUB)),
                     jnp.broadcast_to(lse[:, BAND_SUB:], (HEAD_DIM, BAND_SUB))]
        o_ref[c0:c0 + BAND_SUB, :] = jnp.concatenate(outs, axis=0).T
        lse_ref[c0:c0 + BAND_SUB, :] = jnp.concatenate(lses, axis=0).T


def _band(qt, k, vt, *, batch, seq, dil, tqs):
    t_total = batch * seq
    length = seq // dil
    tqs = min(tqs, length)
    nqb = length // tqs
    ncls = batch * dil
    nkb = length // V7X_LANES
    r128 = tqs // V7X_LANES

    def prev_blk(c, i):
        return c * nkb + jnp.maximum(i * r128 - 1, 0)

    def next_blk(c, i):
        return c * nkb + jnp.minimum((i + 1) * r128, nkb - 1)

    kern = functools.partial(_band_kernel, length=length, tqs=tqs)
    out_spec = pl.BlockSpec((tqs, A_GROUP_W), lambda c, i: ((c // dil) * nqb + i, c % dil))
    o, lse = pl.pallas_call(
        kern,
        grid=(ncls, nqb),
        in_specs=[
            pl.BlockSpec((A_GROUP_W, tqs), lambda c, i: (0, c * nqb + i)),
            pl.BlockSpec((V7X_LANES, A_GROUP_W), lambda c, i: (prev_blk(c, i), 0)),
            pl.BlockSpec((tqs, A_GROUP_W), lambda c, i: (c * nqb + i, 0)),
            pl.BlockSpec((V7X_LANES, A_GROUP_W), lambda c, i: (next_blk(c, i), 0)),
            pl.BlockSpec((A_GROUP_W, V7X_LANES), lambda c, i: (0, prev_blk(c, i))),
            pl.BlockSpec((A_GROUP_W, tqs), lambda c, i: (0, c * nqb + i)),
            pl.BlockSpec((A_GROUP_W, V7X_LANES), lambda c, i: (0, next_blk(c, i))),
        ],
        out_specs=[out_spec, out_spec],
        out_shape=[jax.ShapeDtypeStruct((t_total // dil, dil * A_GROUP_W), F32)] * 2,
        compiler_params=_params("parallel", "parallel"),
        name="band",
    )(qt, k, k, k, vt, vt, vt)
    return o.reshape(t_total, A_GROUP_W), lse.reshape(t_total, A_GROUP_W)


def _outproj_even_kernel(x_ref, o0_ref, o1_ref, o2_ref, l0_ref, l1_ref, l2_ref, obt_ref, wa_ref, wb_ref,
                         out_ref):
    l0, l1, l2 = l0_ref[...], l1_ref[...], l2_ref[...]
    m = jnp.maximum(jnp.maximum(l0, l1), l2)
    w0, w1, w2 = jnp.exp(l0 - m), jnp.exp(l1 - m), jnp.exp(l2 - m)
    oa = (w0 * o0_ref[...] + w1 * o1_ref[...] + w2 * o2_ref[...]) / (w0 + w1 + w2)
    acc = jnp.dot(oa.astype(BF16), wa_ref[...], preferred_element_type=F32)
    acc += lax.dot_general(obt_ref[...], wb_ref[...], (((0,), (0,)), ((), ())),
                           preferred_element_type=F32)
    out_ref[...] = x_ref[...] + acc


def _outproj_even(x2d, oa, lse, obt, wa, wb, *, tm):
    t_total = x2d.shape[0]
    row = lambda i: (i, 0)
    const = lambda i: (0, 0)
    a_spec = pl.BlockSpec((tm, A_GROUP_W), row)
    return pl.pallas_call(
        _outproj_even_kernel,
        grid=(t_total // tm,),
        in_specs=[pl.BlockSpec((tm, D_MODEL), row)] + [a_spec] * 6 + [
            pl.BlockSpec((B_V, tm), lambda i: (0, i)),
            pl.BlockSpec((A_GROUP_W, D_MODEL), const),
            pl.BlockSpec((B_V, D_MODEL), const),
        ],
        out_specs=pl.BlockSpec((tm, D_MODEL), row),
        out_shape=jax.ShapeDtypeStruct((t_total, D_MODEL), F32),
        compiler_params=_params("parallel"),
        name="outproj_even",
    )(x2d, *oa, *lse, obt, wa, wb)


def _outproj_odd_kernel(x_ref, ot_ref, w_ref, out_ref):
    acc = lax.dot_general(ot_ref[...], w_ref[...], (((0,), (0,)), ((), ())), preferred_element_type=F32)
    out_ref[...] = x_ref[...] + acc


def _outproj_odd(x2d, ot, w, *, tm):
    t_total = x2d.shape[0]
    n_in = ot.shape[0]
    return pl.pallas_call(
        _outproj_odd_kernel,
        grid=(t_total // tm,),
        in_specs=[
            pl.BlockSpec((tm, D_MODEL), lambda i: (i, 0)),
            pl.BlockSpec((n_in, tm), lambda i: (0, i)),
            pl.BlockSpec((n_in, D_MODEL), lambda i: (0, 0)),
        ],
        out_specs=pl.BlockSpec((tm, D_MODEL), lambda i: (i, 0)),
        out_shape=jax.ShapeDtypeStruct((t_total, D_MODEL), F32),
        compiler_params=_params("parallel"),
        name="outproj_odd",
    )(x2d, ot, w)


FF_HALO = 16


def _ffn_kernel(xp_ref, x_ref, xn_ref, g_ref, wup_ref, cw_ref, cb_ref, wd_ref, out_ref, hs_ref, acc_ref,
                *, seq, tm):
    i = pl.program_id(0)
    g = g_ref[...]
    has_prev = ((i * tm) % seq != 0).astype(F32)
    has_next = (((i + 1) * tm) % seq != 0).astype(F32)
    hs_ref[0:FF_HALO, :] = (_rms_rows(xp_ref[...], g) * has_prev).astype(BF16)
    hs_ref[FF_HALO:FF_HALO + tm, :] = _rms_rows(x_ref[...], g).astype(BF16)
    hs_ref[FF_HALO + tm:, :] = (_rms_rows(xn_ref[...], g) * has_next).astype(BF16)
    rows = tm + 2 * FF_HALO

    def conv(u, c0):
        w = cw_ref[:, c0:c0 + FF_CHUNK]
        b = cb_ref[:, c0:c0 + FF_CHUNK]
        um = pltpu.roll(u, 1, 0)[FF_HALO:FF_HALO + tm]
        up = pltpu.roll(u, rows - 1, 0)[FF_HALO:FF_HALO + tm]
        return um * w[0:1] + u[FF_HALO:FF_HALO + tm] * w[1:2] + up * w[2:3] + b

    for c in range(D_FF // FF_CHUNK):
        cg = c * FF_CHUNK
        cv = D_FF + c * FF_CHUNK
        hs = hs_ref[...]
        ug = jnp.dot(hs, wup_ref[:, cg:cg + FF_CHUNK], preferred_element_type=F32)
        uv = jnp.dot(hs, wup_ref[:, cv:cv + FF_CHUNK], preferred_element_type=F32)
        gate = conv(ug, cg)
        act = (gate * jax.nn.sigmoid(gate) * conv(uv, cv)).astype(BF16)
        part = jnp.dot(act, wd_ref[cg:cg + FF_CHUNK, :], preferred_element_type=F32)
        if c == 0:
            acc_ref[...] = part
        else:
            acc_ref[...] += part
    out_ref[...] = x_ref[...] + acc_ref[...]


def _ffn(x2d, g, wup, cw, cb, wd, *, seq, tm):
    t_total = x2d.shape[0]
    tm = min(tm, seq)
    r = tm // FF_HALO
    last = t_total // FF_HALO - 1
    const = lambda i: (0, 0)
    resident = dict(pipeline_mode=pl.Buffered(1))
    kern = functools.partial(_ffn_kernel, seq=seq, tm=tm)
    return pl.pallas_call(
        kern,
        grid=(t_total // tm,),
        in_specs=[
            pl.BlockSpec((FF_HALO, D_MODEL), lambda i: (jnp.maximum(i * r - 1, 0), 0)),
            pl.BlockSpec((tm, D_MODEL), lambda i: (i, 0)),
            pl.BlockSpec((FF_HALO, D_MODEL), lambda i: (jnp.minimum((i + 1) * r, last), 0)),
            pl.BlockSpec((1, D_MODEL), const),
            pl.BlockSpec((D_MODEL, 2 * D_FF), const, **resident),
            pl.BlockSpec((3, 2 * D_FF), const),
            pl.BlockSpec((1, 2 * D_FF), const),
            pl.BlockSpec((D_FF, D_MODEL), const, **resident),
        ],
        out_specs=pl.BlockSpec((tm, D_MODEL), lambda i: (i, 0)),
        out_shape=jax.ShapeDtypeStruct((t_total, D_MODEL), F32),
        scratch_shapes=[
            pltpu.VMEM((tm + 2 * FF_HALO, D_MODEL), BF16),
            pltpu.VMEM((tm, D_MODEL), F32),
        ],
        compiler_params=_params("parallel"),
        name="ffn",
    )(x2d, x2d, x2d, g, wup, cw, cb, wd)


def _rope_tables(pos, dim, theta):
    inv = theta ** (-jnp.arange(0, dim, 2, dtype=F32) / dim)
    ang = pos.astype(F32)[:, None] * inv[None, :]
    return jnp.cos(ang).T, jnp.sin(ang).T


def _class_order(table, dil):
    rows, seq = table.shape
    return table.reshape(rows, seq // dil, dil).transpose(0, 2, 1).reshape(rows, seq)


def _lane_bcast(v, width):
    return jnp.broadcast_to(v.astype(F32)[:, None], (v.shape[0], width))


TM = 512
TQ_C = 256
TQ_B = 512
TKC = 512
TQ_BAND = 512
Q_SCALE = HEAD_DIM ** -0.5 * LOG2E


def _trunk(x, norm_mix, norm_ffn, w_in_ab, q_norm_a, k_norm_a, q_norm_b, k_norm_b,
           lambda_q1, lambda_k1, lambda_q2, lambda_k2, subln_b, w_out_ab,
           w_in_c, q_norm_c, k_norm_c, w_out_c, w_up, conv_w, conv_b, w_down):
    batch, seq, _ = x.shape
    t_total = batch * seq
    depth = norm_mix.shape[0]
    x2d = x.reshape(t_total, D_MODEL)

    cos, sin = _rope_tables(jnp.arange(seq), ROT_DIM, ROPE_THETA)
    rows = seq // GRID_W
    row = jnp.repeat(jnp.arange(rows), GRID_W)
    col = jnp.tile(jnp.arange(GRID_W), rows)
    cr, sr = _rope_tables(row, AXIAL_DIM, AXIAL_THETA)
    cc, sc = _rope_tables(col, AXIAL_DIM, AXIAL_THETA)
    cos_ax = jnp.concatenate([cr, cc], axis=0)
    sin_ax = jnp.concatenate([sr, sc], axis=0)

    for i in range(depth):
        j = i // 2
        g_mix = norm_mix[i].reshape(1, D_MODEL)
        if i % 2 == 0:
            lam_init = 0.8 - 0.6 * math.exp(-0.3 * i)
            w = w_in_ab[j]
            oa, lse = [], []
            gains_a = jnp.concatenate([_lane_bcast(q_norm_a[j] * Q_SCALE, TM), _lane_bcast(k_norm_a[j], TM)], 0)
            for gi, (_, dil) in enumerate(A_PATTERNS):
                sl = slice(gi * A_GROUP_W, (gi + 1) * A_GROUP_W)
                wt = jnp.concatenate([w[:, sl], w[:, A_QKV:2 * A_QKV][:, sl], w[:, 2 * A_QKV:3 * A_QKV][:, sl]],
                                     axis=1).T.astype(BF16)
                qt, k, vt = _proj(x2d, g_mix, wt, gains_a, _class_order(cos, dil), _class_order(sin, dil),
                                  seq=seq, dil=dil, nq=A_HEADS_PER_GROUP, nk=A_HEADS_PER_GROUP, nv=A_GROUP_W,
                                  axial=False, tm=TM)
                o_g, lse_g = _band(qt, k, vt, batch=batch, seq=seq, dil=dil, tqs=TQ_BAND)
                oa.append(o_g)
                lse.append(lse_g)
            wt_b = w[:, 3 * A_QKV:].T.astype(BF16)
            gains_b = jnp.concatenate([_lane_bcast(q_norm_b[j] * Q_SCALE, TM), _lane_bcast(k_norm_b[j], TM)], 0)
            qt, k, vt = _proj(x2d, g_mix, wt_b, gains_b, cos, sin, seq=seq, dil=1,
                              nq=2 * B_HEADS, nk=2 * B_HEADS, nv=B_V, axial=False, tm=TM)
            lam = (jnp.exp(jnp.sum(lambda_q1[j].astype(F32) * lambda_k1[j].astype(F32)))
                   - jnp.exp(jnp.sum(lambda_q2[j].astype(F32) * lambda_k2[j].astype(F32))) + lam_init)
            sg = _lane_bcast(subln_b[j] * (1.0 - lam_init), TQ_B)
            obt = _attn_b(lam.reshape(1).astype(F32), qt, k, vt, sg, batch=batch, seq=seq, tq=TQ_B, tkc=TKC)
            wo = w_out_ab[j].astype(BF16)
            x2d = _outproj_even(x2d, oa, lse, obt, wo[:A_GROUP_W], wo[A_GROUP_W:], tm=TM)
        else:
            wt = w_in_c[j].T.astype(BF16)
            gains_c = jnp.concatenate([_lane_bcast(q_norm_c[j] * Q_SCALE, TM), _lane_bcast(k_norm_c[j], TM)], 0)
            qt, k, vt = _proj(x2d, g_mix, wt, gains_c, cos_ax, sin_ax, seq=seq, dil=1,
                              nq=C_Q_HEADS, nk=C_KV_HEADS, nv=C_KV_HEADS * HEAD_DIM, axial=True, tm=TM)
            ot = _attn_c(qt, k, vt, batch=batch, seq=seq, tq=TQ_C, tkc=TKC)
            x2d = _outproj_odd(x2d, ot, w_out_c[j].astype(BF16), tm=TM)
        x2d = _ffn(x2d, norm_ffn[i].reshape(1, D_MODEL), w_up[i].astype(BF16), conv_w[i],
                   conv_b[i].reshape(1, 2 * D_FF), w_down[i].astype(BF16), seq=seq, tm=TM)
    return x2d.reshape(batch, seq, D_MODEL)


def kernel(x_prompt, x_sample, norm_mix, norm_ffn, w_in_ab, q_norm_a, k_norm_a, q_norm_b, k_norm_b,
           lambda_q1, lambda_k1, lambda_q2, lambda_k2, subln_b, w_out_ab, w_in_c, q_norm_c, k_norm_c,
           w_out_c, w_up, conv_w, conv_b, w_down):
    params = (norm_mix, norm_ffn, w_in_ab, q_norm_a, k_norm_a, q_norm_b, k_norm_b,
              lambda_q1, lambda_k1, lambda_q2, lambda_k2, subln_b, w_out_ab,
              w_in_c, q_norm_c, k_norm_c, w_out_c, w_up, conv_w, conv_b, w_down)
    return (_trunk(x_prompt, *params), _trunk(x_sample, *params))
```

```python
import functools
import math

import jax
import jax.numpy as jnp
from jax import lax
from jax.experimental import pallas as pl
from jax.experimental.pallas import tpu as pltpu

F32 = jnp.float32
BF16 = jnp.bfloat16

D_MODEL = 1024
HEAD_DIM = 64
A_PATTERNS = ((128, 1), (512, 4), (2048, 16))
A_GROUPS = 3
A_HEADS_PER_GROUP = 4
A_QKV = A_GROUPS * A_HEADS_PER_GROUP * HEAD_DIM
A_GROUP_W = A_HEADS_PER_GROUP * HEAD_DIM
B_HEADS = 4
B_QK = B_HEADS * 2 * HEAD_DIM
B_V = B_HEADS * 2 * HEAD_DIM
C_Q_HEADS = 16
C_KV_HEADS = 4
C_GROUP = C_Q_HEADS // C_KV_HEADS
ROPE_THETA = 500000.0
ROT_DIM = HEAD_DIM // 4
AXIAL_THETA = 10000.0
AXIAL_DIM = HEAD_DIM // 2
GRID_W = 64
D_FF = 2816
NORM_EPS = 1e-6
SUBLN_EPS = 1e-5
NEG_INF = -1e30
LOG2E = 1.4426950408889634
LN2 = 0.6931471805599453

V7X_LANES = 128
V7X_VMEM_BYTES = 64 * 1024 * 1024
VMEM_LIMIT = V7X_VMEM_BYTES - 8 * 1024 * 1024
BAND_HALF = 64
BAND_SUB = 128
BAND_WIN = BAND_SUB + 2 * V7X_LANES
FF_CHUNK = 256


def _params(*semantics):
    return pltpu.CompilerParams(dimension_semantics=semantics, vmem_limit_bytes=VMEM_LIMIT)


def _rms_rows(x, g):
    ms = jnp.mean(x * x, axis=-1, keepdims=True)
    return x * lax.rsqrt(ms + NORM_EPS) * g


def _head_norm_rot(y, gain, cos, sin, axial):
    ssq = jnp.sum(y * y, axis=0, keepdims=True)
    yn = y * lax.rsqrt(ssq * (1.0 / HEAD_DIM) + NORM_EPS) * gain
    if axial:
        h = AXIAL_DIM // 2
        a1, a2, b1, b2 = yn[0:h], yn[h:2 * h], yn[2 * h:3 * h], yn[3 * h:4 * h]
        cr, cc = cos[0:h], cos[h:2 * h]
        sr, sc = sin[0:h], sin[h:2 * h]
        return jnp.concatenate(
            [a1 * cr - a2 * sr, a2 * cr + a1 * sr, b1 * cc - b2 * sc, b2 * cc + b1 * sc], axis=0)
    h = ROT_DIM // 2
    x1, x2 = yn[0:h], yn[h:2 * h]
    return jnp.concatenate([x1 * cos - x2 * sin, x2 * cos + x1 * sin, yn[2 * h:]], axis=0)


def _proj_kernel(x_ref, g_ref, wt_ref, gain_ref, cos_ref, sin_ref, qt_ref, k_ref, vt_ref,
                 *, nq, nk, axial):
    h = _rms_rows(x_ref[...], g_ref[...]).astype(BF16)
    yt = lax.dot_general(wt_ref[...], h, (((1,), (1,)), ((), ())),
                         preferred_element_type=F32)
    cos = cos_ref[...]
    sin = sin_ref[...]
    gq = gain_ref[0:HEAD_DIM, :]
    gk = gain_ref[HEAD_DIM:2 * HEAD_DIM, :]
    for i in range(nq):
        y = yt[i * HEAD_DIM:(i + 1) * HEAD_DIM, :]
        qt_ref[i * HEAD_DIM:(i + 1) * HEAD_DIM, :] = _head_norm_rot(y, gq, cos, sin, axial).astype(BF16)
    base = nq * HEAD_DIM
    for p in range(nk // 2):
        pair = []
        for i in (2 * p, 2 * p + 1):
            y = yt[base + i * HEAD_DIM:base + (i + 1) * HEAD_DIM, :]
            pair.append(_head_norm_rot(y, gk, cos, sin, axial))
        kt = jnp.concatenate(pair, axis=0)
        k_ref[:, p * 2 * HEAD_DIM:(p + 1) * 2 * HEAD_DIM] = kt.T.astype(BF16)
    base = (nq + nk) * HEAD_DIM
    vt_ref[...] = yt[base:, :].astype(BF16)


def _proj(x2d, g, wt, gains, cos_t, sin_t, *, seq, dil, nq, nk, nv, axial, tm):
    t_total = x2d.shape[0]
    length = seq // dil
    tm = min(tm, length)
    nlb = length // tm
    nrow = t_total // dil // tm
    xv = x2d.reshape(t_total // dil, dil * D_MODEL)
    n_out = (nq + nk) * HEAD_DIM + nv
    rot_rows = cos_t.shape[0]

    def col_block(i, r):
        return ((i // nlb) * dil + r) * nlb + (i % nlb)

    kern = functools.partial(_proj_kernel, nq=nq, nk=nk, axial=axial)
    return pl.pallas_call(
        kern,
        grid=(nrow, dil),
        in_specs=[
            pl.BlockSpec((tm, D_MODEL), lambda i, r: (i, r)),
            pl.BlockSpec((1, D_MODEL), lambda i, r: (0, 0)),
            pl.BlockSpec((n_out, D_MODEL), lambda i, r: (0, 0)),
            pl.BlockSpec((2 * HEAD_DIM, tm), lambda i, r: (0, 0)),
            pl.BlockSpec((rot_rows, tm), lambda i, r: (0, r * nlb + i % nlb)),
            pl.BlockSpec((rot_rows, tm), lambda i, r: (0, r * nlb + i % nlb)),
        ],
        out_specs=[
            pl.BlockSpec((nq * HEAD_DIM, tm), lambda i, r: (0, col_block(i, r))),
            pl.BlockSpec((tm, nk * HEAD_DIM), lambda i, r: (col_block(i, r), 0)),
            pl.BlockSpec((nv, tm), lambda i, r: (0, col_block(i, r))),
        ],
        out_shape=[
            jax.ShapeDtypeStruct((nq * HEAD_DIM, t_total), BF16),
            jax.ShapeDtypeStruct((t_total, nk * HEAD_DIM), BF16),
            jax.ShapeDtypeStruct((nv, t_total), BF16),
        ],
        compiler_params=_params("parallel", "parallel"),
        name="proj",
    )(xv, g, wt, gains[:, :tm], cos_t, sin_t)


FLASH_UNROLL = 4
FLASH_COL_TILE = 256
V_PAD = 16


def _flash_scratch(dv, ncols, seq, tkc):
    return [
        pltpu.VMEM((2 * HEAD_DIM, ncols), BF16),
        pltpu.VMEM((dv + V_PAD, seq), BF16),
        pltpu.VMEM((dv + V_PAD, ncols), F32),
        pltpu.VMEM((1, ncols), F32),
        pltpu.VMEM((tkc, ncols), F32), pltpu.VMEM((tkc, ncols), F32),
        pltpu.VMEM((1, ncols), F32), pltpu.VMEM((1, ncols), F32),
    ]


def _flash_unit(k_ref, vt_ref, scratch, *, dv, seq, tkc):
    qpad_ref, vext_ref, acc_ref, m_ref, s0, s1, c0, c1 = scratch
    s_bufs, cm_bufs = (s0, s1), (c0, c1)
    nc = seq // tkc

    vext_ref[0:dv, :] = vt_ref[...]
    ones_row = lax.broadcasted_iota(jnp.int32, (V_PAD, seq), 0) == 0
    vext_ref[dv:, :] = jnp.where(ones_row, 1.0, 0.0).astype(BF16)
    m_ref[...] = jnp.full(m_ref.shape, NEG_INF, F32)
    acc_ref[...] = jnp.zeros(acc_ref.shape, F32)

    def chunk(c):
        return pl.ds(c * tkc if isinstance(c, int) else pl.multiple_of(c * tkc, tkc), tkc)

    ncols = qpad_ref.shape[1]
    tiles = [slice(j * FLASH_COL_TILE, (j + 1) * FLASH_COL_TILE) for j in range(ncols // FLASH_COL_TILE)]

    def scores(c, slot, t):
        s = jnp.dot(k_ref[chunk(c), :], qpad_ref[:, t], preferred_element_type=F32)
        s_bufs[slot][:, t] = s
        cm_bufs[slot][:, t] = jnp.max(s, axis=0, keepdims=True)

    def softmax_pv(c, slot, t):
        m_old = m_ref[:, t]
        m_new = jnp.maximum(m_old, cm_bufs[slot][:, t])
        alpha = jnp.exp2(m_old - m_new)
        m_ref[:, t] = m_new
        p = jnp.exp2((s_bufs[slot][:, t] - m_new).astype(BF16))
        pv = jnp.dot(vext_ref[:, chunk(c)], p, preferred_element_type=F32)
        acc_ref[:, t] = alpha * acc_ref[:, t] + pv

    def step(c, slot, last=False):
        for t in tiles:
            if not last:
                scores(c + 1, 1 - slot, t)
            softmax_pv(c, slot, t)

    for t in tiles:
        scores(0, 0, t)
    unroll = FLASH_UNROLL if nc % FLASH_UNROLL == 0 else 2
    if nc > unroll:
        def group(t, carry):
            for u in range(unroll):
                step(unroll * t + u, u % 2)
            return carry

        lax.fori_loop(0, nc // unroll - 1, group, 0)
    tail = min(nc, unroll)
    for c in range(nc - tail, nc):
        step(c, c % 2, last=(c == nc - 1))


def _attn_c_kernel(qt_ref, k_ref, vt_ref, o_ref, *scratch, seq, tq, tkc):
    n = pl.program_id(1)
    qpad_ref, acc_ref = scratch[0], scratch[2]
    qcat = jnp.concatenate([qt_ref[g * HEAD_DIM:(g + 1) * HEAD_DIM, :] for g in range(C_GROUP)], axis=1)
    zeros = jnp.zeros_like(qcat)

    @pl.when(n % 2 == 0)
    def _():
        qpad_ref[0:HEAD_DIM, :] = qcat
        qpad_ref[HEAD_DIM:, :] = zeros

    @pl.when(n % 2 == 1)
    def _():
        qpad_ref[0:HEAD_DIM, :] = zeros
        qpad_ref[HEAD_DIM:, :] = qcat

    _flash_unit(k_ref, vt_ref, scratch, dv=HEAD_DIM, seq=seq, tkc=tkc)
    o = acc_ref[0:HEAD_DIM, :] * (1.0 / acc_ref[HEAD_DIM:HEAD_DIM + 1, :])
    for g in range(C_GROUP):
        o_ref[g * HEAD_DIM:(g + 1) * HEAD_DIM, :] = o[:, g * tq:(g + 1) * tq].astype(BF16)


def _attn_c(qt, k, vt, *, batch, seq, tq, tkc):
    t_total = batch * seq
    tq = min(tq, seq)
    tkc = min(tkc, seq)
    nqb = seq // tq
    ncols = C_GROUP * tq
    kern = functools.partial(_attn_c_kernel, seq=seq, tq=tq, tkc=tkc)
    return pl.pallas_call(
        kern,
        grid=(batch, C_KV_HEADS, nqb),
        in_specs=[
            pl.BlockSpec((C_GROUP * HEAD_DIM, tq), lambda b, n, i: (n, b * nqb + i)),
            pl.BlockSpec((seq, 2 * HEAD_DIM), lambda b, n, i: (b, n // 2)),
            pl.BlockSpec((HEAD_DIM, seq), lambda b, n, i: (n, b)),
        ],
        out_specs=pl.BlockSpec((C_GROUP * HEAD_DIM, tq), lambda b, n, i: (n, b * nqb + i)),
        out_shape=jax.ShapeDtypeStruct((C_Q_HEADS * HEAD_DIM, t_total), BF16),
        scratch_shapes=_flash_scratch(HEAD_DIM, ncols, seq, tkc),
        compiler_params=_params("parallel", "parallel", "parallel"),
        name="attn_c",
    )(qt, k, vt)


def _attn_b_kernel(lam_ref, qt_ref, k_ref, vt_ref, sg_ref, o_ref, *scratch, seq, tq, tkc):
    dv = 2 * HEAD_DIM
    qpad_ref, acc_ref = scratch[0], scratch[2]
    q = qt_ref[...]
    zeros = jnp.zeros((HEAD_DIM, tq), BF16)
    qpad_ref[0:HEAD_DIM, 0:tq] = q[0:HEAD_DIM]
    qpad_ref[0:HEAD_DIM, tq:] = zeros
    qpad_ref[HEAD_DIM:, 0:tq] = zeros
    qpad_ref[HEAD_DIM:, tq:] = q[HEAD_DIM:]
    _flash_unit(k_ref, vt_ref, scratch, dv=dv, seq=seq, tkc=tkc)
    on = acc_ref[0:dv, :] * (1.0 / acc_ref[dv:dv + 1, :])
    o = on[:, 0:tq] - lam_ref[0] * on[:, tq:]
    ms = jnp.mean(o * o, axis=0, keepdims=True)
    o_ref[...] = (o * lax.rsqrt(ms + SUBLN_EPS) * sg_ref[...]).astype(BF16)


def _attn_b(lam, qt, k, vt, sg, *, batch, seq, tq, tkc):
    t_total = batch * seq
    tq = min(tq, seq)
    tkc = min(tkc, seq)
    nqb = seq // tq
    kern = functools.partial(_attn_b_kernel, seq=seq, tq=tq, tkc=tkc)
    return pl.pallas_call(
        kern,
        grid=(batch, B_HEADS, nqb),
        in_specs=[
            pl.BlockSpec(memory_space=pltpu.SMEM),
            pl.BlockSpec((2 * HEAD_DIM, tq), lambda b, h, i: (h, b * nqb + i)),
            pl.BlockSpec((seq, 2 * HEAD_DIM), lambda b, h, i: (b, h)),
            pl.BlockSpec((2 * HEAD_DIM, seq), lambda b, h, i: (h, b)),
            pl.BlockSpec((2 * HEAD_DIM, tq), lambda b, h, i: (0, 0)),
        ],
        out_specs=pl.BlockSpec((2 * HEAD_DIM, tq), lambda b, h, i: (h, b * nqb + i)),
        out_shape=jax.ShapeDtypeStruct((B_V, t_total), BF16),
        scratch_shapes=_flash_scratch(2 * HEAD_DIM, 2 * tq, seq, tkc),
        compiler_params=_params("parallel", "parallel", "parallel"),
        name="attn_b",
    )(lam, qt, k, vt, sg[:, :tq])


def _band_kernel(qt_ref, kp_ref, km_ref, kn_ref, vp_ref, vm_ref, vn_ref, o_ref, lse_ref, *, length, tqs):
    l0 = pl.program_id(1) * tqs
    kwin = jnp.concatenate([kp_ref[...], km_ref[...], kn_ref[...]], axis=0)
    vwin = jnp.concatenate([vp_ref[...], vm_ref[...], vn_ref[...]], axis=1)
    row = lax.broadcasted_iota(jnp.int32, (BAND_WIN, BAND_SUB), 0)
    col = lax.broadcasted_iota(jnp.int32, (BAND_WIN, BAND_SUB), 1)
    rel = row - V7X_LANES - col
    in_band = jnp.abs(rel) <= BAND_HALF
    zeros = jnp.zeros((HEAD_DIM, BAND_SUB), BF16)
    for j in range(tqs // BAND_SUB):
        c0 = j * BAND_SUB
        q = qt_ref[:, c0:c0 + BAND_SUB]
        kw = kwin[c0:c0 + BAND_WIN]
        vw = vwin[:, c0:c0 + BAND_WIN]
        kpos = l0 + (c0 - V7X_LANES) + row
        valid = in_band & (kpos >= 0) & (kpos < length)
        valid2 = jnp.concatenate([valid, valid], axis=1)
        outs, lses = [], []
        for p in range(A_HEADS_PER_GROUP // 2):
            ha, hb = 2 * p, 2 * p + 1
            qa = q[ha * HEAD_DIM:(ha + 1) * HEAD_DIM]
            qb = q[hb * HEAD_DIM:(hb + 1) * HEAD_DIM]
            qpad = jnp.concatenate([jnp.concatenate([qa, zeros], axis=1),
                                    jnp.concatenate([zeros, qb], axis=1)], axis=0)
            s = jnp.dot(kw[:, p * 2 * HEAD_DIM:(p + 1) * 2 * HEAD_DIM], qpad,
                        preferred_element_type=F32)
            s = jnp.where(valid2, s, NEG_INF)
            m = jnp.max(s, axis=0, keepdims=True)
            pr = jnp.exp2(s - m)
            l = jnp.sum(pr, axis=0, keepdims=True)
            pb = pr.astype(BF16)
            oa = jnp.dot(vw[ha * HEAD_DIM:(ha + 1) * HEAD_DIM], pb[:, :BAND_SUB], preferred_element_type=F32)
            ob = jnp.dot(vw[hb * HEAD_DIM:(hb + 1) * HEAD_DIM], pb[:, BAND_SUB:], preferred_element_type=F32)
            inv = 1.0 / l
            lse = m * LN2 + jnp.log(l)
            outs += [oa * inv[:, :BAND_SUB], ob * inv[:, BAND_SUB:]]
            lses += [jnp.broadcast_to(lse[:, :BAND_SUB], (HEAD_DIM, BAND_SUB)),
                     jnp.broadcast_to(lse[:, BAND_SUB:], (HEAD_DIM, BAND_SUB))]
        o_ref[c0:c0 + BAND_SUB, :] = jnp.concatenate(outs, axis=0).T
        lse_ref[c0:c0 + BAND_SUB, :] = jnp.concatenate(lses, axis=0).T


def _band(qt, k, vt, *, batch, seq, dil, tqs):
    t_total = batch * seq
    length = seq // dil
    tqs = min(tqs, length)
    nqb = length // tqs
    ncls = batch * dil
    nkb = length // V7X_LANES
    r128 = tqs // V7X_LANES

    def prev_blk(c, i):
        return c * nkb + jnp.maximum(i * r128 - 1, 0)

    def next_blk(c, i):
        return c * nkb + jnp.minimum((i + 1) * r128, nkb - 1)

    kern = functools.partial(_band_kernel, length=length, tqs=tqs)
    out_spec = pl.BlockSpec((tqs, A_GROUP_W), lambda c, i: ((c // dil) * nqb + i, c % dil))
    o, lse = pl.pallas_call(
        kern,
        grid=(ncls, nqb),
        in_specs=[
            pl.BlockSpec((A_GROUP_W, tqs), lambda c, i: (0, c * nqb + i)),
            pl.BlockSpec((V7X_LANES, A_GROUP_W), lambda c, i: (prev_blk(c, i), 0)),
            pl.BlockSpec((tqs, A_GROUP_W), lambda c, i: (c * nqb + i, 0)),
            pl.BlockSpec((V7X_LANES, A_GROUP_W), lambda c, i: (next_blk(c, i), 0)),
            pl.BlockSpec((A_GROUP_W, V7X_LANES), lambda c, i: (0, prev_blk(c, i))),
            pl.BlockSpec((A_GROUP_W, tqs), lambda c, i: (0, c * nqb + i)),
            pl.BlockSpec((A_GROUP_W, V7X_LANES), lambda c, i: (0, next_blk(c, i))),
        ],
        out_specs=[out_spec, out_spec],
        out_shape=[jax.ShapeDtypeStruct((t_total // dil, dil * A_GROUP_W), F32)] * 2,
        compiler_params=_params("parallel", "parallel"),
        name="band",
    )(qt, k, k, k, vt, vt, vt)
    return o.reshape(t_total, A_GROUP_W), lse.reshape(t_total, A_GROUP_W)


def _outproj_even_kernel(x_ref, o0_ref, o1_ref, o2_ref, l0_ref, l1_ref, l2_ref, obt_ref, wa_ref, wb_ref,
                         out_ref):
    l0, l1, l2 = l0_ref[...], l1_ref[...], l2_ref[...]
    m = jnp.maximum(jnp.maximum(l0, l1), l2)
    w0, w1, w2 = jnp.exp(l0 - m), jnp.exp(l1 - m), jnp.exp(l2 - m)
    oa = (w0 * o0_ref[...] + w1 * o1_ref[...] + w2 * o2_ref[...]) / (w0 + w1 + w2)
    acc = jnp.dot(oa.astype(BF16), wa_ref[...], preferred_element_type=F32)
    acc += lax.dot_general(obt_ref[...], wb_ref[...], (((0,), (0,)), ((), ())),
                           preferred_element_type=F32)
    out_ref[...] = x_ref[...] + acc


def _outproj_even(x2d, oa, lse, obt, wa, wb, *, tm):
    t_total = x2d.shape[0]
    row = lambda i: (i, 0)
    const = lambda i: (0, 0)
    a_spec = pl.BlockSpec((tm, A_GROUP_W), row)
    return pl.pallas_call(
        _outproj_even_kernel,
        grid=(t_total // tm,),
        in_specs=[pl.BlockSpec((tm, D_MODEL), row)] + [a_spec] * 6 + [
            pl.BlockSpec((B_V, tm), lambda i: (0, i)),
            pl.BlockSpec((A_GROUP_W, D_MODEL), const),
            pl.BlockSpec((B_V, D_MODEL), const),
        ],
        out_specs=pl.BlockSpec((tm, D_MODEL), row),
        out_shape=jax.ShapeDtypeStruct((t_total, D_MODEL), F32),
        compiler_params=_params("parallel"),
        name="outproj_even",
    )(x2d, *oa, *lse, obt, wa, wb)


def _outproj_odd_kernel(x_ref, ot_ref, w_ref, out_ref):
    acc = lax.dot_general(ot_ref[...], w_ref[...], (((0,), (0,)), ((), ())), preferred_element_type=F32)
    out_ref[...] = x_ref[...] + acc


def _outproj_odd(x2d, ot, w, *, tm):
    t_total = x2d.shape[0]
    n_in = ot.shape[0]
    return pl.pallas_call(
        _outproj_odd_kernel,
        grid=(t_total // tm,),
        in_specs=[
            pl.BlockSpec((tm, D_MODEL), lambda i: (i, 0)),
            pl.BlockSpec((n_in, tm), lambda i: (0, i)),
            pl.BlockSpec((n_in, D_MODEL), lambda i: (0, 0)),
        ],
        out_specs=pl.BlockSpec((tm, D_MODEL), lambda i: (i, 0)),
        out_shape=jax.ShapeDtypeStruct((t_total, D_MODEL), F32),
        compiler_params=_params("parallel"),
        name="outproj_odd",
    )(x2d, ot, w)


FF_HALO = 16


def _ffn_kernel(xp_ref, x_ref, xn_ref, g_ref, wup_ref, cw_ref, cb_ref, wd_ref, out_ref, hs_ref, acc_ref,
                ug0_ref, ug1_ref, uv0_ref, uv1_ref, act0_ref, act1_ref, *, seq, tm):
    i = pl.program_id(0)
    g = g_ref[...]
    has_prev = ((i * tm) % seq != 0).astype(F32)
    has_next = (((i + 1) * tm) % seq != 0).astype(F32)
    hs_ref[0:FF_HALO, :] = (_rms_rows(xp_ref[...], g) * has_prev).astype(BF16)
    hs_ref[FF_HALO:FF_HALO + tm, :] = _rms_rows(x_ref[...], g).astype(BF16)
    hs_ref[FF_HALO + tm:, :] = (_rms_rows(xn_ref[...], g) * has_next).astype(BF16)
    rows = tm + 2 * FF_HALO
    u_bufs = ((ug0_ref, uv0_ref), (ug1_ref, uv1_ref))
    nchunks = D_FF // FF_CHUNK

    def up_proj(c):
        ug_ref, uv_ref = u_bufs[c % 2]
        hs = hs_ref[...]
        ug_ref[...] = jnp.dot(hs, wup_ref[:, c * FF_CHUNK:(c + 1) * FF_CHUNK], preferred_element_type=F32)
        uv_ref[...] = jnp.dot(hs, wup_ref[:, D_FF + c * FF_CHUNK:D_FF + (c + 1) * FF_CHUNK],
                              preferred_element_type=F32)

    def conv(u_ref, c0):
        u = u_ref[...]
        w = cw_ref[:, c0:c0 + FF_CHUNK]
        b = cb_ref[:, c0:c0 + FF_CHUNK]
        um = pltpu.roll(u, 1, 0)[FF_HALO:FF_HALO + tm]
        up = pltpu.roll(u, rows - 1, 0)[FF_HALO:FF_HALO + tm]
        return um * w[0:1] + u[FF_HALO:FF_HALO + tm] * w[1:2] + up * w[2:3] + b

    act_bufs = (act0_ref, act1_ref)

    def down_proj(c):
        part = jnp.dot(act_bufs[c % 2][...], wd_ref[c * FF_CHUNK:(c + 1) * FF_CHUNK, :],
                       preferred_element_type=F32)
        if c == 0:
            acc_ref[...] = part
        else:
            acc_ref[...] += part

    up_proj(0)
    for c in range(nchunks):
        if c + 1 < nchunks:
            up_proj(c + 1)
        ug_ref, uv_ref = u_bufs[c % 2]
        gate = conv(ug_ref, c * FF_CHUNK)
        act_bufs[c % 2][...] = (gate * jax.nn.sigmoid(gate) * conv(uv_ref, D_FF + c * FF_CHUNK)).astype(BF16)
        if c >= 1:
            down_proj(c - 1)
    down_proj(nchunks - 1)
    out_ref[...] = x_ref[...] + acc_ref[...]


def _ffn(x2d, g, wup, cw, cb, wd, *, seq, tm):
    t_total = x2d.shape[0]
    tm = min(tm, seq)
    r = tm // FF_HALO
    last = t_total // FF_HALO - 1
    const = lambda i: (0, 0)
    resident = dict(pipeline_mode=pl.Buffered(1))
    kern = functools.partial(_ffn_kernel, seq=seq, tm=tm)
    return pl.pallas_call(
        kern,
        grid=(t_total // tm,),
        in_specs=[
            pl.BlockSpec((FF_HALO, D_MODEL), lambda i: (jnp.maximum(i * r - 1, 0), 0)),
            pl.BlockSpec((tm, D_MODEL), lambda i: (i, 0)),
            pl.BlockSpec((FF_HALO, D_MODEL), lambda i: (jnp.minimum((i + 1) * r, last), 0)),
            pl.BlockSpec((1, D_MODEL), const),
            pl.BlockSpec((D_MODEL, 2 * D_FF), const, **resident),
            pl.BlockSpec((3, 2 * D_FF), const),
            pl.BlockSpec((1, 2 * D_FF), const),
            pl.BlockSpec((D_FF, D_MODEL), const, **resident),
        ],
        out_specs=pl.BlockSpec((tm, D_MODEL), lambda i: (i, 0)),
        out_shape=jax.ShapeDtypeStruct((t_total, D_MODEL), F32),
        scratch_shapes=[
            pltpu.VMEM((tm + 2 * FF_HALO, D_MODEL), BF16),
            pltpu.VMEM((tm, D_MODEL), F32),
        ] + [pltpu.VMEM((tm + 2 * FF_HALO, FF_CHUNK), F32)] * 4 + [pltpu.VMEM((tm, FF_CHUNK), BF16)] * 2,
        compiler_params=_params("parallel"),
        name="ffn",
    )(x2d, x2d, x2d, g, wup, cw, cb, wd)


def _rope_tables(pos, dim, theta):
    inv = theta ** (-jnp.arange(0, dim, 2, dtype=F32) / dim)
    ang = pos.astype(F32)[:, None] * inv[None, :]
    return jnp.cos(ang).T, jnp.sin(ang).T


def _class_order(table, dil):
    rows, seq = table.shape
    return table.reshape(rows, seq // dil, dil).transpose(0, 2, 1).reshape(rows, seq)


def _lane_bcast(v, width):
    return jnp.broadcast_to(v.astype(F32)[:, None], (v.shape[0], width))


TM = 512
TQ_C = 512
TQ_B = 1024
TKC = 512
TQ_BAND = 512
Q_SCALE = HEAD_DIM ** -0.5 * LOG2E


def _trunk(x, norm_mix, norm_ffn, w_in_ab, q_norm_a, k_norm_a, q_norm_b, k_norm_b,
           lambda_q1, lambda_k1, lambda_q2, lambda_k2, subln_b, w_out_ab,
           w_in_c, q_norm_c, k_norm_c, w_out_c, w_up, conv_w, conv_b, w_down):
    batch, seq, _ = x.shape
    t_total = batch * seq
    depth = norm_mix.shape[0]
    x2d = x.reshape(t_total, D_MODEL)

    cos, sin = _rope_tables(jnp.arange(seq), ROT_DIM, ROPE_THETA)
    rows = seq // GRID_W
    row = jnp.repeat(jnp.arange(rows), GRID_W)
    col = jnp.tile(jnp.arange(GRID_W), rows)
    cr, sr = _rope_tables(row, AXIAL_DIM, AXIAL_THETA)
    cc, sc = _rope_tables(col, AXIAL_DIM, AXIAL_THETA)
    cos_ax = jnp.concatenate([cr, cc], axis=0)
    sin_ax = jnp.concatenate([sr, sc], axis=0)

    for i in range(depth):
        j = i // 2
        g_mix = norm_mix[i].reshape(1, D_MODEL)
        if i % 2 == 0:
            lam_init = 0.8 - 0.6 * math.exp(-0.3 * i)
            w = w_in_ab[j]
            oa, lse = [], []
            gains_a = jnp.concatenate([_lane_bcast(q_norm_a[j] * Q_SCALE, TM), _lane_bcast(k_norm_a[j], TM)], 0)
            for gi, (_, dil) in enumerate(A_PATTERNS):
                sl = slice(gi * A_GROUP_W, (gi + 1) * A_GROUP_W)
                wt = jnp.concatenate([w[:, sl], w[:, A_QKV:2 * A_QKV][:, sl], w[:, 2 * A_QKV:3 * A_QKV][:, sl]],
                                     axis=1).T.astype(BF16)
                qt, k, vt = _proj(x2d, g_mix, wt, gains_a, _class_order(cos, dil), _class_order(sin, dil),
                                  seq=seq, dil=dil, nq=A_HEADS_PER_GROUP, nk=A_HEADS_PER_GROUP, nv=A_GROUP_W,
                                  axial=False, tm=TM)
                o_g, lse_g = _band(qt, k, vt, batch=batch, seq=seq, dil=dil, tqs=TQ_BAND)
                oa.append(o_g)
                lse.append(lse_g)
            wt_b = w[:, 3 * A_QKV:].T.astype(BF16)
            gains_b = jnp.concatenate([_lane_bcast(q_norm_b[j] * Q_SCALE, TM), _lane_bcast(k_norm_b[j], TM)], 0)
            qt, k, vt = _proj(x2d, g_mix, wt_b, gains_b, cos, sin, seq=seq, dil=1,
                              nq=2 * B_HEADS, nk=2 * B_HEADS, nv=B_V, axial=False, tm=TM)
            lam = (jnp.exp(jnp.sum(lambda_q1[j].astype(F32) * lambda_k1[j].astype(F32)))
                   - jnp.exp(jnp.sum(lambda_q2[j].astype(F32) * lambda_k2[j].astype(F32))) + lam_init)
            sg = _lane_bcast(subln_b[j] * (1.0 - lam_init), TQ_B)
            obt = _attn_b(lam.reshape(1).astype(F32), qt, k, vt, sg, batch=batch, seq=seq, tq=TQ_B, tkc=TKC)
            wo = w_out_ab[j].astype(BF16)
            x2d = _outproj_even(x2d, oa, lse, obt, wo[:A_GROUP_W], wo[A_GROUP_W:], tm=TM)
        else:
            wt = w_in_c[j].T.astype(BF16)
            gains_c = jnp.concatenate([_lane_bcast(q_norm_c[j] * Q_SCALE, TM), _lane_bcast(k_norm_c[j], TM)], 0)
            qt, k, vt = _proj(x2d, g_mix, wt, gains_c, cos_ax, sin_ax, seq=seq, dil=1,
                              nq=C_Q_HEADS, nk=C_KV_HEADS, nv=C_KV_HEADS * HEAD_DIM, axial=True, tm=TM)
            ot = _attn_c(qt, k, vt, batch=batch, seq=seq, tq=TQ_C, tkc=TKC)
            x2d = _outproj_odd(x2d, ot, w_out_c[j].astype(BF16), tm=TM)
        x2d = _ffn(x2d, norm_ffn[i].reshape(1, D_MODEL), w_up[i].astype(BF16), conv_w[i],
                   conv_b[i].reshape(1, 2 * D_FF), w_down[i].astype(BF16), seq=seq, tm=TM)
    return x2d.reshape(batch, seq, D_MODEL)


def kernel(x_prompt, x_sample, norm_mix, norm_ffn, w_in_ab, q_norm_a, k_norm_a, q_norm_b, k_norm_b,
           lambda_q1, lambda_k1, lambda_q2, lambda_k2, subln_b, w_out_ab, w_in_c, q_norm_c, k_norm_c,
           w_out_c, w_up, conv_w, conv_b, w_down):
    params = (norm_mix, norm_ffn, w_in_ab, q_norm_a, k_norm_a, q_norm_b, k_norm_b,
              lambda_q1, lambda_k1, lambda_q2, lambda_k2, subln_b, w_out_ab,
              w_in_c, q_norm_c, k_norm_c, w_out_c, w_up, conv_w, conv_b, w_down)
    return (_trunk(x_prompt, *params), _trunk(x_sample, *params))
```

```python
import functools
import math

import jax
import jax.numpy as jnp
from jax import lax
from jax.experimental import pallas as pl
from jax.experimental.pallas import tpu as pltpu

F32 = jnp.float32
BF16 = jnp.bfloat16

D_MODEL = 1024
HEAD_DIM = 64
A_PATTERNS = ((128, 1), (512, 4), (2048, 16))
A_GROUPS = 3
A_HEADS_PER_GROUP = 4
A_QKV = A_GROUPS * A_HEADS_PER_GROUP * HEAD_DIM
A_GROUP_W = A_HEADS_PER_GROUP * HEAD_DIM
B_HEADS = 4
B_QK = B_HEADS * 2 * HEAD_DIM
B_V = B_HEADS * 2 * HEAD_DIM
C_Q_HEADS = 16
C_KV_HEADS = 4
C_GROUP = C_Q_HEADS // C_KV_HEADS
ROPE_THETA = 500000.0
ROT_DIM = HEAD_DIM // 4
AXIAL_THETA = 10000.0
AXIAL_DIM = HEAD_DIM // 2
GRID_W = 64
D_FF = 2816
NORM_EPS = 1e-6
SUBLN_EPS = 1e-5
NEG_INF = -1e30
LOG2E = 1.4426950408889634
LN2 = 0.6931471805599453

V7X_LANES = 128
V7X_VMEM_BYTES = 64 * 1024 * 1024
VMEM_LIMIT = V7X_VMEM_BYTES - 8 * 1024 * 1024
BAND_HALF = 64
BAND_SUB = 128
BAND_WIN = BAND_SUB + 2 * V7X_LANES
FF_CHUNK = 256


def _params(*semantics):
    return pltpu.CompilerParams(dimension_semantics=semantics, vmem_limit_bytes=VMEM_LIMIT)


def _rms_rows(x, g):
    ms = jnp.mean(x * x, axis=-1, keepdims=True)
    return x * lax.rsqrt(ms + NORM_EPS) * g


def _head_norm_rot(y, gain, cos, sin, axial):
    ssq = jnp.sum(y * y, axis=0, keepdims=True)
    yn = y * lax.rsqrt(ssq * (1.0 / HEAD_DIM) + NORM_EPS) * gain
    if axial:
        h = AXIAL_DIM // 2
        a1, a2, b1, b2 = yn[0:h], yn[h:2 * h], yn[2 * h:3 * h], yn[3 * h:4 * h]
        cr, cc = cos[0:h], cos[h:2 * h]
        sr, sc = sin[0:h], sin[h:2 * h]
        return jnp.concatenate(
            [a1 * cr - a2 * sr, a2 * cr + a1 * sr, b1 * cc - b2 * sc, b2 * cc + b1 * sc], axis=0)
    h = ROT_DIM // 2
    x1, x2 = yn[0:h], yn[h:2 * h]
    return jnp.concatenate([x1 * cos - x2 * sin, x2 * cos + x1 * sin, yn[2 * h:]], axis=0)


def _proj_kernel(x_ref, g_ref, wt_ref, gain_ref, cos_ref, sin_ref, qt_ref, k_ref, vt_ref,
                 *, nq, nk, axial):
    h = _rms_rows(x_ref[...], g_ref[...]).astype(BF16)
    yt = lax.dot_general(wt_ref[...], h, (((1,), (1,)), ((), ())),
                         preferred_element_type=F32)
    cos = cos_ref[...]
    sin = sin_ref[...]
    gq = gain_ref[0:HEAD_DIM, :]
    gk = gain_ref[HEAD_DIM:2 * HEAD_DIM, :]
    for i in range(nq):
        y = yt[i * HEAD_DIM:(i + 1) * HEAD_DIM, :]
        qt_ref[i * HEAD_DIM:(i + 1) * HEAD_DIM, :] = _head_norm_rot(y, gq, cos, sin, axial).astype(BF16)
    base = nq * HEAD_DIM
    for p in range(nk // 2):
        pair = []
        for i in (2 * p, 2 * p + 1):
            y = yt[base + i * HEAD_DIM:base + (i + 1) * HEAD_DIM, :]
            pair.append(_head_norm_rot(y, gk, cos, sin, axial))
        kt = jnp.concatenate(pair, axis=0)
        k_ref[:, p * 2 * HEAD_DIM:(p + 1) * 2 * HEAD_DIM] = kt.T.astype(BF16)
    base = (nq + nk) * HEAD_DIM
    vt_ref[...] = yt[base:, :].astype(BF16)


def _proj(x2d, g, wt, gains, cos_t, sin_t, *, seq, nq, nk, nv, axial, tm):
    t_total = x2d.shape[0]
    tm = min(tm, seq)
    nlb = seq // tm
    n_out = (nq + nk) * HEAD_DIM + nv
    rot_rows = cos_t.shape[0]
    kern = functools.partial(_proj_kernel, nq=nq, nk=nk, axial=axial)
    return pl.pallas_call(
        kern,
        grid=(t_total // tm,),
        in_specs=[
            pl.BlockSpec((tm, D_MODEL), lambda i: (i, 0)),
            pl.BlockSpec((1, D_MODEL), lambda i: (0, 0)),
            pl.BlockSpec((n_out, D_MODEL), lambda i: (0, 0)),
            pl.BlockSpec((2 * HEAD_DIM, tm), lambda i: (0, 0)),
            pl.BlockSpec((rot_rows, tm), lambda i: (0, i % nlb)),
            pl.BlockSpec((rot_rows, tm), lambda i: (0, i % nlb)),
        ],
        out_specs=[
            pl.BlockSpec((nq * HEAD_DIM, tm), lambda i: (0, i)),
            pl.BlockSpec((tm, nk * HEAD_DIM), lambda i: (i, 0)),
            pl.BlockSpec((nv, tm), lambda i: (0, i)),
        ],
        out_shape=[
            jax.ShapeDtypeStruct((nq * HEAD_DIM, t_total), BF16),
            jax.ShapeDtypeStruct((t_total, nk * HEAD_DIM), BF16),
            jax.ShapeDtypeStruct((nv, t_total), BF16),
        ],
        compiler_params=_params("parallel"),
        name="proj",
    )(x2d, g, wt, gains[:, :tm], cos_t, sin_t)


def _proj_cls_kernel(x_ref, g_ref, wt_ref, gain_ref, cos_ref, sin_ref, qt_ref, k_ref, vt_ref, hs_ref,
                     stage_ref, *, dil, n):
    h = _rms_rows(x_ref[...], g_ref[...])
    if dil == 1:
        hs_ref[...] = h.astype(BF16)
    else:
        for c in range(D_MODEL // V7X_LANES):
            stage_ref[c] = h[:, c * V7X_LANES:(c + 1) * V7X_LANES]
        for r in range(dil):
            for c in range(D_MODEL // V7X_LANES):
                hs_ref[r * n:(r + 1) * n, c * V7X_LANES:(c + 1) * V7X_LANES] = (
                    stage_ref[c, pl.ds(r, n, stride=dil), :].astype(BF16))
    yt = lax.dot_general(wt_ref[...], hs_ref[...], (((1,), (1,)), ((), ())),
                         preferred_element_type=F32)
    cos = cos_ref[...]
    sin = sin_ref[...]
    gq = gain_ref[0:HEAD_DIM, :]
    gk = gain_ref[HEAD_DIM:2 * HEAD_DIM, :]
    nh = A_HEADS_PER_GROUP
    for i in range(nh):
        q = _head_norm_rot(yt[i * HEAD_DIM:(i + 1) * HEAD_DIM, :], gq, cos, sin, False).astype(BF16)
        for r in range(dil):
            qt_ref[r, i * HEAD_DIM:(i + 1) * HEAD_DIM, :] = q[:, r * n:(r + 1) * n]
    base = nh * HEAD_DIM
    for p in range(nh // 2):
        pair = [_head_norm_rot(yt[base + i * HEAD_DIM:base + (i + 1) * HEAD_DIM, :], gk, cos, sin, False)
                for i in (2 * p, 2 * p + 1)]
        kt = jnp.concatenate(pair, axis=0).T.astype(BF16)
        for r in range(dil):
            k_ref[r, :, p * 2 * HEAD_DIM:(p + 1) * 2 * HEAD_DIM] = kt[r * n:(r + 1) * n, :]
    base = 2 * nh * HEAD_DIM
    v = yt[base:, :].astype(BF16)
    for r in range(dil):
        vt_ref[r] = v[:, r * n:(r + 1) * n]


def _proj_cls(x2d, g, wt, gains, cos_c, sin_c, *, batch, seq, dil, n):
    t_total = x2d.shape[0]
    tmx = dil * n
    length = seq // dil
    tps = seq // tmx
    kern = functools.partial(_proj_cls_kernel, dil=dil, n=n)
    feat = pl.BlockSpec((dil, A_GROUP_W, n), lambda i: (i // tps, 0, i % tps))
    return pl.pallas_call(
        kern,
        grid=(t_total // tmx,),
        in_specs=[
            pl.BlockSpec((tmx, D_MODEL), lambda i: (i, 0)),
            pl.BlockSpec((1, D_MODEL), lambda i: (0, 0)),
            pl.BlockSpec((3 * A_GROUP_W, D_MODEL), lambda i: (0, 0)),
            pl.BlockSpec((2 * HEAD_DIM, tmx), lambda i: (0, 0)),
            pl.BlockSpec((None, ROT_DIM // 2, tmx), lambda i: (i % tps, 0, 0)),
            pl.BlockSpec((None, ROT_DIM // 2, tmx), lambda i: (i % tps, 0, 0)),
        ],
        out_specs=[feat, pl.BlockSpec((dil, n, A_GROUP_W), lambda i: (i // tps, i % tps, 0)), feat],
        out_shape=[
            jax.ShapeDtypeStruct((batch * dil, A_GROUP_W, length), BF16),
            jax.ShapeDtypeStruct((batch * dil, length, A_GROUP_W), BF16),
            jax.ShapeDtypeStruct((batch * dil, A_GROUP_W, length), BF16),
        ],
        scratch_shapes=[pltpu.VMEM((tmx, D_MODEL), BF16),
                        pltpu.VMEM((D_MODEL // V7X_LANES, tmx if dil > 1 else 8, V7X_LANES), F32)],
        compiler_params=_params("parallel"),
        name="proj_cls",
    )(x2d, g, wt, gains, cos_c, sin_c)


FLASH_UNROLL = 4
FLASH_COL_TILE = 256
V_PAD = 16


def _flash_scratch(dv, ncols, seq, tkc):
    return [
        pltpu.VMEM((2 * HEAD_DIM, ncols), BF16),
        pltpu.VMEM((dv + V_PAD, seq), BF16),
        pltpu.VMEM((dv + V_PAD, ncols), F32),
        pltpu.VMEM((1, ncols), F32),
        pltpu.VMEM((tkc, ncols), F32), pltpu.VMEM((tkc, ncols), F32),
        pltpu.VMEM((1, ncols), F32), pltpu.VMEM((1, ncols), F32),
    ]


def _flash_unit(k_ref, vt_ref, scratch, *, dv, seq, tkc):
    qpad_ref, vext_ref, acc_ref, m_ref, s0, s1, c0, c1 = scratch
    s_bufs, cm_bufs = (s0, s1), (c0, c1)
    nc = seq // tkc

    vext_ref[0:dv, :] = vt_ref[...]
    ones_row = lax.broadcasted_iota(jnp.int32, (V_PAD, seq), 0) == 0
    vext_ref[dv:, :] = jnp.where(ones_row, 1.0, 0.0).astype(BF16)
    m_ref[...] = jnp.full(m_ref.shape, NEG_INF, F32)
    acc_ref[...] = jnp.zeros(acc_ref.shape, F32)

    def chunk(c):
        return pl.ds(c * tkc if isinstance(c, int) else pl.multiple_of(c * tkc, tkc), tkc)

    ncols = qpad_ref.shape[1]
    tiles = [slice(j * FLASH_COL_TILE, (j + 1) * FLASH_COL_TILE) for j in range(ncols // FLASH_COL_TILE)]

    def scores(c, slot, t):
        s = jnp.dot(k_ref[chunk(c), :], qpad_ref[:, t], preferred_element_type=F32)
        s_bufs[slot][:, t] = s
        cm_bufs[slot][:, t] = jnp.max(s, axis=0, keepdims=True)

    def softmax_pv(c, slot, t):
        m_old = m_ref[:, t]
        m_new = jnp.maximum(m_old, cm_bufs[slot][:, t])
        alpha = jnp.exp2(m_old - m_new)
        m_ref[:, t] = m_new
        p = jnp.exp2((s_bufs[slot][:, t] - m_new).astype(BF16))
        pv = jnp.dot(vext_ref[:, chunk(c)], p, preferred_element_type=F32)
        acc_ref[:, t] = alpha * acc_ref[:, t] + pv

    def step(c, slot, last=False):
        for t in tiles:
            if not last:
                scores(c + 1, 1 - slot, t)
            softmax_pv(c, slot, t)

    for t in tiles:
        scores(0, 0, t)
    unroll = FLASH_UNROLL if nc % FLASH_UNROLL == 0 else 2
    if nc > unroll:
        def group(t, carry):
            for u in range(unroll):
                step(unroll * t + u, u % 2)
            return carry

        lax.fori_loop(0, nc // unroll - 1, group, 0)
    tail = min(nc, unroll)
    for c in range(nc - tail, nc):
        step(c, c % 2, last=(c == nc - 1))


def _attn_c_kernel(qt_ref, k_ref, vt_ref, o_ref, *scratch, seq, tq, tkc):
    n = pl.program_id(1)
    qpad_ref, acc_ref = scratch[0], scratch[2]
    qcat = jnp.concatenate([qt_ref[g * HEAD_DIM:(g + 1) * HEAD_DIM, :] for g in range(C_GROUP)], axis=1)
    zeros = jnp.zeros_like(qcat)

    @pl.when(n % 2 == 0)
    def _():
        qpad_ref[0:HEAD_DIM, :] = qcat
        qpad_ref[HEAD_DIM:, :] = zeros

    @pl.when(n % 2 == 1)
    def _():
        qpad_ref[0:HEAD_DIM, :] = zeros
        qpad_ref[HEAD_DIM:, :] = qcat

    _flash_unit(k_ref, vt_ref, scratch, dv=HEAD_DIM, seq=seq, tkc=tkc)
    o = acc_ref[0:HEAD_DIM, :] * (1.0 / acc_ref[HEAD_DIM:HEAD_DIM + 1, :])
    for g in range(C_GROUP):
        o_ref[g * HEAD_DIM:(g + 1) * HEAD_DIM, :] = o[:, g * tq:(g + 1) * tq].astype(BF16)


def _attn_c(qt, k, vt, *, batch, seq, tq, tkc):
    t_total = batch * seq
    tq = min(tq, seq)
    tkc = min(tkc, seq)
    nqb = seq // tq
    ncols = C_GROUP * tq
    kern = functools.partial(_attn_c_kernel, seq=seq, tq=tq, tkc=tkc)
    return pl.pallas_call(
        kern,
        grid=(batch, C_KV_HEADS, nqb),
        in_specs=[
            pl.BlockSpec((C_GROUP * HEAD_DIM, tq), lambda b, n, i: (n, b * nqb + i)),
            pl.BlockSpec((seq, 2 * HEAD_DIM), lambda b, n, i: (b, n // 2)),
            pl.BlockSpec((HEAD_DIM, seq), lambda b, n, i: (n, b)),
        ],
        out_specs=pl.BlockSpec((C_GROUP * HEAD_DIM, tq), lambda b, n, i: (n, b * nqb + i)),
        out_shape=jax.ShapeDtypeStruct((C_Q_HEADS * HEAD_DIM, t_total), BF16),
        scratch_shapes=_flash_scratch(HEAD_DIM, ncols, seq, tkc),
        compiler_params=_params("parallel", "parallel", "parallel"),
        name="attn_c",
    )(qt, k, vt)


def _attn_b_kernel(lam_ref, qt_ref, k_ref, vt_ref, sg_ref, o_ref, *scratch, seq, tq, tkc):
    dv = 2 * HEAD_DIM
    qpad_ref, acc_ref = scratch[0], scratch[2]
    q = qt_ref[...]
    zeros = jnp.zeros((HEAD_DIM, tq), BF16)
    qpad_ref[0:HEAD_DIM, 0:tq] = q[0:HEAD_DIM]
    qpad_ref[0:HEAD_DIM, tq:] = zeros
    qpad_ref[HEAD_DIM:, 0:tq] = zeros
    qpad_ref[HEAD_DIM:, tq:] = q[HEAD_DIM:]
    _flash_unit(k_ref, vt_ref, scratch, dv=dv, seq=seq, tkc=tkc)
    on = acc_ref[0:dv, :] * (1.0 / acc_ref[dv:dv + 1, :])
    o = on[:, 0:tq] - lam_ref[0] * on[:, tq:]
    ms = jnp.mean(o * o, axis=0, keepdims=True)
    o_ref[...] = (o * lax.rsqrt(ms + SUBLN_EPS) * sg_ref[...]).astype(BF16)


def _attn_b(lam, qt, k, vt, sg, *, batch, seq, tq, tkc):
    t_total = batch * seq
    tq = min(tq, seq)
    tkc = min(tkc, seq)
    nqb = seq // tq
    kern = functools.partial(_attn_b_kernel, seq=seq, tq=tq, tkc=tkc)
    return pl.pallas_call(
        kern,
        grid=(batch, B_HEADS, nqb),
        in_specs=[
            pl.BlockSpec(memory_space=pltpu.SMEM),
            pl.BlockSpec((2 * HEAD_DIM, tq), lambda b, h, i: (h, b * nqb + i)),
            pl.BlockSpec((seq, 2 * HEAD_DIM), lambda b, h, i: (b, h)),
            pl.BlockSpec((2 * HEAD_DIM, seq), lambda b, h, i: (h, b)),
            pl.BlockSpec((2 * HEAD_DIM, tq), lambda b, h, i: (0, 0)),
        ],
        out_specs=pl.BlockSpec((2 * HEAD_DIM, tq), lambda b, h, i: (h, b * nqb + i)),
        out_shape=jax.ShapeDtypeStruct((B_V, t_total), BF16),
        scratch_shapes=_flash_scratch(2 * HEAD_DIM, 2 * tq, seq, tkc),
        compiler_params=_params("parallel", "parallel", "parallel"),
        name="attn_b",
    )(lam, qt, k, vt, sg[:, :tq])


def _band_kernel(qt_ref, kp_ref, km_ref, kn_ref, vp_ref, vm_ref, vn_ref, o_ref, lse_ref, *, length, tqs):
    l0 = pl.program_id(1) * tqs
    kwin = jnp.concatenate([kp_ref[...], km_ref[...], kn_ref[...]], axis=0)
    vwin = jnp.concatenate([vp_ref[...], vm_ref[...], vn_ref[...]], axis=1)
    row = lax.broadcasted_iota(jnp.int32, (BAND_WIN, BAND_SUB), 0)
    col = lax.broadcasted_iota(jnp.int32, (BAND_WIN, BAND_SUB), 1)
    rel = row - V7X_LANES - col
    in_band = jnp.abs(rel) <= BAND_HALF
    zeros = jnp.zeros((HEAD_DIM, BAND_SUB), BF16)
    for j in range(tqs // BAND_SUB):
        c0 = j * BAND_SUB
        q = qt_ref[:, c0:c0 + BAND_SUB]
        kw = kwin[c0:c0 + BAND_WIN]
        vw = vwin[:, c0:c0 + BAND_WIN]
        kpos = l0 + (c0 - V7X_LANES) + row
        valid = in_band & (kpos >= 0) & (kpos < length)
        valid2 = jnp.concatenate([valid, valid], axis=1)
        outs, lses = [], []
        for p in range(A_HEADS_PER_GROUP // 2):
            ha, hb = 2 * p, 2 * p + 1
            qa = q[ha * HEAD_DIM:(ha + 1) * HEAD_DIM]
            qb = q[hb * HEAD_DIM:(hb + 1) * HEAD_DIM]
            qpad = jnp.concatenate([jnp.concatenate([qa, zeros], axis=1),
                                    jnp.concatenate([zeros, qb], axis=1)], axis=0)
            s = jnp.dot(kw[:, p * 2 * HEAD_DIM:(p + 1) * 2 * HEAD_DIM], qpad,
                        preferred_element_type=F32)
            s = jnp.where(valid2, s, NEG_INF)
            m = jnp.max(s, axis=0, keepdims=True)
            pr = jnp.exp2(s - m)
            l = jnp.sum(pr, axis=0, keepdims=True)
            pb = pr.astype(BF16)
            oa = jnp.dot(vw[ha * HEAD_DIM:(ha + 1) * HEAD_DIM], pb[:, :BAND_SUB], preferred_element_type=F32)
            ob = jnp.dot(vw[hb * HEAD_DIM:(hb + 1) * HEAD_DIM], pb[:, BAND_SUB:], preferred_element_type=F32)
            inv = 1.0 / l
            lse = m * LN2 + jnp.log(l)
            outs += [oa * inv[:, :BAND_SUB], ob * inv[:, BAND_SUB:]]
            lses += [jnp.broadcast_to(lse[:, :BAND_SUB], (HEAD_DIM, BAND_SUB)),
                     jnp.broadcast_to(lse[:, BAND_SUB:], (HEAD_DIM, BAND_SUB))]
        o_ref[c0:c0 + BAND_SUB, :] = jnp.concatenate(outs, axis=0).T
        lse_ref[c0:c0 + BAND_SUB, :] = jnp.concatenate(lses, axis=0).T


def _band(qt, k, vt, *, batch, seq, dil, tqs):
    length = seq // dil
    tqs = min(tqs, length)
    nqb = length // tqs
    ncls = batch * dil
    nkb = length // V7X_LANES
    r128 = tqs // V7X_LANES

    def prev_blk(i):
        return jnp.maximum(i * r128 - 1, 0)

    def next_blk(i):
        return jnp.minimum((i + 1) * r128, nkb - 1)

    kern = functools.partial(_band_kernel, length=length, tqs=tqs)
    out_spec = pl.BlockSpec((None, tqs, A_GROUP_W), lambda c, i: (c, i, 0))
    return pl.pallas_call(
        kern,
        grid=(ncls, nqb),
        in_specs=[
            pl.BlockSpec((None, A_GROUP_W, tqs), lambda c, i: (c, 0, i)),
            pl.BlockSpec((None, V7X_LANES, A_GROUP_W), lambda c, i: (c, prev_blk(i), 0)),
            pl.BlockSpec((None, tqs, A_GROUP_W), lambda c, i: (c, i, 0)),
            pl.BlockSpec((None, V7X_LANES, A_GROUP_W), lambda c, i: (c, next_blk(i), 0)),
            pl.BlockSpec((None, A_GROUP_W, V7X_LANES), lambda c, i: (c, 0, prev_blk(i))),
            pl.BlockSpec((None, A_GROUP_W, tqs), lambda c, i: (c, 0, i)),
            pl.BlockSpec((None, A_GROUP_W, V7X_LANES), lambda c, i: (c, 0, next_blk(i))),
        ],
        out_specs=[out_spec, out_spec],
        out_shape=[jax.ShapeDtypeStruct((ncls, length, A_GROUP_W), F32)] * 2,
        compiler_params=_params("parallel", "parallel"),
        name="band",
    )(qt, k, k, k, vt, vt, vt)


def _outproj_even_kernel(x_ref, o0_ref, o1_ref, o2_ref, l0_ref, l1_ref, l2_ref, obt_ref, wa_ref, wb_ref,
                         out_ref, *bufs, tm):
    def natural(blk_ref, buf_ref, dil):
        if dil == 1:
            return blk_ref[0]
        ntile = A_GROUP_W // V7X_LANES
        for r in range(dil):
            for c in range(ntile):
                buf_ref[c, pl.ds(r, tm // dil, stride=dil), :] = blk_ref[r, :, c * V7X_LANES:(c + 1) * V7X_LANES]
        return jnp.concatenate([buf_ref[c] for c in range(ntile)], axis=1)

    dils = [d for _, d in A_PATTERNS]
    o = [natural(ref, bufs[2 * gi], d) for gi, (ref, d) in enumerate(zip((o0_ref, o1_ref, o2_ref), dils))]
    l = [natural(ref, bufs[2 * gi + 1], d) for gi, (ref, d) in enumerate(zip((l0_ref, l1_ref, l2_ref), dils))]
    m = jnp.maximum(jnp.maximum(l[0], l[1]), l[2])
    w0, w1, w2 = jnp.exp(l[0] - m), jnp.exp(l[1] - m), jnp.exp(l[2] - m)
    oa = (w0 * o[0] + w1 * o[1] + w2 * o[2]) / (w0 + w1 + w2)
    acc = jnp.dot(oa.astype(BF16), wa_ref[...], preferred_element_type=F32)
    acc += lax.dot_general(obt_ref[...], wb_ref[...], (((0,), (0,)), ((), ())),
                           preferred_element_type=F32)
    out_ref[...] = x_ref[...] + acc


def _outproj_even(x2d, oa, lse, obt, wa, wb, *, seq, tm):
    t_total = x2d.shape[0]
    tps = seq // tm
    row = lambda i: (i, 0)
    const = lambda i: (0, 0)
    a_specs = [pl.BlockSpec((d, tm // d, A_GROUP_W), lambda i: (i // tps, i % tps, 0)) for _, d in A_PATTERNS]
    return pl.pallas_call(
        functools.partial(_outproj_even_kernel, tm=tm),
        grid=(t_total // tm,),
        in_specs=[pl.BlockSpec((tm, D_MODEL), row)] + a_specs * 2 + [
            pl.BlockSpec((B_V, tm), lambda i: (0, i)),
            pl.BlockSpec((A_GROUP_W, D_MODEL), const),
            pl.BlockSpec((B_V, D_MODEL), const),
        ],
        out_specs=pl.BlockSpec((tm, D_MODEL), row),
        out_shape=jax.ShapeDtypeStruct((t_total, D_MODEL), F32),
        scratch_shapes=[pltpu.VMEM((A_GROUP_W // V7X_LANES, tm, V7X_LANES), F32)] * (2 * A_GROUPS),
        compiler_params=_params("parallel"),
        name="outproj_even",
    )(x2d, *oa, *lse, obt, wa, wb)


def _outproj_odd_kernel(x_ref, ot_ref, w_ref, out_ref):
    acc = lax.dot_general(ot_ref[...], w_ref[...], (((0,), (0,)), ((), ())), preferred_element_type=F32)
    out_ref[...] = x_ref[...] + acc


def _outproj_odd(x2d, ot, w, *, tm):
    t_total = x2d.shape[0]
    n_in = ot.shape[0]
    return pl.pallas_call(
        _outproj_odd_kernel,
        grid=(t_total // tm,),
        in_specs=[
            pl.BlockSpec((tm, D_MODEL), lambda i: (i, 0)),
            pl.BlockSpec((n_in, tm), lambda i: (0, i)),
            pl.BlockSpec((n_in, D_MODEL), lambda i: (0, 0)),
        ],
        out_specs=pl.BlockSpec((tm, D_MODEL), lambda i: (i, 0)),
        out_shape=jax.ShapeDtypeStruct((t_total, D_MODEL), F32),
        compiler_params=_params("parallel"),
        name="outproj_odd",
    )(x2d, ot, w)


FF_HALO = 16


def _ffn_kernel(xp_ref, x_ref, xn_ref, g_ref, wup_ref, cw_ref, cb_ref, wd_ref, out_ref, hs_ref, acc_ref,
                ug0_ref, ug1_ref, uv0_ref, uv1_ref, act0_ref, act1_ref, *, seq, tm):
    i = pl.program_id(0)
    g = g_ref[...]
    has_prev = ((i * tm) % seq != 0).astype(F32)
    has_next = (((i + 1) * tm) % seq != 0).astype(F32)
    hs_ref[0:FF_HALO, :] = (_rms_rows(xp_ref[...], g) * has_prev).astype(BF16)
    hs_ref[FF_HALO:FF_HALO + tm, :] = _rms_rows(x_ref[...], g).astype(BF16)
    hs_ref[FF_HALO + tm:, :] = (_rms_rows(xn_ref[...], g) * has_next).astype(BF16)
    rows = tm + 2 * FF_HALO
    u_bufs = ((ug0_ref, uv0_ref), (ug1_ref, uv1_ref))
    nchunks = D_FF // FF_CHUNK

    def up_proj(c):
        ug_ref, uv_ref = u_bufs[c % 2]
        hs = hs_ref[...]
        ug_ref[...] = jnp.dot(hs, wup_ref[:, c * FF_CHUNK:(c + 1) * FF_CHUNK], preferred_element_type=F32)
        uv_ref[...] = jnp.dot(hs, wup_ref[:, D_FF + c * FF_CHUNK:D_FF + (c + 1) * FF_CHUNK],
                              preferred_element_type=F32)

    def conv(u_ref, c0):
        u = u_ref[...]
        w = cw_ref[:, c0:c0 + FF_CHUNK]
        b = cb_ref[:, c0:c0 + FF_CHUNK]
        um = pltpu.roll(u, 1, 0)[FF_HALO:FF_HALO + tm]
        up = pltpu.roll(u, rows - 1, 0)[FF_HALO:FF_HALO + tm]
        return um * w[0:1] + u[FF_HALO:FF_HALO + tm] * w[1:2] + up * w[2:3] + b

    act_bufs = (act0_ref, act1_ref)

    def down_proj(c):
        part = jnp.dot(act_bufs[c % 2][...], wd_ref[c * FF_CHUNK:(c + 1) * FF_CHUNK, :],
                       preferred_element_type=F32)
        if c == 0:
            acc_ref[...] = part
        else:
            acc_ref[...] += part

    up_proj(0)
    for c in range(nchunks):
        if c + 1 < nchunks:
            up_proj(c + 1)
        ug_ref, uv_ref = u_bufs[c % 2]
        gate = conv(ug_ref, c * FF_CHUNK)
        act_bufs[c % 2][...] = (gate * jax.nn.sigmoid(gate) * conv(uv_ref, D_FF + c * FF_CHUNK)).astype(BF16)
        if c >= 1:
            down_proj(c - 1)
    down_proj(nchunks - 1)
    out_ref[...] = x_ref[...] + acc_ref[...]


def _ffn(x2d, g, wup, cw, cb, wd, *, seq, tm):
    t_total = x2d.shape[0]
    tm = min(tm, seq)
    r = tm // FF_HALO
    last = t_total // FF_HALO - 1
    const = lambda i: (0, 0)
    resident = dict(pipeline_mode=pl.Buffered(1))
    kern = functools.partial(_ffn_kernel, seq=seq, tm=tm)
    return pl.pallas_call(
        kern,
        grid=(t_total // tm,),
        in_specs=[
            pl.BlockSpec((FF_HALO, D_MODEL), lambda i: (jnp.maximum(i * r - 1, 0), 0)),
            pl.BlockSpec((tm, D_MODEL), lambda i: (i, 0)),
            pl.BlockSpec((FF_HALO, D_MODEL), lambda i: (jnp.minimum((i + 1) * r, last), 0)),
            pl.BlockSpec((1, D_MODEL), const),
            pl.BlockSpec((D_MODEL, 2 * D_FF), const, **resident),
            pl.BlockSpec((3, 2 * D_FF), const),
            pl.BlockSpec((1, 2 * D_FF), const),
            pl.BlockSpec((D_FF, D_MODEL), const, **resident),
        ],
        out_specs=pl.BlockSpec((tm, D_MODEL), lambda i: (i, 0)),
        out_shape=jax.ShapeDtypeStruct((t_total, D_MODEL), F32),
        scratch_shapes=[
            pltpu.VMEM((tm + 2 * FF_HALO, D_MODEL), BF16),
            pltpu.VMEM((tm, D_MODEL), F32),
        ] + [pltpu.VMEM((tm + 2 * FF_HALO, FF_CHUNK), F32)] * 4 + [pltpu.VMEM((tm, FF_CHUNK), BF16)] * 2,
        compiler_params=_params("parallel"),
        name="ffn",
    )(x2d, x2d, x2d, g, wup, cw, cb, wd)


def _rope_tables(pos, dim, theta):
    inv = theta ** (-jnp.arange(0, dim, 2, dtype=F32) / dim)
    ang = pos.astype(F32)[:, None] * inv[None, :]
    return jnp.cos(ang).T, jnp.sin(ang).T


def _class_tiles(table, dil, n):
    rows, seq = table.shape
    return table.reshape(rows, seq // (dil * n), n, dil).transpose(1, 0, 3, 2).reshape(-1, rows, dil * n)


def _lane_bcast(v, width):
    return jnp.broadcast_to(v.astype(F32)[:, None], (v.shape[0], width))


TM = 512
TQ_C = 512
TQ_B = 1024
TKC = 512
TQ_BAND = 512
A_CLASS_TOKENS = (512, 512, 128)
Q_SCALE = HEAD_DIM ** -0.5 * LOG2E


def _trunk(x, norm_mix, norm_ffn, w_in_ab, q_norm_a, k_norm_a, q_norm_b, k_norm_b,
           lambda_q1, lambda_k1, lambda_q2, lambda_k2, subln_b, w_out_ab,
           w_in_c, q_norm_c, k_norm_c, w_out_c, w_up, conv_w, conv_b, w_down):
    batch, seq, _ = x.shape
    t_total = batch * seq
    depth = norm_mix.shape[0]
    x2d = x.reshape(t_total, D_MODEL)

    cos, sin = _rope_tables(jnp.arange(seq), ROT_DIM, ROPE_THETA)
    rows = seq // GRID_W
    row = jnp.repeat(jnp.arange(rows), GRID_W)
    col = jnp.tile(jnp.arange(GRID_W), rows)
    cr, sr = _rope_tables(row, AXIAL_DIM, AXIAL_THETA)
    cc, sc = _rope_tables(col, AXIAL_DIM, AXIAL_THETA)
    cos_ax = jnp.concatenate([cr, cc], axis=0)
    sin_ax = jnp.concatenate([sr, sc], axis=0)

    for i in range(depth):
        j = i // 2
        g_mix = norm_mix[i].reshape(1, D_MODEL)
        if i % 2 == 0:
            lam_init = 0.8 - 0.6 * math.exp(-0.3 * i)
            w = w_in_ab[j]
            oa, lse = [], []
            for gi, (_, dil) in enumerate(A_PATTERNS):
                n_cls = min(A_CLASS_TOKENS[gi], seq // dil)
                width = dil * n_cls
                gains_a = jnp.concatenate([_lane_bcast(q_norm_a[j] * Q_SCALE, width),
                                           _lane_bcast(k_norm_a[j], width)], 0)
                sl = slice(gi * A_GROUP_W, (gi + 1) * A_GROUP_W)
                wt = jnp.concatenate([w[:, sl], w[:, A_QKV:2 * A_QKV][:, sl], w[:, 2 * A_QKV:3 * A_QKV][:, sl]],
                                     axis=1).T.astype(BF16)
                qt, k, vt = _proj_cls(x2d, g_mix, wt, gains_a, _class_tiles(cos, dil, n_cls),
                                      _class_tiles(sin, dil, n_cls), batch=batch, seq=seq, dil=dil, n=n_cls)
                o_g, lse_g = _band(qt, k, vt, batch=batch, seq=seq, dil=dil, tqs=TQ_BAND)
                oa.append(o_g)
                lse.append(lse_g)
            wt_b = w[:, 3 * A_QKV:].T.astype(BF16)
            gains_b = jnp.concatenate([_lane_bcast(q_norm_b[j] * Q_SCALE, TM), _lane_bcast(k_norm_b[j], TM)], 0)
            qt, k, vt = _proj(x2d, g_mix, wt_b, gains_b, cos, sin, seq=seq,
                              nq=2 * B_HEADS, nk=2 * B_HEADS, nv=B_V, axial=False, tm=TM)
            lam = (jnp.exp(jnp.sum(lambda_q1[j].astype(F32) * lambda_k1[j].astype(F32)))
                   - jnp.exp(jnp.sum(lambda_q2[j].astype(F32) * lambda_k2[j].astype(F32))) + lam_init)
            sg = _lane_bcast(subln_b[j] * (1.0 - lam_init), TQ_B)
            obt = _attn_b(lam.reshape(1).astype(F32), qt, k, vt, sg, batch=batch, seq=seq, tq=TQ_B, tkc=TKC)
            wo = w_out_ab[j].astype(BF16)
            x2d = _outproj_even(x2d, oa, lse, obt, wo[:A_GROUP_W], wo[A_GROUP_W:], seq=seq, tm=TM)
        else:
            wt = w_in_c[j].T.astype(BF16)
            gains_c = jnp.concatenate([_lane_bcast(q_norm_c[j] * Q_SCALE, TM), _lane_bcast(k_norm_c[j], TM)], 0)
            qt, k, vt = _proj(x2d, g_mix, wt, gains_c, cos_ax, sin_ax, seq=seq,
                              nq=C_Q_HEADS, nk=C_KV_HEADS, nv=C_KV_HEADS * HEAD_DIM, axial=True, tm=TM)
            ot = _attn_c(qt, k, vt, batch=batch, seq=seq, tq=TQ_C, tkc=TKC)
            x2d = _outproj_odd(x2d, ot, w_out_c[j].astype(BF16), tm=TM)
        x2d = _ffn(x2d, norm_ffn[i].reshape(1, D_MODEL), w_up[i].astype(BF16), conv_w[i],
                   conv_b[i].reshape(1, 2 * D_FF), w_down[i].astype(BF16), seq=seq, tm=TM)
    return x2d.reshape(batch, seq, D_MODEL)


def kernel(x_prompt, x_sample, norm_mix, norm_ffn, w_in_ab, q_norm_a, k_norm_a, q_norm_b, k_norm_b,
           lambda_q1, lambda_k1, lambda_q2, lambda_k2, subln_b, w_out_ab, w_in_c, q_norm_c, k_norm_c,
           w_out_c, w_up, conv_w, conv_b, w_down):
    params = (norm_mix, norm_ffn, w_in_ab, q_norm_a, k_norm_a, q_norm_b, k_norm_b,
              lambda_q1, lambda_k1, lambda_q2, lambda_k2, subln_b, w_out_ab,
              w_in_c, q_norm_c, k_norm_c, w_out_c, w_up, conv_w, conv_b, w_down)
    return (_trunk(x_prompt, *params), _trunk(x_sample, *params))
```

```python
import functools
import math

import jax
import jax.numpy as jnp
from jax import lax
from jax.experimental import pallas as pl
from jax.experimental.pallas import tpu as pltpu

F32 = jnp.float32
BF16 = jnp.bfloat16

D_MODEL = 1024
HEAD_DIM = 64
A_PATTERNS = ((128, 1), (512, 4), (2048, 16))
A_GROUPS = 3
A_HEADS_PER_GROUP = 4
A_QKV = A_GROUPS * A_HEADS_PER_GROUP * HEAD_DIM
A_GROUP_W = A_HEADS_PER_GROUP * HEAD_DIM
B_HEADS = 4
B_QK = B_HEADS * 2 * HEAD_DIM
B_V = B_HEADS * 2 * HEAD_DIM
C_Q_HEADS = 16
C_KV_HEADS = 4
C_GROUP = C_Q_HEADS // C_KV_HEADS
ROPE_THETA = 500000.0
ROT_DIM = HEAD_DIM // 4
AXIAL_THETA = 10000.0
AXIAL_DIM = HEAD_DIM // 2
GRID_W = 64
D_FF = 2816
NORM_EPS = 1e-6
SUBLN_EPS = 1e-5
NEG_INF = -1e30
LOG2E = 1.4426950408889634
LN2 = 0.6931471805599453

V7X_LANES = 128
V7X_VMEM_BYTES = 64 * 1024 * 1024
VMEM_LIMIT = V7X_VMEM_BYTES - 8 * 1024 * 1024
BAND_HALF = 64
BAND_SUB = 128
BAND_WIN = BAND_SUB + 2 * V7X_LANES
FF_CHUNK = 256


def _params(*semantics):
    return pltpu.CompilerParams(dimension_semantics=semantics, vmem_limit_bytes=VMEM_LIMIT)


def _rms_rows(x, g):
    ms = jnp.mean(x * x, axis=-1, keepdims=True)
    return x * lax.rsqrt(ms + NORM_EPS) * g


def _head_norm_rot(y, gain, cos, sin, axial):
    ssq = jnp.sum(y * y, axis=0, keepdims=True)
    yn = y * lax.rsqrt(ssq * (1.0 / HEAD_DIM) + NORM_EPS) * gain
    if axial:
        h = AXIAL_DIM // 2
        a1, a2, b1, b2 = yn[0:h], yn[h:2 * h], yn[2 * h:3 * h], yn[3 * h:4 * h]
        cr, cc = cos[0:h], cos[h:2 * h]
        sr, sc = sin[0:h], sin[h:2 * h]
        return jnp.concatenate(
            [a1 * cr - a2 * sr, a2 * cr + a1 * sr, b1 * cc - b2 * sc, b2 * cc + b1 * sc], axis=0)
    h = ROT_DIM // 2
    x1, x2 = yn[0:h], yn[h:2 * h]
    return jnp.concatenate([x1 * cos - x2 * sin, x2 * cos + x1 * sin, yn[2 * h:]], axis=0)


def _proj_kernel(x_ref, g_ref, wt_ref, gain_ref, cos_ref, sin_ref, qt_ref, k_ref, vt_ref,
                 *, nq, nk, axial):
    h = _rms_rows(x_ref[...], g_ref[...]).astype(BF16)
    yt = lax.dot_general(wt_ref[...], h, (((1,), (1,)), ((), ())),
                         preferred_element_type=F32)
    cos = cos_ref[...]
    sin = sin_ref[...]
    gq = gain_ref[0:HEAD_DIM, :]
    gk = gain_ref[HEAD_DIM:2 * HEAD_DIM, :]
    for i in range(nq):
        y = yt[i * HEAD_DIM:(i + 1) * HEAD_DIM, :]
        qt_ref[i * HEAD_DIM:(i + 1) * HEAD_DIM, :] = _head_norm_rot(y, gq, cos, sin, axial).astype(BF16)
    base = nq * HEAD_DIM
    for p in range(nk // 2):
        pair = []
        for i in (2 * p, 2 * p + 1):
            y = yt[base + i * HEAD_DIM:base + (i + 1) * HEAD_DIM, :]
            pair.append(_head_norm_rot(y, gk, cos, sin, axial))
        kt = jnp.concatenate(pair, axis=0)
        k_ref[:, p * 2 * HEAD_DIM:(p + 1) * 2 * HEAD_DIM] = kt.T.astype(BF16)
    base = (nq + nk) * HEAD_DIM
    vt_ref[...] = yt[base:, :].astype(BF16)


def _proj(x2d, g, wt, gains, cos_t, sin_t, *, seq, nq, nk, nv, axial, tm):
    t_total = x2d.shape[0]
    tm = min(tm, seq)
    nlb = seq // tm
    n_out = (nq + nk) * HEAD_DIM + nv
    rot_rows = cos_t.shape[0]
    kern = functools.partial(_proj_kernel, nq=nq, nk=nk, axial=axial)
    return pl.pallas_call(
        kern,
        grid=(t_total // tm,),
        in_specs=[
            pl.BlockSpec((tm, D_MODEL), lambda i: (i, 0)),
            pl.BlockSpec((1, D_MODEL), lambda i: (0, 0)),
            pl.BlockSpec((n_out, D_MODEL), lambda i: (0, 0)),
            pl.BlockSpec((2 * HEAD_DIM, tm), lambda i: (0, 0)),
            pl.BlockSpec((rot_rows, tm), lambda i: (0, i % nlb)),
            pl.BlockSpec((rot_rows, tm), lambda i: (0, i % nlb)),
        ],
        out_specs=[
            pl.BlockSpec((nq * HEAD_DIM, tm), lambda i: (0, i)),
            pl.BlockSpec((tm, nk * HEAD_DIM), lambda i: (i, 0)),
            pl.BlockSpec((nv, tm), lambda i: (0, i)),
        ],
        out_shape=[
            jax.ShapeDtypeStruct((nq * HEAD_DIM, t_total), BF16),
            jax.ShapeDtypeStruct((t_total, nk * HEAD_DIM), BF16),
            jax.ShapeDtypeStruct((nv, t_total), BF16),
        ],
        compiler_params=_params("parallel"),
        name="proj",
    )(x2d, g, wt, gains[:, :tm], cos_t, sin_t)


def _proj_cls_kernel(x_ref, g_ref, wt_ref, gain_ref, cos_ref, sin_ref, qt_ref, k_ref, vt_ref, hs_ref,
                     stage_ref, *, dil, n):
    h = _rms_rows(x_ref[...], g_ref[...])
    if dil == 1:
        hs_ref[...] = h.astype(BF16)
    else:
        for c in range(D_MODEL // V7X_LANES):
            stage_ref[c] = h[:, c * V7X_LANES:(c + 1) * V7X_LANES]
        for r in range(dil):
            for c in range(D_MODEL // V7X_LANES):
                hs_ref[r * n:(r + 1) * n, c * V7X_LANES:(c + 1) * V7X_LANES] = (
                    stage_ref[c, pl.ds(r, n, stride=dil), :].astype(BF16))
    yt = lax.dot_general(wt_ref[...], hs_ref[...], (((1,), (1,)), ((), ())),
                         preferred_element_type=F32)
    cos = cos_ref[...]
    sin = sin_ref[...]
    gq = gain_ref[0:HEAD_DIM, :]
    gk = gain_ref[HEAD_DIM:2 * HEAD_DIM, :]
    nh = A_HEADS_PER_GROUP
    for i in range(nh):
        q = _head_norm_rot(yt[i * HEAD_DIM:(i + 1) * HEAD_DIM, :], gq, cos, sin, False).astype(BF16)
        for r in range(dil):
            qt_ref[r, i * HEAD_DIM:(i + 1) * HEAD_DIM, :] = q[:, r * n:(r + 1) * n]
    base = nh * HEAD_DIM
    for p in range(nh // 2):
        pair = [_head_norm_rot(yt[base + i * HEAD_DIM:base + (i + 1) * HEAD_DIM, :], gk, cos, sin, False)
                for i in (2 * p, 2 * p + 1)]
        kt = jnp.concatenate(pair, axis=0).T.astype(BF16)
        for r in range(dil):
            k_ref[r, :, p * 2 * HEAD_DIM:(p + 1) * 2 * HEAD_DIM] = kt[r * n:(r + 1) * n, :]
    base = 2 * nh * HEAD_DIM
    v = yt[base:, :].astype(BF16)
    for r in range(dil):
        vt_ref[r] = v[:, r * n:(r + 1) * n]


def _proj_cls(x2d, g, wt, gains, cos_c, sin_c, *, batch, seq, dil, n):
    t_total = x2d.shape[0]
    tmx = dil * n
    length = seq // dil
    tps = seq // tmx
    kern = functools.partial(_proj_cls_kernel, dil=dil, n=n)
    feat = pl.BlockSpec((dil, A_GROUP_W, n), lambda i: (i // tps, 0, i % tps))
    return pl.pallas_call(
        kern,
        grid=(t_total // tmx,),
        in_specs=[
            pl.BlockSpec((tmx, D_MODEL), lambda i: (i, 0)),
            pl.BlockSpec((1, D_MODEL), lambda i: (0, 0)),
            pl.BlockSpec((3 * A_GROUP_W, D_MODEL), lambda i: (0, 0)),
            pl.BlockSpec((2 * HEAD_DIM, tmx), lambda i: (0, 0)),
            pl.BlockSpec((None, ROT_DIM // 2, tmx), lambda i: (i % tps, 0, 0)),
            pl.BlockSpec((None, ROT_DIM // 2, tmx), lambda i: (i % tps, 0, 0)),
        ],
        out_specs=[feat, pl.BlockSpec((dil, n, A_GROUP_W), lambda i: (i // tps, i % tps, 0)), feat],
        out_shape=[
            jax.ShapeDtypeStruct((batch * dil, A_GROUP_W, length), BF16),
            jax.ShapeDtypeStruct((batch * dil, length, A_GROUP_W), BF16),
            jax.ShapeDtypeStruct((batch * dil, A_GROUP_W, length), BF16),
        ],
        scratch_shapes=[pltpu.VMEM((tmx, D_MODEL), BF16),
                        pltpu.VMEM((D_MODEL // V7X_LANES, tmx if dil > 1 else 8, V7X_LANES), F32)],
        compiler_params=_params("parallel"),
        name="proj_cls",
    )(x2d, g, wt, gains, cos_c, sin_c)


FLASH_UNROLL = 4
FLASH_COL_TILE = 256
V_PAD = 16


def _flash_scratch(dv, ncols, seq, tkc):
    return [
        pltpu.VMEM((2 * HEAD_DIM, ncols), BF16),
        pltpu.VMEM((dv + V_PAD, seq), BF16),
        pltpu.VMEM((dv + V_PAD, ncols), F32),
        pltpu.VMEM((1, ncols), F32),
        pltpu.VMEM((tkc, ncols), F32), pltpu.VMEM((tkc, ncols), F32),
        pltpu.VMEM((1, ncols), F32), pltpu.VMEM((1, ncols), F32),
    ]


def _flash_unit(k_ref, vt_ref, scratch, *, dv, seq, tkc):
    qpad_ref, vext_ref, acc_ref, m_ref, s0, s1, c0, c1 = scratch
    s_bufs, cm_bufs = (s0, s1), (c0, c1)
    nc = seq // tkc

    vext_ref[0:dv, :] = vt_ref[...]
    ones_row = lax.broadcasted_iota(jnp.int32, (V_PAD, seq), 0) == 0
    vext_ref[dv:, :] = jnp.where(ones_row, 1.0, 0.0).astype(BF16)
    m_ref[...] = jnp.full(m_ref.shape, NEG_INF, F32)
    acc_ref[...] = jnp.zeros(acc_ref.shape, F32)

    def chunk(c):
        return pl.ds(c * tkc if isinstance(c, int) else pl.multiple_of(c * tkc, tkc), tkc)

    ncols = qpad_ref.shape[1]
    tiles = [slice(j * FLASH_COL_TILE, (j + 1) * FLASH_COL_TILE) for j in range(ncols // FLASH_COL_TILE)]

    def scores(c, slot, t):
        s = jnp.dot(k_ref[chunk(c), :], qpad_ref[:, t], preferred_element_type=F32)
        s_bufs[slot][:, t] = s
        cm_bufs[slot][:, t] = jnp.max(s, axis=0, keepdims=True)

    def softmax_pv(c, slot, t):
        m_old = m_ref[:, t]
        m_new = jnp.maximum(m_old, cm_bufs[slot][:, t])
        alpha = jnp.exp2(m_old - m_new)
        m_ref[:, t] = m_new
        p = jnp.exp2((s_bufs[slot][:, t] - m_new).astype(BF16))
        pv = jnp.dot(vext_ref[:, chunk(c)], p, preferred_element_type=F32)
        acc_ref[:, t] = alpha * acc_ref[:, t] + pv

    def step(c, slot, last=False):
        for t in tiles:
            if not last:
                scores(c + 1, 1 - slot, t)
            softmax_pv(c, slot, t)

    for t in tiles:
        scores(0, 0, t)
    unroll = FLASH_UNROLL if nc % FLASH_UNROLL == 0 else 2
    if nc > unroll:
        def group(t, carry):
            for u in range(unroll):
                step(unroll * t + u, u % 2)
            return carry

        lax.fori_loop(0, nc // unroll - 1, group, 0)
    tail = min(nc, unroll)
    for c in range(nc - tail, nc):
        step(c, c % 2, last=(c == nc - 1))


def _attn_c_kernel(qt_ref, k_ref, vt_ref, o_ref, *scratch, seq, tq, tkc):
    n = pl.program_id(1)
    qpad_ref, acc_ref = scratch[0], scratch[2]
    qcat = jnp.concatenate([qt_ref[g * HEAD_DIM:(g + 1) * HEAD_DIM, :] for g in range(C_GROUP)], axis=1)
    zeros = jnp.zeros_like(qcat)

    @pl.when(n % 2 == 0)
    def _():
        qpad_ref[0:HEAD_DIM, :] = qcat
        qpad_ref[HEAD_DIM:, :] = zeros

    @pl.when(n % 2 == 1)
    def _():
        qpad_ref[0:HEAD_DIM, :] = zeros
        qpad_ref[HEAD_DIM:, :] = qcat

    _flash_unit(k_ref, vt_ref, scratch, dv=HEAD_DIM, seq=seq, tkc=tkc)
    o = acc_ref[0:HEAD_DIM, :] * (1.0 / acc_ref[HEAD_DIM:HEAD_DIM + 1, :])
    for g in range(C_GROUP):
        o_ref[g * HEAD_DIM:(g + 1) * HEAD_DIM, :] = o[:, g * tq:(g + 1) * tq].astype(BF16)


def _attn_c(qt, k, vt, *, batch, seq, tq, tkc):
    t_total = batch * seq
    tq = min(tq, seq)
    tkc = min(tkc, seq)
    nqb = seq // tq
    ncols = C_GROUP * tq
    kern = functools.partial(_attn_c_kernel, seq=seq, tq=tq, tkc=tkc)
    return pl.pallas_call(
        kern,
        grid=(batch, C_KV_HEADS, nqb),
        in_specs=[
            pl.BlockSpec((C_GROUP * HEAD_DIM, tq), lambda b, n, i: (n, b * nqb + i)),
            pl.BlockSpec((seq, 2 * HEAD_DIM), lambda b, n, i: (b, n // 2)),
            pl.BlockSpec((HEAD_DIM, seq), lambda b, n, i: (n, b)),
        ],
        out_specs=pl.BlockSpec((C_GROUP * HEAD_DIM, tq), lambda b, n, i: (n, b * nqb + i)),
        out_shape=jax.ShapeDtypeStruct((C_Q_HEADS * HEAD_DIM, t_total), BF16),
        scratch_shapes=_flash_scratch(HEAD_DIM, ncols, seq, tkc),
        compiler_params=_params("parallel", "parallel", "parallel"),
        name="attn_c",
    )(qt, k, vt)


def _attn_b_kernel(lam_ref, qt_ref, k_ref, vt_ref, sg_ref, o_ref, *scratch, seq, tq, tkc):
    dv = 2 * HEAD_DIM
    qpad_ref, acc_ref = scratch[0], scratch[2]
    q = qt_ref[...]
    zeros = jnp.zeros((HEAD_DIM, tq), BF16)
    qpad_ref[0:HEAD_DIM, 0:tq] = q[0:HEAD_DIM]
    qpad_ref[0:HEAD_DIM, tq:] = zeros
    qpad_ref[HEAD_DIM:, 0:tq] = zeros
    qpad_ref[HEAD_DIM:, tq:] = q[HEAD_DIM:]
    _flash_unit(k_ref, vt_ref, scratch, dv=dv, seq=seq, tkc=tkc)
    on = acc_ref[0:dv, :] * (1.0 / acc_ref[dv:dv + 1, :])
    o = on[:, 0:tq] - lam_ref[0] * on[:, tq:]
    ms = jnp.mean(o * o, axis=0, keepdims=True)
    o_ref[...] = (o * lax.rsqrt(ms + SUBLN_EPS) * sg_ref[...]).astype(BF16)


def _attn_b(lam, qt, k, vt, sg, *, batch, seq, tq, tkc):
    t_total = batch * seq
    tq = min(tq, seq)
    tkc = min(tkc, seq)
    nqb = seq // tq
    kern = functools.partial(_attn_b_kernel, seq=seq, tq=tq, tkc=tkc)
    return pl.pallas_call(
        kern,
        grid=(batch, B_HEADS, nqb),
        in_specs=[
            pl.BlockSpec(memory_space=pltpu.SMEM),
            pl.BlockSpec((2 * HEAD_DIM, tq), lambda b, h, i: (h, b * nqb + i)),
            pl.BlockSpec((seq, 2 * HEAD_DIM), lambda b, h, i: (b, h)),
            pl.BlockSpec((2 * HEAD_DIM, seq), lambda b, h, i: (h, b)),
            pl.BlockSpec((2 * HEAD_DIM, tq), lambda b, h, i: (0, 0)),
        ],
        out_specs=pl.BlockSpec((2 * HEAD_DIM, tq), lambda b, h, i: (h, b * nqb + i)),
        out_shape=jax.ShapeDtypeStruct((B_V, t_total), BF16),
        scratch_shapes=_flash_scratch(2 * HEAD_DIM, 2 * tq, seq, tkc),
        compiler_params=_params("parallel", "parallel", "parallel"),
        name="attn_b",
    )(lam, qt, k, vt, sg[:, :tq])


def _band_kernel(qt_ref, kp_ref, km_ref, kn_ref, vp_ref, vm_ref, vn_ref, o_ref, lse_ref, *, length, tqs):
    l0 = pl.program_id(1) * tqs
    kwin = jnp.concatenate([kp_ref[...], km_ref[...], kn_ref[...]], axis=0)
    vwin = jnp.concatenate([vp_ref[...], vm_ref[...], vn_ref[...]], axis=1)
    row = lax.broadcasted_iota(jnp.int32, (BAND_WIN, BAND_SUB), 0)
    col = lax.broadcasted_iota(jnp.int32, (BAND_WIN, BAND_SUB), 1)
    rel = row - V7X_LANES - col
    in_band = jnp.abs(rel) <= BAND_HALF
    zeros = jnp.zeros((HEAD_DIM, BAND_SUB), BF16)
    npair = A_HEADS_PER_GROUP // 2
    units = [(j, p) for j in range(tqs // BAND_SUB) for p in range(npair)]
    scores = {}
    for j, p in units:
        c0 = j * BAND_SUB
        q = qt_ref[:, c0:c0 + BAND_SUB]
        qa = q[2 * p * HEAD_DIM:(2 * p + 1) * HEAD_DIM]
        qb = q[(2 * p + 1) * HEAD_DIM:(2 * p + 2) * HEAD_DIM]
        qpad = jnp.concatenate([jnp.concatenate([qa, zeros], axis=1),
                                jnp.concatenate([zeros, qb], axis=1)], axis=0)
        s = jnp.dot(kwin[c0:c0 + BAND_WIN, p * 2 * HEAD_DIM:(p + 1) * 2 * HEAD_DIM], qpad,
                    preferred_element_type=F32)
        kpos = l0 + (c0 - V7X_LANES) + row
        valid = in_band & (kpos >= 0) & (kpos < length)
        scores[j, p] = jnp.where(jnp.concatenate([valid, valid], axis=1), s, NEG_INF)
    probs = {}
    for u in units:
        m = jnp.max(scores[u], axis=0, keepdims=True)
        pr = jnp.exp2(scores[u] - m)
        probs[u] = (m, jnp.sum(pr, axis=0, keepdims=True), pr.astype(BF16))
    outs = {}
    for j, p in units:
        c0 = j * BAND_SUB
        m, l, pb = probs[j, p]
        va = vwin[2 * p * HEAD_DIM:(2 * p + 1) * HEAD_DIM, c0:c0 + BAND_WIN]
        vb = vwin[(2 * p + 1) * HEAD_DIM:(2 * p + 2) * HEAD_DIM, c0:c0 + BAND_WIN]
        oa = jnp.dot(va, pb[:, :BAND_SUB], preferred_element_type=F32)
        ob = jnp.dot(vb, pb[:, BAND_SUB:], preferred_element_type=F32)
        inv = 1.0 / l
        lse = m * LN2 + jnp.log(l)
        outs[j, p] = ([oa * inv[:, :BAND_SUB], ob * inv[:, BAND_SUB:]],
                      [jnp.broadcast_to(lse[:, :BAND_SUB], (HEAD_DIM, BAND_SUB)),
                       jnp.broadcast_to(lse[:, BAND_SUB:], (HEAD_DIM, BAND_SUB))])
    for j in range(tqs // BAND_SUB):
        c0 = j * BAND_SUB
        o_ref[c0:c0 + BAND_SUB, :] = jnp.concatenate([x for p in range(npair) for x in outs[j, p][0]], axis=0).T
        lse_ref[c0:c0 + BAND_SUB, :] = jnp.concatenate([x for p in range(npair) for x in outs[j, p][1]], axis=0).T


def _band(qt, k, vt, *, batch, seq, dil, tqs):
    length = seq // dil
    tqs = min(tqs, length)
    nqb = length // tqs
    ncls = batch * dil
    nkb = length // V7X_LANES
    r128 = tqs // V7X_LANES

    def prev_blk(i):
        return jnp.maximum(i * r128 - 1, 0)

    def next_blk(i):
        return jnp.minimum((i + 1) * r128, nkb - 1)

    kern = functools.partial(_band_kernel, length=length, tqs=tqs)
    out_spec = pl.BlockSpec((None, tqs, A_GROUP_W), lambda c, i: (c, i, 0))
    return pl.pallas_call(
        kern,
        grid=(ncls, nqb),
        in_specs=[
            pl.BlockSpec((None, A_GROUP_W, tqs), lambda c, i: (c, 0, i)),
            pl.BlockSpec((None, V7X_LANES, A_GROUP_W), lambda c, i: (c, prev_blk(i), 0)),
            pl.BlockSpec((None, tqs, A_GROUP_W), lambda c, i: (c, i, 0)),
            pl.BlockSpec((None, V7X_LANES, A_GROUP_W), lambda c, i: (c, next_blk(i), 0)),
            pl.BlockSpec((None, A_GROUP_W, V7X_LANES), lambda c, i: (c, 0, prev_blk(i))),
            pl.BlockSpec((None, A_GROUP_W, tqs), lambda c, i: (c, 0, i)),
            pl.BlockSpec((None, A_GROUP_W, V7X_LANES), lambda c, i: (c, 0, next_blk(i))),
        ],
        out_specs=[out_spec, out_spec],
        out_shape=[jax.ShapeDtypeStruct((ncls, length, A_GROUP_W), F32)] * 2,
        compiler_params=_params("parallel", "parallel"),
        name="band",
    )(qt, k, k, k, vt, vt, vt)


def _outproj_even_kernel(x_ref, o0_ref, o1_ref, o2_ref, l0_ref, l1_ref, l2_ref, obt_ref, wa_ref, wb_ref,
                         out_ref, *bufs, tm):
    def natural(blk_ref, buf_ref, dil):
        if dil == 1:
            return blk_ref[0]
        ntile = A_GROUP_W // V7X_LANES
        for r in range(dil):
            for c in range(ntile):
                buf_ref[c, pl.ds(r, tm // dil, stride=dil), :] = blk_ref[r, :, c * V7X_LANES:(c + 1) * V7X_LANES]
        return jnp.concatenate([buf_ref[c] for c in range(ntile)], axis=1)

    dils = [d for _, d in A_PATTERNS]
    o = [natural(ref, bufs[2 * gi], d) for gi, (ref, d) in enumerate(zip((o0_ref, o1_ref, o2_ref), dils))]
    l = [natural(ref, bufs[2 * gi + 1], d) for gi, (ref, d) in enumerate(zip((l0_ref, l1_ref, l2_ref), dils))]
    m = jnp.maximum(jnp.maximum(l[0], l[1]), l[2])
    w0, w1, w2 = jnp.exp(l[0] - m), jnp.exp(l[1] - m), jnp.exp(l[2] - m)
    oa = (w0 * o[0] + w1 * o[1] + w2 * o[2]) / (w0 + w1 + w2)
    acc = jnp.dot(oa.astype(BF16), wa_ref[...], preferred_element_type=F32)
    acc += lax.dot_general(obt_ref[...], wb_ref[...], (((0,), (0,)), ((), ())),
                           preferred_element_type=F32)
    out_ref[...] = x_ref[...] + acc


def _outproj_even(x2d, oa, lse, obt, wa, wb, *, seq, tm):
    t_total = x2d.shape[0]
    tps = seq // tm
    row = lambda i: (i, 0)
    const = lambda i: (0, 0)
    a_specs = [pl.BlockSpec((d, tm // d, A_GROUP_W), lambda i: (i // tps, i % tps, 0)) for _, d in A_PATTERNS]
    return pl.pallas_call(
        functools.partial(_outproj_even_kernel, tm=tm),
        grid=(t_total // tm,),
        in_specs=[pl.BlockSpec((tm, D_MODEL), row)] + a_specs * 2 + [
            pl.BlockSpec((B_V, tm), lambda i: (0, i)),
            pl.BlockSpec((A_GROUP_W, D_MODEL), const),
            pl.BlockSpec((B_V, D_MODEL), const),
        ],
        out_specs=pl.BlockSpec((tm, D_MODEL), row),
        out_shape=jax.ShapeDtypeStruct((t_total, D_MODEL), F32),
        scratch_shapes=[pltpu.VMEM((A_GROUP_W // V7X_LANES, tm, V7X_LANES), F32)] * (2 * A_GROUPS),
        compiler_params=_params("parallel"),
        name="outproj_even",
    )(x2d, *oa, *lse, obt, wa, wb)


def _outproj_odd_kernel(x_ref, ot_ref, w_ref, out_ref):
    acc = lax.dot_general(ot_ref[...], w_ref[...], (((0,), (0,)), ((), ())), preferred_element_type=F32)
    out_ref[...] = x_ref[...] + acc


def _outproj_odd(x2d, ot, w, *, tm):
    t_total = x2d.shape[0]
    n_in = ot.shape[0]
    return pl.pallas_call(
        _outproj_odd_kernel,
        grid=(t_total // tm,),
        in_specs=[
            pl.BlockSpec((tm, D_MODEL), lambda i: (i, 0)),
            pl.BlockSpec((n_in, tm), lambda i: (0, i)),
            pl.BlockSpec((n_in, D_MODEL), lambda i: (0, 0)),
        ],
        out_specs=pl.BlockSpec((tm, D_MODEL), lambda i: (i, 0)),
        out_shape=jax.ShapeDtypeStruct((t_total, D_MODEL), F32),
        compiler_params=_params("parallel"),
        name="outproj_odd",
    )(x2d, ot, w)


FF_HALO = 16


def _ffn_kernel(xp_ref, x_ref, xn_ref, g_ref, wup_ref, cw_ref, cb_ref, wd_ref, out_ref, hs_ref, acc_ref,
                ug0_ref, ug1_ref, uv0_ref, uv1_ref, act0_ref, act1_ref, *, seq, tm):
    i = pl.program_id(0)
    g = g_ref[...]
    has_prev = ((i * tm) % seq != 0).astype(F32)
    has_next = (((i + 1) * tm) % seq != 0).astype(F32)
    hs_ref[0:FF_HALO, :] = (_rms_rows(xp_ref[...], g) * has_prev).astype(BF16)
    hs_ref[FF_HALO:FF_HALO + tm, :] = _rms_rows(x_ref[...], g).astype(BF16)
    hs_ref[FF_HALO + tm:, :] = (_rms_rows(xn_ref[...], g) * has_next).astype(BF16)
    rows = tm + 2 * FF_HALO
    u_bufs = ((ug0_ref, uv0_ref), (ug1_ref, uv1_ref))
    nchunks = D_FF // FF_CHUNK

    def up_proj(c):
        ug_ref, uv_ref = u_bufs[c % 2]
        hs = hs_ref[...]
        ug_ref[...] = jnp.dot(hs, wup_ref[:, c * FF_CHUNK:(c + 1) * FF_CHUNK], preferred_element_type=F32)
        uv_ref[...] = jnp.dot(hs, wup_ref[:, D_FF + c * FF_CHUNK:D_FF + (c + 1) * FF_CHUNK],
                              preferred_element_type=F32)

    def conv(u_ref, c0):
        u = u_ref[...]
        w = cw_ref[:, c0:c0 + FF_CHUNK]
        b = cb_ref[:, c0:c0 + FF_CHUNK]
        um = pltpu.roll(u, 1, 0)[FF_HALO:FF_HALO + tm]
        up = pltpu.roll(u, rows - 1, 0)[FF_HALO:FF_HALO + tm]
        return um * w[0:1] + u[FF_HALO:FF_HALO + tm] * w[1:2] + up * w[2:3] + b

    act_bufs = (act0_ref, act1_ref)

    def down_proj(c):
        part = jnp.dot(act_bufs[c % 2][...], wd_ref[c * FF_CHUNK:(c + 1) * FF_CHUNK, :],
                       preferred_element_type=F32)
        if c == 0:
            acc_ref[...] = part
        else:
            acc_ref[...] += part

    up_proj(0)
    for c in range(nchunks):
        if c + 1 < nchunks:
            up_proj(c + 1)
        ug_ref, uv_ref = u_bufs[c % 2]
        gate = conv(ug_ref, c * FF_CHUNK)
        act_bufs[c % 2][...] = (gate * jax.nn.sigmoid(gate) * conv(uv_ref, D_FF + c * FF_CHUNK)).astype(BF16)
        if c >= 1:
            down_proj(c - 1)
    down_proj(nchunks - 1)
    out_ref[...] = x_ref[...] + acc_ref[...]


def _ffn(x2d, g, wup, cw, cb, wd, *, seq, tm):
    t_total = x2d.shape[0]
    tm = min(tm, seq)
    r = tm // FF_HALO
    last = t_total // FF_HALO - 1
    const = lambda i: (0, 0)
    resident = dict(pipeline_mode=pl.Buffered(1))
    kern = functools.partial(_ffn_kernel, seq=seq, tm=tm)
    return pl.pallas_call(
        kern,
        grid=(t_total // tm,),
        in_specs=[
            pl.BlockSpec((FF_HALO, D_MODEL), lambda i: (jnp.maximum(i * r - 1, 0), 0)),
            pl.BlockSpec((tm, D_MODEL), lambda i: (i, 0)),
            pl.BlockSpec((FF_HALO, D_MODEL), lambda i: (jnp.minimum((i + 1) * r, last), 0)),
            pl.BlockSpec((1, D_MODEL), const),
            pl.BlockSpec((D_MODEL, 2 * D_FF), const, **resident),
            pl.BlockSpec((3, 2 * D_FF), const),
            pl.BlockSpec((1, 2 * D_FF), const),
            pl.BlockSpec((D_FF, D_MODEL), const, **resident),
        ],
        out_specs=pl.BlockSpec((tm, D_MODEL), lambda i: (i, 0)),
        out_shape=jax.ShapeDtypeStruct((t_total, D_MODEL), F32),
        scratch_shapes=[
            pltpu.VMEM((tm + 2 * FF_HALO, D_MODEL), BF16),
            pltpu.VMEM((tm, D_MODEL), F32),
        ] + [pltpu.VMEM((tm + 2 * FF_HALO, FF_CHUNK), F32)] * 4 + [pltpu.VMEM((tm, FF_CHUNK), BF16)] * 2,
        compiler_params=_params("parallel"),
        name="ffn",
    )(x2d, x2d, x2d, g, wup, cw, cb, wd)


def _rope_tables(pos, dim, theta):
    inv = theta ** (-jnp.arange(0, dim, 2, dtype=F32) / dim)
    ang = pos.astype(F32)[:, None] * inv[None, :]
    return jnp.cos(ang).T, jnp.sin(ang).T


def _class_tiles(table, dil, n):
    rows, seq = table.shape
    return table.reshape(rows, seq // (dil * n), n, dil).transpose(1, 0, 3, 2).reshape(-1, rows, dil * n)


def _lane_bcast(v, width):
    return jnp.broadcast_to(v.astype(F32)[:, None], (v.shape[0], width))


TM = 512
TQ_C = 512
TQ_B = 1024
TKC = 512
TQ_BAND = 512
A_CLASS_TOKENS = (512, 512, 128)
Q_SCALE = HEAD_DIM ** -0.5 * LOG2E


def _trunk(x, norm_mix, norm_ffn, w_in_ab, q_norm_a, k_norm_a, q_norm_b, k_norm_b,
           lambda_q1, lambda_k1, lambda_q2, lambda_k2, subln_b, w_out_ab,
           w_in_c, q_norm_c, k_norm_c, w_out_c, w_up, conv_w, conv_b, w_down):
    batch, seq, _ = x.shape
    t_total = batch * seq
    depth = norm_mix.shape[0]
    x2d = x.reshape(t_total, D_MODEL)

    cos, sin = _rope_tables(jnp.arange(seq), ROT_DIM, ROPE_THETA)
    rows = seq // GRID_W
    row = jnp.repeat(jnp.arange(rows), GRID_W)
    col = jnp.tile(jnp.arange(GRID_W), rows)
    cr, sr = _rope_tables(row, AXIAL_DIM, AXIAL_THETA)
    cc, sc = _rope_tables(col, AXIAL_DIM, AXIAL_THETA)
    cos_ax = jnp.concatenate([cr, cc], axis=0)
    sin_ax = jnp.concatenate([sr, sc], axis=0)

    for i in range(depth):
        j = i // 2
        g_mix = norm_mix[i].reshape(1, D_MODEL)
        if i % 2 == 0:
            lam_init = 0.8 - 0.6 * math.exp(-0.3 * i)
            w = w_in_ab[j]
            oa, lse = [], []
            for gi, (_, dil) in enumerate(A_PATTERNS):
                n_cls = min(A_CLASS_TOKENS[gi], seq // dil)
                width = dil * n_cls
                gains_a = jnp.concatenate([_lane_bcast(q_norm_a[j] * Q_SCALE, width),
                                           _lane_bcast(k_norm_a[j], width)], 0)
                sl = slice(gi * A_GROUP_W, (gi + 1) * A_GROUP_W)
                wt = jnp.concatenate([w[:, sl], w[:, A_QKV:2 * A_QKV][:, sl], w[:, 2 * A_QKV:3 * A_QKV][:, sl]],
                                     axis=1).T.astype(BF16)
                qt, k, vt = _proj_cls(x2d, g_mix, wt, gains_a, _class_tiles(cos, dil, n_cls),
                                      _class_tiles(sin, dil, n_cls), batch=batch, seq=seq, dil=dil, n=n_cls)
                o_g, lse_g = _band(qt, k, vt, batch=batch, seq=seq, dil=dil, tqs=TQ_BAND)
                oa.append(o_g)
                lse.append(lse_g)
            wt_b = w[:, 3 * A_QKV:].T.astype(BF16)
            gains_b = jnp.concatenate([_lane_bcast(q_norm_b[j] * Q_SCALE, TM), _lane_bcast(k_norm_b[j], TM)], 0)
            qt, k, vt = _proj(x2d, g_mix, wt_b, gains_b, cos, sin, seq=seq,
                              nq=2 * B_HEADS, nk=2 * B_HEADS, nv=B_V, axial=False, tm=TM)
            lam = (jnp.exp(jnp.sum(lambda_q1[j].astype(F32) * lambda_k1[j].astype(F32)))
                   - jnp.exp(jnp.sum(lambda_q2[j].astype(F32) * lambda_k2[j].astype(F32))) + lam_init)
            sg = _lane_bcast(subln_b[j] * (1.0 - lam_init), TQ_B)
            obt = _attn_b(lam.reshape(1).astype(F32), qt, k, vt, sg, batch=batch, seq=seq, tq=TQ_B, tkc=TKC)
            wo = w_out_ab[j].astype(BF16)
            x2d = _outproj_even(x2d, oa, lse, obt, wo[:A_GROUP_W], wo[A_GROUP_W:], seq=seq, tm=TM)
        else:
            wt = w_in_c[j].T.astype(BF16)
            gains_c = jnp.concatenate([_lane_bcast(q_norm_c[j] * Q_SCALE, TM), _lane_bcast(k_norm_c[j], TM)], 0)
            qt, k, vt = _proj(x2d, g_mix, wt, gains_c, cos_ax, sin_ax, seq=seq,
                              nq=C_Q_HEADS, nk=C_KV_HEADS, nv=C_KV_HEADS * HEAD_DIM, axial=True, tm=TM)
            ot = _attn_c(qt, k, vt, batch=batch, seq=seq, tq=TQ_C, tkc=TKC)
            x2d = _outproj_odd(x2d, ot, w_out_c[j].astype(BF16), tm=TM)
        x2d = _ffn(x2d, norm_ffn[i].reshape(1, D_MODEL), w_up[i].astype(BF16), conv_w[i],
                   conv_b[i].reshape(1, 2 * D_FF), w_down[i].astype(BF16), seq=seq, tm=TM)
    return x2d.reshape(batch, seq, D_MODEL)


def kernel(x_prompt, x_sample, norm_mix, norm_ffn, w_in_ab, q_norm_a, k_norm_a, q_norm_b, k_norm_b,
           lambda_q1, lambda_k1, lambda_q2, lambda_k2, subln_b, w_out_ab, w_in_c, q_norm_c, k_norm_c,
           w_out_c, w_up, conv_w, conv_b, w_down):
    params = (norm_mix, norm_ffn, w_in_ab, q_norm_a, k_norm_a, q_norm_b, k_norm_b,
              lambda_q1, lambda_k1, lambda_q2, lambda_k2, subln_b, w_out_ab,
              w_in_c, q_norm_c, k_norm_c, w_out_c, w_up, conv_w, conv_b, w_down)
    return (_trunk(x_prompt, *params), _trunk(x_sample, *params))
```

```python
import functools
import math

import jax
import jax.numpy as jnp
from jax import lax
from jax.experimental import pallas as pl
from jax.experimental.pallas import tpu as pltpu

F32 = jnp.float32
BF16 = jnp.bfloat16

D_MODEL = 1024
HEAD_DIM = 64
A_PATTERNS = ((128, 1), (512, 4), (2048, 16))
A_GROUPS = 3
A_HEADS_PER_GROUP = 4
A_QKV = A_GROUPS * A_HEADS_PER_GROUP * HEAD_DIM
A_GROUP_W = A_HEADS_PER_GROUP * HEAD_DIM
B_HEADS = 4
B_QK = B_HEADS * 2 * HEAD_DIM
B_V = B_HEADS * 2 * HEAD_DIM
C_Q_HEADS = 16
C_KV_HEADS = 4
C_GROUP = C_Q_HEADS // C_KV_HEADS
ROPE_THETA = 500000.0
ROT_DIM = HEAD_DIM // 4
AXIAL_THETA = 10000.0
AXIAL_DIM = HEAD_DIM // 2
GRID_W = 64
D_FF = 2816
NORM_EPS = 1e-6
SUBLN_EPS = 1e-5
NEG_INF = -1e30
LOG2E = 1.4426950408889634
LN2 = 0.6931471805599453

V7X_LANES = 128
V7X_VMEM_BYTES = 64 * 1024 * 1024
VMEM_LIMIT = V7X_VMEM_BYTES - 8 * 1024 * 1024
BAND_HALF = 64
BAND_SUB = 128
BAND_WIN = BAND_SUB + 2 * V7X_LANES
FF_CHUNK = 256


def _params(*semantics):
    return pltpu.CompilerParams(dimension_semantics=semantics, vmem_limit_bytes=VMEM_LIMIT)


def _rms_rows(x, g):
    ms = jnp.mean(x * x, axis=-1, keepdims=True)
    return x * lax.rsqrt(ms + NORM_EPS) * g


def _head_norm_rot(y, gain, cos, sin, axial):
    ssq = jnp.sum(y * y, axis=0, keepdims=True)
    yn = y * lax.rsqrt(ssq * (1.0 / HEAD_DIM) + NORM_EPS) * gain
    if axial:
        h = AXIAL_DIM // 2
        a1, a2, b1, b2 = yn[0:h], yn[h:2 * h], yn[2 * h:3 * h], yn[3 * h:4 * h]
        cr, cc = cos[0:h], cos[h:2 * h]
        sr, sc = sin[0:h], sin[h:2 * h]
        return jnp.concatenate(
            [a1 * cr - a2 * sr, a2 * cr + a1 * sr, b1 * cc - b2 * sc, b2 * cc + b1 * sc], axis=0)
    h = ROT_DIM // 2
    x1, x2 = yn[0:h], yn[h:2 * h]
    return jnp.concatenate([x1 * cos - x2 * sin, x2 * cos + x1 * sin, yn[2 * h:]], axis=0)


def _proj_kernel(x_ref, g_ref, wt_ref, gain_ref, cos_ref, sin_ref, qt_ref, k_ref, vt_ref,
                 *, nq, nk, axial):
    h = _rms_rows(x_ref[...], g_ref[...]).astype(BF16)
    yt = lax.dot_general(wt_ref[...], h, (((1,), (1,)), ((), ())),
                         preferred_element_type=F32)
    cos = cos_ref[...]
    sin = sin_ref[...]
    gq = gain_ref[0:HEAD_DIM, :]
    gk = gain_ref[HEAD_DIM:2 * HEAD_DIM, :]
    for i in range(nq):
        y = yt[i * HEAD_DIM:(i + 1) * HEAD_DIM, :]
        qt_ref[i * HEAD_DIM:(i + 1) * HEAD_DIM, :] = _head_norm_rot(y, gq, cos, sin, axial).astype(BF16)
    base = nq * HEAD_DIM
    for p in range(nk // 2):
        pair = []
        for i in (2 * p, 2 * p + 1):
            y = yt[base + i * HEAD_DIM:base + (i + 1) * HEAD_DIM, :]
            pair.append(_head_norm_rot(y, gk, cos, sin, axial))
        kt = jnp.concatenate(pair, axis=0)
        k_ref[:, p * 2 * HEAD_DIM:(p + 1) * 2 * HEAD_DIM] = kt.T.astype(BF16)
    base = (nq + nk) * HEAD_DIM
    vt_ref[...] = yt[base:, :].astype(BF16)


def _proj(x2d, g, wt, gains, cos_t, sin_t, *, seq, nq, nk, nv, axial, tm):
    t_total = x2d.shape[0]
    tm = min(tm, seq)
    nlb = seq // tm
    n_out = (nq + nk) * HEAD_DIM + nv
    rot_rows = cos_t.shape[0]
    kern = functools.partial(_proj_kernel, nq=nq, nk=nk, axial=axial)
    return pl.pallas_call(
        kern,
        grid=(t_total // tm,),
        in_specs=[
            pl.BlockSpec((tm, D_MODEL), lambda i: (i, 0)),
            pl.BlockSpec((1, D_MODEL), lambda i: (0, 0)),
            pl.BlockSpec((n_out, D_MODEL), lambda i: (0, 0)),
            pl.BlockSpec((2 * HEAD_DIM, tm), lambda i: (0, 0)),
            pl.BlockSpec((rot_rows, tm), lambda i: (0, i % nlb)),
            pl.BlockSpec((rot_rows, tm), lambda i: (0, i % nlb)),
        ],
        out_specs=[
            pl.BlockSpec((nq * HEAD_DIM, tm), lambda i: (0, i)),
            pl.BlockSpec((tm, nk * HEAD_DIM), lambda i: (i, 0)),
            pl.BlockSpec((nv, tm), lambda i: (0, i)),
        ],
        out_shape=[
            jax.ShapeDtypeStruct((nq * HEAD_DIM, t_total), BF16),
            jax.ShapeDtypeStruct((t_total, nk * HEAD_DIM), BF16),
            jax.ShapeDtypeStruct((nv, t_total), BF16),
        ],
        compiler_params=_params("parallel"),
        name="proj",
    )(x2d, g, wt, gains[:, :tm], cos_t, sin_t)


def _proj_cls_kernel(x_ref, g_ref, wt_ref, gain_ref, cos_ref, sin_ref, qt_ref, k_ref, vt_ref, hs_ref,
                     stage_ref, *, dil, n):
    h = _rms_rows(x_ref[...], g_ref[...])
    if dil == 1:
        hs_ref[...] = h.astype(BF16)
    else:
        for c in range(D_MODEL // V7X_LANES):
            stage_ref[c] = h[:, c * V7X_LANES:(c + 1) * V7X_LANES]
        for r in range(dil):
            for c in range(D_MODEL // V7X_LANES):
                hs_ref[r * n:(r + 1) * n, c * V7X_LANES:(c + 1) * V7X_LANES] = (
                    stage_ref[c, pl.ds(r, n, stride=dil), :].astype(BF16))
    yt = lax.dot_general(wt_ref[...], hs_ref[...], (((1,), (1,)), ((), ())),
                         preferred_element_type=F32)
    cos = cos_ref[...]
    sin = sin_ref[...]
    gq = gain_ref[0:HEAD_DIM, :]
    gk = gain_ref[HEAD_DIM:2 * HEAD_DIM, :]
    nh = A_HEADS_PER_GROUP
    for i in range(nh):
        q = _head_norm_rot(yt[i * HEAD_DIM:(i + 1) * HEAD_DIM, :], gq, cos, sin, False).astype(BF16)
        for r in range(dil):
            qt_ref[r, i * HEAD_DIM:(i + 1) * HEAD_DIM, :] = q[:, r * n:(r + 1) * n]
    base = nh * HEAD_DIM
    for p in range(nh // 2):
        pair = [_head_norm_rot(yt[base + i * HEAD_DIM:base + (i + 1) * HEAD_DIM, :], gk, cos, sin, False)
                for i in (2 * p, 2 * p + 1)]
        kt = jnp.concatenate(pair, axis=0).T.astype(BF16)
        for r in range(dil):
            k_ref[r, :, p * 2 * HEAD_DIM:(p + 1) * 2 * HEAD_DIM] = kt[r * n:(r + 1) * n, :]
    base = 2 * nh * HEAD_DIM
    v = yt[base:, :].astype(BF16)
    for r in range(dil):
        vt_ref[r] = v[:, r * n:(r + 1) * n]


def _proj_cls(x2d, g, wt, gains, cos_c, sin_c, *, batch, seq, dil, n):
    t_total = x2d.shape[0]
    tmx = dil * n
    length = seq // dil
    tps = seq // tmx
    kern = functools.partial(_proj_cls_kernel, dil=dil, n=n)
    feat = pl.BlockSpec((dil, A_GROUP_W, n), lambda i: (i // tps, 0, i % tps))
    return pl.pallas_call(
        kern,
        grid=(t_total // tmx,),
        in_specs=[
            pl.BlockSpec((tmx, D_MODEL), lambda i: (i, 0)),
            pl.BlockSpec((1, D_MODEL), lambda i: (0, 0)),
            pl.BlockSpec((3 * A_GROUP_W, D_MODEL), lambda i: (0, 0)),
            pl.BlockSpec((2 * HEAD_DIM, tmx), lambda i: (0, 0)),
            pl.BlockSpec((None, ROT_DIM // 2, tmx), lambda i: (i % tps, 0, 0)),
            pl.BlockSpec((None, ROT_DIM // 2, tmx), lambda i: (i % tps, 0, 0)),
        ],
        out_specs=[feat, pl.BlockSpec((dil, n, A_GROUP_W), lambda i: (i // tps, i % tps, 0)), feat],
        out_shape=[
            jax.ShapeDtypeStruct((batch * dil, A_GROUP_W, length), BF16),
            jax.ShapeDtypeStruct((batch * dil, length, A_GROUP_W), BF16),
            jax.ShapeDtypeStruct((batch * dil, A_GROUP_W, length), BF16),
        ],
        scratch_shapes=[pltpu.VMEM((tmx, D_MODEL), BF16),
                        pltpu.VMEM((D_MODEL // V7X_LANES, tmx if dil > 1 else 8, V7X_LANES), F32)],
        compiler_params=_params("parallel"),
        name="proj_cls",
    )(x2d, g, wt, gains, cos_c, sin_c)


FLASH_UNROLL = 4
FLASH_COL_TILE = 256
V_PAD = 16


def _flash_scratch(dv, ncols, seq, tkc):
    return [
        pltpu.VMEM((2, 2 * HEAD_DIM, ncols), BF16),
        pltpu.VMEM((dv + V_PAD, seq), BF16),
        pltpu.VMEM((dv + V_PAD, ncols), F32),
        pltpu.VMEM((1, ncols), F32),
        pltpu.VMEM((tkc, ncols), F32), pltpu.VMEM((tkc, ncols), F32),
        pltpu.VMEM((1, ncols), F32), pltpu.VMEM((1, ncols), F32),
    ]


def _flash_chain(k_ref, vt_ref, scratch, build_qpad, epilogue, *, dv, seq, tq, tkc):
    qpad_ref, vext_ref, acc_ref, m_ref, s0, s1, c0, c1 = scratch
    s_bufs, cm_bufs = (s0, s1), (c0, c1)
    nc = seq // tkc
    nqb = seq // tq
    unroll = FLASH_UNROLL if nc % FLASH_UNROLL == 0 else nc
    ncols = qpad_ref.shape[2]
    tiles = [slice(j * FLASH_COL_TILE, (j + 1) * FLASH_COL_TILE) for j in range(ncols // FLASH_COL_TILE)]

    vext_ref[0:dv, :] = vt_ref[...]
    ones_row = lax.broadcasted_iota(jnp.int32, (V_PAD, seq), 0) == 0
    vext_ref[dv:, :] = jnp.where(ones_row, 1.0, 0.0).astype(BF16)

    def chunk(c):
        return pl.ds(c * tkc if isinstance(c, int) else pl.multiple_of(c * tkc, tkc), tkc)

    def scores(qslot, c, slot, t):
        s = jnp.dot(k_ref[chunk(c), :], qpad_ref[qslot, :, t], preferred_element_type=F32)
        s_bufs[slot][:, t] = s
        cm_bufs[slot][:, t] = jnp.max(s, axis=0, keepdims=True)

    def softmax_pv(c, slot, t, first=False):
        cm = cm_bufs[slot][:, t]
        if first:
            m_new = cm
        else:
            m_old = m_ref[:, t]
            m_new = jnp.maximum(m_old, cm)
        m_ref[:, t] = m_new
        p = jnp.exp2((s_bufs[slot][:, t] - m_new).astype(BF16))
        pv = jnp.dot(vext_ref[:, chunk(c)], p, preferred_element_type=F32)
        acc_ref[:, t] = pv if first else jnp.exp2(m_old - m_new) * acc_ref[:, t] + pv

    def step(c, slot, nxt=None, first=False):
        for t in tiles:
            if nxt is not None:
                scores(nxt[0], nxt[1], 1 - slot, t)
            softmax_pv(c, slot, t, first)

    if nc == unroll and nqb % 2 == 0:
        build_qpad(0, 0)
        for t in tiles:
            scores(0, 0, 0, t)

        def pair(i, carry):
            for par in (0, 1):
                qb = 2 * i + par
                build_qpad(jnp.minimum(qb + 1, nqb - 1), 1 - par)
                for c in range(nc):
                    step(c, c % 2, nxt=(par, c + 1) if c + 1 < nc else (1 - par, 0), first=(c == 0))
                epilogue(qb)
            return carry

        lax.fori_loop(0, nqb // 2, pair, 0)
    else:
        def block(qb, carry):
            build_qpad(qb, 0)
            for t in tiles:
                scores(0, 0, 0, t)
            if nc > unroll:
                step(0, 0, nxt=(0, 1), first=True)
                for c in range(1, unroll):
                    step(c, c % 2, nxt=(0, c + 1))

                def group(g, inner):
                    for u in range(unroll):
                        step(unroll * g + u, u % 2, nxt=(0, unroll * g + u + 1))
                    return inner

                lax.fori_loop(1, nc // unroll - 1, group, 0)
            tail = unroll if nc > unroll else nc
            for c in range(nc - tail, nc):
                step(c, c % 2, nxt=(0, c + 1) if c + 1 < nc else None, first=(c == 0))
            epilogue(qb)
            return carry

        lax.fori_loop(0, nqb, block, 0)


def _attn_c_kernel(qt_ref, k_ref, vt_ref, o_ref, *scratch, seq, tq, tkc):
    qpad_ref, acc_ref = scratch[0], scratch[2]
    upper = (pl.program_id(1) % 2).astype(F32)

    def col(qb):
        return pl.ds(pl.multiple_of(qb * tq, tq), tq)

    def build_qpad(qb, slot):
        qcat = jnp.concatenate([qt_ref[g * HEAD_DIM:(g + 1) * HEAD_DIM, col(qb)] for g in range(C_GROUP)],
                               axis=1).astype(F32)
        qpad_ref[slot, 0:HEAD_DIM, :] = (qcat * (1.0 - upper)).astype(BF16)
        qpad_ref[slot, HEAD_DIM:, :] = (qcat * upper).astype(BF16)

    def epilogue(qb):
        o = acc_ref[0:HEAD_DIM, :] * (1.0 / acc_ref[HEAD_DIM:HEAD_DIM + 1, :])
        for g in range(C_GROUP):
            o_ref[g * HEAD_DIM:(g + 1) * HEAD_DIM, col(qb)] = o[:, g * tq:(g + 1) * tq].astype(BF16)

    _flash_chain(k_ref, vt_ref, scratch, build_qpad, epilogue, dv=HEAD_DIM, seq=seq, tq=tq, tkc=tkc)


def _attn_c(qt, k, vt, *, batch, seq, tq, tkc):
    t_total = batch * seq
    tq = min(tq, seq)
    tkc = min(tkc, seq)
    kern = functools.partial(_attn_c_kernel, seq=seq, tq=tq, tkc=tkc)
    return pl.pallas_call(
        kern,
        grid=(batch, C_KV_HEADS),
        in_specs=[
            pl.BlockSpec((C_GROUP * HEAD_DIM, seq), lambda b, n: (n, b)),
            pl.BlockSpec((seq, 2 * HEAD_DIM), lambda b, n: (b, n // 2)),
            pl.BlockSpec((HEAD_DIM, seq), lambda b, n: (n, b)),
        ],
        out_specs=pl.BlockSpec((C_GROUP * HEAD_DIM, seq), lambda b, n: (n, b)),
        out_shape=jax.ShapeDtypeStruct((C_Q_HEADS * HEAD_DIM, t_total), BF16),
        scratch_shapes=_flash_scratch(HEAD_DIM, C_GROUP * tq, seq, tkc),
        compiler_params=_params("parallel", "parallel"),
        name="attn_c",
    )(qt, k, vt)


def _attn_b_kernel(lam_ref, qt_ref, k_ref, vt_ref, sg_ref, o_ref, *scratch, seq, tq, tkc):
    dv = 2 * HEAD_DIM
    qpad_ref, acc_ref = scratch[0], scratch[2]
    zeros = jnp.zeros((HEAD_DIM, tq), BF16)

    def col(qb):
        return pl.ds(pl.multiple_of(qb * tq, tq), tq)

    def build_qpad(qb, slot):
        q = qt_ref[:, col(qb)]
        qpad_ref[slot, 0:HEAD_DIM, 0:tq] = q[0:HEAD_DIM]
        qpad_ref[slot, 0:HEAD_DIM, tq:] = zeros
        qpad_ref[slot, HEAD_DIM:, 0:tq] = zeros
        qpad_ref[slot, HEAD_DIM:, tq:] = q[HEAD_DIM:]

    def epilogue(qb):
        on = acc_ref[0:dv, :] * (1.0 / acc_ref[dv:dv + 1, :])
        o = on[:, 0:tq] - lam_ref[0] * on[:, tq:]
        ms = jnp.mean(o * o, axis=0, keepdims=True)
        o_ref[:, col(qb)] = (o * lax.rsqrt(ms + SUBLN_EPS) * sg_ref[...]).astype(BF16)

    _flash_chain(k_ref, vt_ref, scratch, build_qpad, epilogue, dv=dv, seq=seq, tq=tq, tkc=tkc)


def _attn_b(lam, qt, k, vt, sg, *, batch, seq, tq, tkc):
    t_total = batch * seq
    tq = min(tq, seq)
    tkc = min(tkc, seq)
    kern = functools.partial(_attn_b_kernel, seq=seq, tq=tq, tkc=tkc)
    return pl.pallas_call(
        kern,
        grid=(batch, B_HEADS),
        in_specs=[
            pl.BlockSpec(memory_space=pltpu.SMEM),
            pl.BlockSpec((2 * HEAD_DIM, seq), lambda b, h: (h, b)),
            pl.BlockSpec((seq, 2 * HEAD_DIM), lambda b, h: (b, h)),
            pl.BlockSpec((2 * HEAD_DIM, seq), lambda b, h: (h, b)),
            pl.BlockSpec((2 * HEAD_DIM, tq), lambda b, h: (0, 0)),
        ],
        out_specs=pl.BlockSpec((2 * HEAD_DIM, seq), lambda b, h: (h, b)),
        out_shape=jax.ShapeDtypeStruct((B_V, t_total), BF16),
        scratch_shapes=_flash_scratch(2 * HEAD_DIM, 2 * tq, seq, tkc),
        compiler_params=_params("parallel", "parallel"),
        name="attn_b",
    )(lam, qt, k, vt, sg[:, :tq])


def _band_kernel(qt_ref, kp_ref, km_ref, kn_ref, vp_ref, vm_ref, vn_ref, o_ref, lse_ref, *, length, tqs):
    l0 = pl.program_id(1) * tqs
    kwin = jnp.concatenate([kp_ref[...], km_ref[...], kn_ref[...]], axis=0)
    vwin = jnp.concatenate([vp_ref[...], vm_ref[...], vn_ref[...]], axis=1)
    row = lax.broadcasted_iota(jnp.int32, (BAND_WIN, BAND_SUB), 0)
    col = lax.broadcasted_iota(jnp.int32, (BAND_WIN, BAND_SUB), 1)
    rel = row - V7X_LANES - col
    in_band = jnp.abs(rel) <= BAND_HALF
    zeros = jnp.zeros((HEAD_DIM, BAND_SUB), BF16)
    npair = A_HEADS_PER_GROUP // 2
    units = [(j, p) for j in range(tqs // BAND_SUB) for p in range(npair)]
    scores = {}
    for j, p in units:
        c0 = j * BAND_SUB
        q = qt_ref[:, c0:c0 + BAND_SUB]
        qa = q[2 * p * HEAD_DIM:(2 * p + 1) * HEAD_DIM]
        qb = q[(2 * p + 1) * HEAD_DIM:(2 * p + 2) * HEAD_DIM]
        qpad = jnp.concatenate([jnp.concatenate([qa, zeros], axis=1),
                                jnp.concatenate([zeros, qb], axis=1)], axis=0)
        s = jnp.dot(kwin[c0:c0 + BAND_WIN, p * 2 * HEAD_DIM:(p + 1) * 2 * HEAD_DIM], qpad,
                    preferred_element_type=F32)
        kpos = l0 + (c0 - V7X_LANES) + row
        valid = in_band & (kpos >= 0) & (kpos < length)
        scores[j, p] = jnp.where(jnp.concatenate([valid, valid], axis=1), s, NEG_INF)
    probs = {}
    for u in units:
        m = jnp.max(scores[u], axis=0, keepdims=True)
        pr = jnp.exp2(scores[u] - m)
        probs[u] = (m, jnp.sum(pr, axis=0, keepdims=True), pr.astype(BF16))
    outs = {}
    for j, p in units:
        c0 = j * BAND_SUB
        m, l, pb = probs[j, p]
        va = vwin[2 * p * HEAD_DIM:(2 * p + 1) * HEAD_DIM, c0:c0 + BAND_WIN]
        vb = vwin[(2 * p + 1) * HEAD_DIM:(2 * p + 2) * HEAD_DIM, c0:c0 + BAND_WIN]
        oa = jnp.dot(va, pb[:, :BAND_SUB], preferred_element_type=F32)
        ob = jnp.dot(vb, pb[:, BAND_SUB:], preferred_element_type=F32)
        inv = 1.0 / l
        lse = m * LN2 + jnp.log(l)
        outs[j, p] = ([oa * inv[:, :BAND_SUB], ob * inv[:, BAND_SUB:]],
                      [jnp.broadcast_to(lse[:, :BAND_SUB], (HEAD_DIM, BAND_SUB)),
                       jnp.broadcast_to(lse[:, BAND_SUB:], (HEAD_DIM, BAND_SUB))])
    for j in range(tqs // BAND_SUB):
        c0 = j * BAND_SUB
        o_ref[c0:c0 + BAND_SUB, :] = jnp.concatenate([x for p in range(npair) for x in outs[j, p][0]], axis=0).T
        lse_ref[c0:c0 + BAND_SUB, :] = jnp.concatenate([x for p in range(npair) for x in outs[j, p][1]], axis=0).T


def _band(qt, k, vt, *, batch, seq, dil, tqs):
    length = seq // dil
    tqs = min(tqs, length)
    nqb = length // tqs
    ncls = batch * dil
    nkb = length // V7X_LANES
    r128 = tqs // V7X_LANES

    def prev_blk(i):
        return jnp.maximum(i * r128 - 1, 0)

    def next_blk(i):
        return jnp.minimum((i + 1) * r128, nkb - 1)

    kern = functools.partial(_band_kernel, length=length, tqs=tqs)
    out_spec = pl.BlockSpec((None, tqs, A_GROUP_W), lambda c, i: (c, i, 0))
    return pl.pallas_call(
        kern,
        grid=(ncls, nqb),
        in_specs=[
            pl.BlockSpec((None, A_GROUP_W, tqs), lambda c, i: (c, 0, i)),
            pl.BlockSpec((None, V7X_LANES, A_GROUP_W), lambda c, i: (c, prev_blk(i), 0)),
            pl.BlockSpec((None, tqs, A_GROUP_W), lambda c, i: (c, i, 0)),
            pl.BlockSpec((None, V7X_LANES, A_GROUP_W), lambda c, i: (c, next_blk(i), 0)),
            pl.BlockSpec((None, A_GROUP_W, V7X_LANES), lambda c, i: (c, 0, prev_blk(i))),
            pl.BlockSpec((None, A_GROUP_W, tqs), lambda c, i: (c, 0, i)),
            pl.BlockSpec((None, A_GROUP_W, V7X_LANES), lambda c, i: (c, 0, next_blk(i))),
        ],
        out_specs=[out_spec, out_spec],
        out_shape=[jax.ShapeDtypeStruct((ncls, length, A_GROUP_W), F32)] * 2,
        compiler_params=_params("parallel", "parallel"),
        name="band",
    )(qt, k, k, k, vt, vt, vt)


def _outproj_even_kernel(x_ref, o0_ref, o1_ref, o2_ref, l0_ref, l1_ref, l2_ref, obt_ref, wa_ref, wb_ref,
                         out_ref, *bufs, tm):
    def natural(blk_ref, buf_ref, dil):
        if dil == 1:
            return blk_ref[0]
        ntile = A_GROUP_W // V7X_LANES
        for r in range(dil):
            for c in range(ntile):
                buf_ref[c, pl.ds(r, tm // dil, stride=dil), :] = blk_ref[r, :, c * V7X_LANES:(c + 1) * V7X_LANES]
        return jnp.concatenate([buf_ref[c] for c in range(ntile)], axis=1)

    dils = [d for _, d in A_PATTERNS]
    o = [natural(ref, bufs[2 * gi], d) for gi, (ref, d) in enumerate(zip((o0_ref, o1_ref, o2_ref), dils))]
    l = [natural(ref, bufs[2 * gi + 1], d) for gi, (ref, d) in enumerate(zip((l0_ref, l1_ref, l2_ref), dils))]
    m = jnp.maximum(jnp.maximum(l[0], l[1]), l[2])
    w0, w1, w2 = jnp.exp(l[0] - m), jnp.exp(l[1] - m), jnp.exp(l[2] - m)
    oa = (w0 * o[0] + w1 * o[1] + w2 * o[2]) / (w0 + w1 + w2)
    acc = jnp.dot(oa.astype(BF16), wa_ref[...], preferred_element_type=F32)
    acc += lax.dot_general(obt_ref[...], wb_ref[...], (((0,), (0,)), ((), ())),
                           preferred_element_type=F32)
    out_ref[...] = x_ref[...] + acc


def _outproj_even(x2d, oa, lse, obt, wa, wb, *, seq, tm):
    t_total = x2d.shape[0]
    tps = seq // tm
    row = lambda i: (i, 0)
    const = lambda i: (0, 0)
    a_specs = [pl.BlockSpec((d, tm // d, A_GROUP_W), lambda i: (i // tps, i % tps, 0)) for _, d in A_PATTERNS]
    return pl.pallas_call(
        functools.partial(_outproj_even_kernel, tm=tm),
        grid=(t_total // tm,),
        in_specs=[pl.BlockSpec((tm, D_MODEL), row)] + a_specs * 2 + [
            pl.BlockSpec((B_V, tm), lambda i: (0, i)),
            pl.BlockSpec((A_GROUP_W, D_MODEL), const),
            pl.BlockSpec((B_V, D_MODEL), const),
        ],
        out_specs=pl.BlockSpec((tm, D_MODEL), row),
        out_shape=jax.ShapeDtypeStruct((t_total, D_MODEL), F32),
        scratch_shapes=[pltpu.VMEM((A_GROUP_W // V7X_LANES, tm, V7X_LANES), F32)] * (2 * A_GROUPS),
        compiler_params=_params("parallel"),
        name="outproj_even",
    )(x2d, *oa, *lse, obt, wa, wb)


def _outproj_odd_kernel(x_ref, ot_ref, w_ref, out_ref):
    acc = lax.dot_general(ot_ref[...], w_ref[...], (((0,), (0,)), ((), ())), preferred_element_type=F32)
    out_ref[...] = x_ref[...] + acc


def _outproj_odd(x2d, ot, w, *, tm):
    t_total = x2d.shape[0]
    n_in = ot.shape[0]
    return pl.pallas_call(
        _outproj_odd_kernel,
        grid=(t_total // tm,),
        in_specs=[
            pl.BlockSpec((tm, D_MODEL), lambda i: (i, 0)),
            pl.BlockSpec((n_in, tm), lambda i: (0, i)),
            pl.BlockSpec((n_in, D_MODEL), lambda i: (0, 0)),
        ],
        out_specs=pl.BlockSpec((tm, D_MODEL), lambda i: (i, 0)),
        out_shape=jax.ShapeDtypeStruct((t_total, D_MODEL), F32),
        compiler_params=_params("parallel"),
        name="outproj_odd",
    )(x2d, ot, w)


FF_HALO = 16


def _ffn_kernel(xp_ref, x_ref, xn_ref, g_ref, wup_ref, cw_ref, cb_ref, wd_ref, out_ref, hs_ref, acc_ref,
                ug0_ref, ug1_ref, uv0_ref, uv1_ref, act0_ref, act1_ref, *, seq, tm):
    i = pl.program_id(0)
    g = g_ref[...]
    has_prev = ((i * tm) % seq != 0).astype(F32)
    has_next = (((i + 1) * tm) % seq != 0).astype(F32)
    hs_ref[0:FF_HALO, :] = (_rms_rows(xp_ref[...], g) * has_prev).astype(BF16)
    hs_ref[FF_HALO:FF_HALO + tm, :] = _rms_rows(x_ref[...], g).astype(BF16)
    hs_ref[FF_HALO + tm:, :] = (_rms_rows(xn_ref[...], g) * has_next).astype(BF16)
    rows = tm + 2 * FF_HALO
    u_bufs = ((ug0_ref, uv0_ref), (ug1_ref, uv1_ref))
    nchunks = D_FF // FF_CHUNK

    def up_proj(c):
        ug_ref, uv_ref = u_bufs[c % 2]
        hs = hs_ref[...]
        ug_ref[...] = jnp.dot(hs, wup_ref[:, c * FF_CHUNK:(c + 1) * FF_CHUNK], preferred_element_type=F32)
        uv_ref[...] = jnp.dot(hs, wup_ref[:, D_FF + c * FF_CHUNK:D_FF + (c + 1) * FF_CHUNK],
                              preferred_element_type=F32)

    def conv(u_ref, c0):
        u = u_ref[...]
        w = cw_ref[:, c0:c0 + FF_CHUNK]
        b = cb_ref[:, c0:c0 + FF_CHUNK]
        um = pltpu.roll(u, 1, 0)[FF_HALO:FF_HALO + tm]
        up = pltpu.roll(u, rows - 1, 0)[FF_HALO:FF_HALO + tm]
        return um * w[0:1] + u[FF_HALO:FF_HALO + tm] * w[1:2] + up * w[2:3] + b

    act_bufs = (act0_ref, act1_ref)

    def down_proj(c):
        part = jnp.dot(act_bufs[c % 2][...], wd_ref[c * FF_CHUNK:(c + 1) * FF_CHUNK, :],
                       preferred_element_type=F32)
        if c == 0:
            acc_ref[...] = part
        else:
            acc_ref[...] += part

    up_proj(0)
    for c in range(nchunks):
        if c + 1 < nchunks:
            up_proj(c + 1)
        ug_ref, uv_ref = u_bufs[c % 2]
        gate = conv(ug_ref, c * FF_CHUNK)
        act_bufs[c % 2][...] = (gate * jax.nn.sigmoid(gate) * conv(uv_ref, D_FF + c * FF_CHUNK)).astype(BF16)
        if c >= 1:
            down_proj(c - 1)
    down_proj(nchunks - 1)
    out_ref[...] = x_ref[...] + acc_ref[...]


def _ffn(x2d, g, wup, cw, cb, wd, *, seq, tm):
    t_total = x2d.shape[0]
    tm = min(tm, seq)
    r = tm // FF_HALO
    last = t_total // FF_HALO - 1
    const = lambda i: (0, 0)
    resident = dict(pipeline_mode=pl.Buffered(1))
    kern = functools.partial(_ffn_kernel, seq=seq, tm=tm)
    return pl.pallas_call(
        kern,
        grid=(t_total // tm,),
        in_specs=[
            pl.BlockSpec((FF_HALO, D_MODEL), lambda i: (jnp.maximum(i * r - 1, 0), 0)),
            pl.BlockSpec((tm, D_MODEL), lambda i: (i, 0)),
            pl.BlockSpec((FF_HALO, D_MODEL), lambda i: (jnp.minimum((i + 1) * r, last), 0)),
            pl.BlockSpec((1, D_MODEL), const),
            pl.BlockSpec((D_MODEL, 2 * D_FF), const, **resident),
            pl.BlockSpec((3, 2 * D_FF), const),
            pl.BlockSpec((1, 2 * D_FF), const),
            pl.BlockSpec((D_FF, D_MODEL), const, **resident),
        ],
        out_specs=pl.BlockSpec((tm, D_MODEL), lambda i: (i, 0)),
        out_shape=jax.ShapeDtypeStruct((t_total, D_MODEL), F32),
        scratch_shapes=[
            pltpu.VMEM((tm + 2 * FF_HALO, D_MODEL), BF16),
            pltpu.VMEM((tm, D_MODEL), F32),
        ] + [pltpu.VMEM((tm + 2 * FF_HALO, FF_CHUNK), F32)] * 4 + [pltpu.VMEM((tm, FF_CHUNK), BF16)] * 2,
        compiler_params=_params("parallel"),
        name="ffn",
    )(x2d, x2d, x2d, g, wup, cw, cb, wd)


def _rope_tables(pos, dim, theta):
    inv = theta ** (-jnp.arange(0, dim, 2, dtype=F32) / dim)
    ang = pos.astype(F32)[:, None] * inv[None, :]
    return jnp.cos(ang).T, jnp.sin(ang).T


def _class_tiles(table, dil, n):
    rows, seq = table.shape
    return table.reshape(rows, seq // (dil * n), n, dil).transpose(1, 0, 3, 2).reshape(-1, rows, dil * n)


def _lane_bcast(v, width):
    return jnp.broadcast_to(v.astype(F32)[:, None], (v.shape[0], width))


TM = 512
TQ_C = 512
TQ_B = 1024
TKC = 512
TQ_BAND = 512
A_CLASS_TOKENS = (512, 512, 128)
Q_SCALE = HEAD_DIM ** -0.5 * LOG2E


def _trunk(x, norm_mix, norm_ffn, w_in_ab, q_norm_a, k_norm_a, q_norm_b, k_norm_b,
           lambda_q1, lambda_k1, lambda_q2, lambda_k2, subln_b, w_out_ab,
           w_in_c, q_norm_c, k_norm_c, w_out_c, w_up, conv_w, conv_b, w_down):
    batch, seq, _ = x.shape
    t_total = batch * seq
    depth = norm_mix.shape[0]
    x2d = x.reshape(t_total, D_MODEL)

    cos, sin = _rope_tables(jnp.arange(seq), ROT_DIM, ROPE_THETA)
    rows = seq // GRID_W
    row = jnp.repeat(jnp.arange(rows), GRID_W)
    col = jnp.tile(jnp.arange(GRID_W), rows)
    cr, sr = _rope_tables(row, AXIAL_DIM, AXIAL_THETA)
    cc, sc = _rope_tables(col, AXIAL_DIM, AXIAL_THETA)
    cos_ax = jnp.concatenate([cr, cc], axis=0)
    sin_ax = jnp.concatenate([sr, sc], axis=0)

    for i in range(depth):
        j = i // 2
        g_mix = norm_mix[i].reshape(1, D_MODEL)
        if i % 2 == 0:
            lam_init = 0.8 - 0.6 * math.exp(-0.3 * i)
            w = w_in_ab[j]
            oa, lse = [], []
            for gi, (_, dil) in enumerate(A_PATTERNS):
                n_cls = min(A_CLASS_TOKENS[gi], seq // dil)
                width = dil * n_cls
                gains_a = jnp.concatenate([_lane_bcast(q_norm_a[j] * Q_SCALE, width),
                                           _lane_bcast(k_norm_a[j], width)], 0)
                sl = slice(gi * A_GROUP_W, (gi + 1) * A_GROUP_W)
                wt = jnp.concatenate([w[:, sl], w[:, A_QKV:2 * A_QKV][:, sl], w[:, 2 * A_QKV:3 * A_QKV][:, sl]],
                                     axis=1).T.astype(BF16)
                qt, k, vt = _proj_cls(x2d, g_mix, wt, gains_a, _class_tiles(cos, dil, n_cls),
                                      _class_tiles(sin, dil, n_cls), batch=batch, seq=seq, dil=dil, n=n_cls)
                o_g, lse_g = _band(qt, k, vt, batch=batch, seq=seq, dil=dil, tqs=TQ_BAND)
                oa.append(o_g)
                lse.append(lse_g)
            wt_b = w[:, 3 * A_QKV:].T.astype(BF16)
            gains_b = jnp.concatenate([_lane_bcast(q_norm_b[j] * Q_SCALE, TM), _lane_bcast(k_norm_b[j], TM)], 0)
            qt, k, vt = _proj(x2d, g_mix, wt_b, gains_b, cos, sin, seq=seq,
                              nq=2 * B_HEADS, nk=2 * B_HEADS, nv=B_V, axial=False, tm=TM)
            lam = (jnp.exp(jnp.sum(lambda_q1[j].astype(F32) * lambda_k1[j].astype(F32)))
                   - jnp.exp(jnp.sum(lambda_q2[j].astype(F32) * lambda_k2[j].astype(F32))) + lam_init)
            sg = _lane_bcast(subln_b[j] * (1.0 - lam_init), TQ_B)
            obt = _attn_b(lam.reshape(1).astype(F32), qt, k, vt, sg, batch=batch, seq=seq, tq=TQ_B, tkc=TKC)
            wo = w_out_ab[j].astype(BF16)
            x2d = _outproj_even(x2d, oa, lse, obt, wo[:A_GROUP_W], wo[A_GROUP_W:], seq=seq, tm=TM)
        else:
            wt = w_in_c[j].T.astype(BF16)
            gains_c = jnp.concatenate([_lane_bcast(q_norm_c[j] * Q_SCALE, TM), _lane_bcast(k_norm_c[j], TM)], 0)
            qt, k, vt = _proj(x2d, g_mix, wt, gains_c, cos_ax, sin_ax, seq=seq,
                              nq=C_Q_HEADS, nk=C_KV_HEADS, nv=C_KV_HEADS * HEAD_DIM, axial=True, tm=TM)
            ot = _attn_c(qt, k, vt, batch=batch, seq=seq, tq=TQ_C, tkc=TKC)
            x2d = _outproj_odd(x2d, ot, w_out_c[j].astype(BF16), tm=TM)
        x2d = _ffn(x2d, norm_ffn[i].reshape(1, D_MODEL), w_up[i].astype(BF16), conv_w[i],
                   conv_b[i].reshape(1, 2 * D_FF), w_down[i].astype(BF16), seq=seq, tm=TM)
    return x2d.reshape(batch, seq, D_MODEL)


def kernel(x_prompt, x_sample, norm_mix, norm_ffn, w_in_ab, q_norm_a, k_norm_a, q_norm_b, k_norm_b,
           lambda_q1, lambda_k1, lambda_q2, lambda_k2, subln_b, w_out_ab, w_in_c, q_norm_c, k_norm_c,
           w_out_c, w_up, conv_w, conv_b, w_down):
    params = (norm_mix, norm_ffn, w_in_ab, q_norm_a, k_norm_a, q_norm_b, k_norm_b,
              lambda_q1, lambda_k1, lambda_q2, lambda_k2, subln_b, w_out_ab,
              w_in_c, q_norm_c, k_norm_c, w_out_c, w_up, conv_w, conv_b, w_down)
    return (_trunk(x_prompt, *params), _trunk(x_sample, *params))
```

```python
import functools
import math

import jax
import jax.numpy as jnp
from jax import lax
from jax.experimental import pallas as pl
from jax.experimental.pallas import tpu as pltpu

F32 = jnp.float32
BF16 = jnp.bfloat16

D_MODEL = 1024
HEAD_DIM = 64
A_PATTERNS = ((128, 1), (512, 4), (2048, 16))
A_GROUPS = 3
A_HEADS_PER_GROUP = 4
A_QKV = A_GROUPS * A_HEADS_PER_GROUP * HEAD_DIM
A_GROUP_W = A_HEADS_PER_GROUP * HEAD_DIM
B_HEADS = 4
B_QK = B_HEADS * 2 * HEAD_DIM
B_V = B_HEADS * 2 * HEAD_DIM
C_Q_HEADS = 16
C_KV_HEADS = 4
C_GROUP = C_Q_HEADS // C_KV_HEADS
ROPE_THETA = 500000.0
ROT_DIM = HEAD_DIM // 4
AXIAL_THETA = 10000.0
AXIAL_DIM = HEAD_DIM // 2
GRID_W = 64
D_FF = 2816
NORM_EPS = 1e-6
SUBLN_EPS = 1e-5
NEG_INF = -1e30
LOG2E = 1.4426950408889634
LN2 = 0.6931471805599453

V7X_LANES = 128
V7X_VMEM_BYTES = 64 * 1024 * 1024
VMEM_LIMIT = V7X_VMEM_BYTES - 8 * 1024 * 1024
BAND_HALF = 64
BAND_SUB = 128
BAND_WIN = BAND_SUB + 2 * V7X_LANES
FF_CHUNK = 256


def _params(*semantics):
    return pltpu.CompilerParams(dimension_semantics=semantics, vmem_limit_bytes=VMEM_LIMIT)


def _rms_rows(x, g):
    ms = jnp.mean(x * x, axis=-1, keepdims=True)
    return x * lax.rsqrt(ms + NORM_EPS) * g


def _head_norm_rot(y, gain, cos, sin, axial):
    ssq = jnp.sum(y * y, axis=0, keepdims=True)
    yn = y * lax.rsqrt(ssq * (1.0 / HEAD_DIM) + NORM_EPS) * gain
    if axial:
        h = AXIAL_DIM // 2
        a1, a2, b1, b2 = yn[0:h], yn[h:2 * h], yn[2 * h:3 * h], yn[3 * h:4 * h]
        cr, cc = cos[0:h], cos[h:2 * h]
        sr, sc = sin[0:h], sin[h:2 * h]
        return jnp.concatenate(
            [a1 * cr - a2 * sr, a2 * cr + a1 * sr, b1 * cc - b2 * sc, b2 * cc + b1 * sc], axis=0)
    h = ROT_DIM // 2
    x1, x2 = yn[0:h], yn[h:2 * h]
    return jnp.concatenate([x1 * cos - x2 * sin, x2 * cos + x1 * sin, yn[2 * h:]], axis=0)


def _proj_kernel(x_ref, g_ref, wt_ref, gain_ref, cos_ref, sin_ref, qt_ref, k_ref, vt_ref,
                 *, nq, nk, axial):
    h = _rms_rows(x_ref[...], g_ref[...]).astype(BF16)
    yt = lax.dot_general(wt_ref[...], h, (((1,), (1,)), ((), ())),
                         preferred_element_type=F32)
    cos = cos_ref[...]
    sin = sin_ref[...]
    gq = gain_ref[0:HEAD_DIM, :]
    gk = gain_ref[HEAD_DIM:2 * HEAD_DIM, :]
    for i in range(nq):
        y = yt[i * HEAD_DIM:(i + 1) * HEAD_DIM, :]
        qt_ref[i * HEAD_DIM:(i + 1) * HEAD_DIM, :] = _head_norm_rot(y, gq, cos, sin, axial).astype(BF16)
    base = nq * HEAD_DIM
    for p in range(nk // 2):
        pair = []
        for i in (2 * p, 2 * p + 1):
            y = yt[base + i * HEAD_DIM:base + (i + 1) * HEAD_DIM, :]
            pair.append(_head_norm_rot(y, gk, cos, sin, axial))
        kt = jnp.concatenate(pair, axis=0)
        k_ref[:, p * 2 * HEAD_DIM:(p + 1) * 2 * HEAD_DIM] = kt.T.astype(BF16)
    base = (nq + nk) * HEAD_DIM
    vt_ref[...] = yt[base:, :].astype(BF16)


def _proj(x2d, g, wt, gains, cos_t, sin_t, *, seq, nq, nk, nv, axial, tm):
    t_total = x2d.shape[0]
    tm = min(tm, seq)
    nlb = seq // tm
    n_out = (nq + nk) * HEAD_DIM + nv
    rot_rows = cos_t.shape[0]
    kern = functools.partial(_proj_kernel, nq=nq, nk=nk, axial=axial)
    return pl.pallas_call(
        kern,
        grid=(t_total // tm,),
        in_specs=[
            pl.BlockSpec((tm, D_MODEL), lambda i: (i, 0)),
            pl.BlockSpec((1, D_MODEL), lambda i: (0, 0)),
            pl.BlockSpec((n_out, D_MODEL), lambda i: (0, 0)),
            pl.BlockSpec((2 * HEAD_DIM, tm), lambda i: (0, 0)),
            pl.BlockSpec((rot_rows, tm), lambda i: (0, i % nlb)),
            pl.BlockSpec((rot_rows, tm), lambda i: (0, i % nlb)),
        ],
        out_specs=[
            pl.BlockSpec((nq * HEAD_DIM, tm), lambda i: (0, i)),
            pl.BlockSpec((tm, nk * HEAD_DIM), lambda i: (i, 0)),
            pl.BlockSpec((nv, tm), lambda i: (0, i)),
        ],
        out_shape=[
            jax.ShapeDtypeStruct((nq * HEAD_DIM, t_total), BF16),
            jax.ShapeDtypeStruct((t_total, nk * HEAD_DIM), BF16),
            jax.ShapeDtypeStruct((nv, t_total), BF16),
        ],
        compiler_params=_params("parallel"),
        name="proj",
    )(x2d, g, wt, gains[:, :tm], cos_t, sin_t)


def _proj_cls_kernel(x_ref, g_ref, wt_ref, gain_ref, cos_ref, sin_ref, qt_ref, k_ref, vt_ref, hs_ref,
                     stage_ref, *, dil, n):
    h = _rms_rows(x_ref[...], g_ref[...])
    if dil == 1:
        hs_ref[...] = h.astype(BF16)
    else:
        for c in range(D_MODEL // V7X_LANES):
            stage_ref[c] = h[:, c * V7X_LANES:(c + 1) * V7X_LANES]
        for r in range(dil):
            for c in range(D_MODEL // V7X_LANES):
                hs_ref[r * n:(r + 1) * n, c * V7X_LANES:(c + 1) * V7X_LANES] = (
                    stage_ref[c, pl.ds(r, n, stride=dil), :].astype(BF16))
    yt = lax.dot_general(wt_ref[...], hs_ref[...], (((1,), (1,)), ((), ())),
                         preferred_element_type=F32)
    cos = cos_ref[...]
    sin = sin_ref[...]
    gq = gain_ref[0:HEAD_DIM, :]
    gk = gain_ref[HEAD_DIM:2 * HEAD_DIM, :]
    nh = A_HEADS_PER_GROUP
    for i in range(nh):
        q = _head_norm_rot(yt[i * HEAD_DIM:(i + 1) * HEAD_DIM, :], gq, cos, sin, False).astype(BF16)
        for r in range(dil):
            qt_ref[r, i * HEAD_DIM:(i + 1) * HEAD_DIM, :] = q[:, r * n:(r + 1) * n]
    base = nh * HEAD_DIM
    for p in range(nh // 2):
        pair = [_head_norm_rot(yt[base + i * HEAD_DIM:base + (i + 1) * HEAD_DIM, :], gk, cos, sin, False)
                for i in (2 * p, 2 * p + 1)]
        kt = jnp.concatenate(pair, axis=0).T.astype(BF16)
        for r in range(dil):
            k_ref[r, :, p * 2 * HEAD_DIM:(p + 1) * 2 * HEAD_DIM] = kt[r * n:(r + 1) * n, :]
    base = 2 * nh * HEAD_DIM
    v = yt[base:, :].astype(BF16)
    for r in range(dil):
        vt_ref[r] = v[:, r * n:(r + 1) * n]


def _proj_cls(x2d, g, wt, gains, cos_c, sin_c, *, batch, seq, dil, n):
    t_total = x2d.shape[0]
    tmx = dil * n
    length = seq // dil
    tps = seq // tmx
    kern = functools.partial(_proj_cls_kernel, dil=dil, n=n)
    feat = pl.BlockSpec((dil, A_GROUP_W, n), lambda i: (i // tps, 0, i % tps))
    return pl.pallas_call(
        kern,
        grid=(t_total // tmx,),
        in_specs=[
            pl.BlockSpec((tmx, D_MODEL), lambda i: (i, 0)),
            pl.BlockSpec((1, D_MODEL), lambda i: (0, 0)),
            pl.BlockSpec((3 * A_GROUP_W, D_MODEL), lambda i: (0, 0)),
            pl.BlockSpec((2 * HEAD_DIM, tmx), lambda i: (0, 0)),
            pl.BlockSpec((None, ROT_DIM // 2, tmx), lambda i: (i % tps, 0, 0)),
            pl.BlockSpec((None, ROT_DIM // 2, tmx), lambda i: (i % tps, 0, 0)),
        ],
        out_specs=[feat, pl.BlockSpec((dil, n, A_GROUP_W), lambda i: (i // tps, i % tps, 0)), feat],
        out_shape=[
            jax.ShapeDtypeStruct((batch * dil, A_GROUP_W, length), BF16),
            jax.ShapeDtypeStruct((batch * dil, length, A_GROUP_W), BF16),
            jax.ShapeDtypeStruct((batch * dil, A_GROUP_W, length), BF16),
        ],
        scratch_shapes=[pltpu.VMEM((tmx, D_MODEL), BF16),
                        pltpu.VMEM((D_MODEL // V7X_LANES, tmx if dil > 1 else 8, V7X_LANES), F32)],
        compiler_params=_params("parallel"),
        name="proj_cls",
    )(x2d, g, wt, gains, cos_c, sin_c)


FLASH_UNROLL = 4
FLASH_COL_TILE = 256
V_PAD = 16


def _flash_scratch(dv, ncols, seq, tkc):
    return [
        pltpu.VMEM((2, 2 * HEAD_DIM, ncols), BF16),
        pltpu.VMEM((dv + V_PAD, seq), BF16),
        pltpu.VMEM((dv + V_PAD, ncols), F32),
        pltpu.VMEM((1, ncols), F32),
        pltpu.VMEM((tkc, ncols), F32), pltpu.VMEM((tkc, ncols), F32),
        pltpu.VMEM((1, ncols), F32), pltpu.VMEM((1, ncols), F32),
    ]


def _flash_chain(k_ref, vt_ref, scratch, build_qpad, epilogue, *, dv, seq, tq, tkc):
    qpad_ref, vext_ref, acc_ref, m_ref, s0, s1, c0, c1 = scratch
    s_bufs, cm_bufs = (s0, s1), (c0, c1)
    nc = seq // tkc
    nqb = seq // tq
    unroll = FLASH_UNROLL if nc % FLASH_UNROLL == 0 else nc
    ncols = qpad_ref.shape[2]
    tiles = [slice(j * FLASH_COL_TILE, (j + 1) * FLASH_COL_TILE) for j in range(ncols // FLASH_COL_TILE)]

    vext_ref[0:dv, :] = vt_ref[...]
    ones_row = lax.broadcasted_iota(jnp.int32, (V_PAD, seq), 0) == 0
    vext_ref[dv:, :] = jnp.where(ones_row, 1.0, 0.0).astype(BF16)

    def chunk(c):
        return pl.ds(c * tkc if isinstance(c, int) else pl.multiple_of(c * tkc, tkc), tkc)

    def scores(qslot, c, slot, t):
        s = jnp.dot(k_ref[chunk(c), :], qpad_ref[qslot, :, t], preferred_element_type=F32)
        s_bufs[slot][:, t] = s
        cm_bufs[slot][:, t] = jnp.max(s, axis=0, keepdims=True)

    def softmax_pv(c, slot, t, first=False):
        cm = cm_bufs[slot][:, t]
        if first:
            m_new = cm
        else:
            m_old = m_ref[:, t]
            m_new = jnp.maximum(m_old, cm)
        m_ref[:, t] = m_new
        p = jnp.exp2((s_bufs[slot][:, t] - m_new).astype(BF16))
        pv = jnp.dot(vext_ref[:, chunk(c)], p, preferred_element_type=F32)
        acc_ref[:, t] = pv if first else jnp.exp2(m_old - m_new) * acc_ref[:, t] + pv

    def step(c, slot, nxt=None, first=False):
        for t in tiles:
            if nxt is not None:
                scores(nxt[0], nxt[1], 1 - slot, t)
            softmax_pv(c, slot, t, first)

    if nc == unroll and nqb % 2 == 0:
        build_qpad(0, 0)
        for t in tiles:
            scores(0, 0, 0, t)

        def pair(i, carry):
            for par in (0, 1):
                qb = 2 * i + par
                build_qpad(jnp.minimum(qb + 1, nqb - 1), 1 - par)
                for c in range(nc):
                    step(c, c % 2, nxt=(par, c + 1) if c + 1 < nc else (1 - par, 0), first=(c == 0))
                epilogue(qb)
            return carry

        lax.fori_loop(0, nqb // 2, pair, 0)
    else:
        def block(qb, carry):
            build_qpad(qb, 0)
            for t in tiles:
                scores(0, 0, 0, t)
            if nc > unroll:
                step(0, 0, nxt=(0, 1), first=True)
                for c in range(1, unroll):
                    step(c, c % 2, nxt=(0, c + 1))

                def group(g, inner):
                    for u in range(unroll):
                        step(unroll * g + u, u % 2, nxt=(0, unroll * g + u + 1))
                    return inner

                lax.fori_loop(1, nc // unroll - 1, group, 0)
            tail = unroll if nc > unroll else nc
            for c in range(nc - tail, nc):
                step(c, c % 2, nxt=(0, c + 1) if c + 1 < nc else None, first=(c == 0))
            epilogue(qb)
            return carry

        lax.fori_loop(0, nqb, block, 0)


def _attn_c_kernel(qt_ref, k_ref, vt_ref, o_ref, *scratch, seq, tq, tkc):
    qpad_ref, acc_ref = scratch[0], scratch[2]
    upper = (pl.program_id(1) % 2).astype(F32)

    def col(qb):
        return pl.ds(pl.multiple_of(qb * tq, tq), tq)

    def build_qpad(qb, slot):
        qcat = jnp.concatenate([qt_ref[g * HEAD_DIM:(g + 1) * HEAD_DIM, col(qb)] for g in range(C_GROUP)],
                               axis=1).astype(F32)
        qpad_ref[slot, 0:HEAD_DIM, :] = (qcat * (1.0 - upper)).astype(BF16)
        qpad_ref[slot, HEAD_DIM:, :] = (qcat * upper).astype(BF16)

    def epilogue(qb):
        o = acc_ref[0:HEAD_DIM, :] * (1.0 / acc_ref[HEAD_DIM:HEAD_DIM + 1, :])
        for g in range(C_GROUP):
            o_ref[g * HEAD_DIM:(g + 1) * HEAD_DIM, col(qb)] = o[:, g * tq:(g + 1) * tq].astype(BF16)

    _flash_chain(k_ref, vt_ref, scratch, build_qpad, epilogue, dv=HEAD_DIM, seq=seq, tq=tq, tkc=tkc)


def _attn_c(qt, k, vt, *, batch, seq, tq, tkc):
    t_total = batch * seq
    tq = min(tq, seq)
    tkc = min(tkc, seq)
    kern = functools.partial(_attn_c_kernel, seq=seq, tq=tq, tkc=tkc)
    return pl.pallas_call(
        kern,
        grid=(batch, C_KV_HEADS),
        in_specs=[
            pl.BlockSpec((C_GROUP * HEAD_DIM, seq), lambda b, n: (n, b)),
            pl.BlockSpec((seq, 2 * HEAD_DIM), lambda b, n: (b, n // 2)),
            pl.BlockSpec((HEAD_DIM, seq), lambda b, n: (n, b)),
        ],
        out_specs=pl.BlockSpec((C_GROUP * HEAD_DIM, seq), lambda b, n: (n, b)),
        out_shape=jax.ShapeDtypeStruct((C_Q_HEADS * HEAD_DIM, t_total), BF16),
        scratch_shapes=_flash_scratch(HEAD_DIM, C_GROUP * tq, seq, tkc),
        compiler_params=_params("parallel", "parallel"),
        name="attn_c",
    )(qt, k, vt)


def _attn_b_kernel(lam_ref, qt_ref, k_ref, vt_ref, sg_ref, o_ref, *scratch, seq, tq, tkc):
    dv = 2 * HEAD_DIM
    qpad_ref, acc_ref = scratch[0], scratch[2]
    zeros = jnp.zeros((HEAD_DIM, tq), BF16)

    def col(qb):
        return pl.ds(pl.multiple_of(qb * tq, tq), tq)

    def build_qpad(qb, slot):
        q = qt_ref[:, col(qb)]
        qpad_ref[slot, 0:HEAD_DIM, 0:tq] = q[0:HEAD_DIM]
        qpad_ref[slot, 0:HEAD_DIM, tq:] = zeros
        qpad_ref[slot, HEAD_DIM:, 0:tq] = zeros
        qpad_ref[slot, HEAD_DIM:, tq:] = q[HEAD_DIM:]

    def epilogue(qb):
        on = acc_ref[0:dv, :] * (1.0 / acc_ref[dv:dv + 1, :])
        o = on[:, 0:tq] - lam_ref[0] * on[:, tq:]
        ms = jnp.mean(o * o, axis=0, keepdims=True)
        o_ref[:, col(qb)] = (o * lax.rsqrt(ms + SUBLN_EPS) * sg_ref[...]).astype(BF16)

    _flash_chain(k_ref, vt_ref, scratch, build_qpad, epilogue, dv=dv, seq=seq, tq=tq, tkc=tkc)


def _attn_b(lam, qt, k, vt, sg, *, batch, seq, tq, tkc):
    t_total = batch * seq
    tq = min(tq, seq)
    tkc = min(tkc, seq)
    kern = functools.partial(_attn_b_kernel, seq=seq, tq=tq, tkc=tkc)
    return pl.pallas_call(
        kern,
        grid=(batch, B_HEADS),
        in_specs=[
            pl.BlockSpec(memory_space=pltpu.SMEM),
            pl.BlockSpec((2 * HEAD_DIM, seq), lambda b, h: (h, b)),
            pl.BlockSpec((seq, 2 * HEAD_DIM), lambda b, h: (b, h)),
            pl.BlockSpec((2 * HEAD_DIM, seq), lambda b, h: (h, b)),
            pl.BlockSpec((2 * HEAD_DIM, tq), lambda b, h: (0, 0)),
        ],
        out_specs=pl.BlockSpec((2 * HEAD_DIM, seq), lambda b, h: (h, b)),
        out_shape=jax.ShapeDtypeStruct((B_V, t_total), BF16),
        scratch_shapes=_flash_scratch(2 * HEAD_DIM, 2 * tq, seq, tkc),
        compiler_params=_params("parallel", "parallel"),
        name="attn_b",
    )(lam, qt, k, vt, sg[:, :tq])


def _band_kernel(qt_ref, kp_ref, km_ref, kn_ref, vp_ref, vm_ref, vn_ref, o_ref, lse_ref, *, length, tqs):
    l0 = pl.program_id(1) * tqs
    kwin = jnp.concatenate([kp_ref[...], km_ref[...], kn_ref[...]], axis=0)
    vwin = jnp.concatenate([vp_ref[...], vm_ref[...], vn_ref[...]], axis=1)
    row = lax.broadcasted_iota(jnp.int32, (BAND_WIN, BAND_SUB), 0)
    col = lax.broadcasted_iota(jnp.int32, (BAND_WIN, BAND_SUB), 1)
    rel = row - V7X_LANES - col
    in_band = jnp.abs(rel) <= BAND_HALF
    zeros = jnp.zeros((HEAD_DIM, BAND_SUB), BF16)
    npair = A_HEADS_PER_GROUP // 2
    units = [(j, p) for j in range(tqs // BAND_SUB) for p in range(npair)]
    scores = {}
    for j, p in units:
        c0 = j * BAND_SUB
        q = qt_ref[:, c0:c0 + BAND_SUB]
        qa = q[2 * p * HEAD_DIM:(2 * p + 1) * HEAD_DIM]
        qb = q[(2 * p + 1) * HEAD_DIM:(2 * p + 2) * HEAD_DIM]
        qpad = jnp.concatenate([jnp.concatenate([qa, zeros], axis=1),
                                jnp.concatenate([zeros, qb], axis=1)], axis=0)
        s = jnp.dot(kwin[c0:c0 + BAND_WIN, p * 2 * HEAD_DIM:(p + 1) * 2 * HEAD_DIM], qpad,
                    preferred_element_type=F32)
        kpos = l0 + (c0 - V7X_LANES) + row
        valid = in_band & (kpos >= 0) & (kpos < length)
        scores[j, p] = jnp.where(jnp.concatenate([valid, valid], axis=1), s, NEG_INF)
    probs = {}
    for u in units:
        m = jnp.max(scores[u], axis=0, keepdims=True)
        pr = jnp.exp2(scores[u] - m)
        probs[u] = (m, jnp.sum(pr, axis=0, keepdims=True), pr.astype(BF16))
    outs = {}
    for j, p in units:
        c0 = j * BAND_SUB
        m, l, pb = probs[j, p]
        va = vwin[2 * p * HEAD_DIM:(2 * p + 1) * HEAD_DIM, c0:c0 + BAND_WIN]
        vb = vwin[(2 * p + 1) * HEAD_DIM:(2 * p + 2) * HEAD_DIM, c0:c0 + BAND_WIN]
        oa = jnp.dot(va, pb[:, :BAND_SUB], preferred_element_type=F32)
        ob = jnp.dot(vb, pb[:, BAND_SUB:], preferred_element_type=F32)
        inv = 1.0 / l
        lse = m * LN2 + jnp.log(l)
        outs[j, p] = ([oa * inv[:, :BAND_SUB], ob * inv[:, BAND_SUB:]],
                      [jnp.broadcast_to(lse[:, :BAND_SUB], (HEAD_DIM, BAND_SUB)),
                       jnp.broadcast_to(lse[:, BAND_SUB:], (HEAD_DIM, BAND_SUB))])
    for j in range(tqs // BAND_SUB):
        c0 = j * BAND_SUB
        o_ref[c0:c0 + BAND_SUB, :] = jnp.concatenate([x for p in range(npair) for x in outs[j, p][0]], axis=0).T
        lse_ref[c0:c0 + BAND_SUB, :] = jnp.concatenate([x for p in range(npair) for x in outs[j, p][1]], axis=0).T


def _band(qt, k, vt, *, batch, seq, dil, tqs):
    length = seq // dil
    tqs = min(tqs, length)
    nqb = length // tqs
    ncls = batch * dil
    nkb = length // V7X_LANES
    r128 = tqs // V7X_LANES

    def prev_blk(i):
        return jnp.maximum(i * r128 - 1, 0)

    def next_blk(i):
        return jnp.minimum((i + 1) * r128, nkb - 1)

    kern = functools.partial(_band_kernel, length=length, tqs=tqs)
    out_spec = pl.BlockSpec((None, tqs, A_GROUP_W), lambda c, i: (c, i, 0))
    return pl.pallas_call(
        kern,
        grid=(ncls, nqb),
        in_specs=[
            pl.BlockSpec((None, A_GROUP_W, tqs), lambda c, i: (c, 0, i)),
            pl.BlockSpec((None, V7X_LANES, A_GROUP_W), lambda c, i: (c, prev_blk(i), 0)),
            pl.BlockSpec((None, tqs, A_GROUP_W), lambda c, i: (c, i, 0)),
            pl.BlockSpec((None, V7X_LANES, A_GROUP_W), lambda c, i: (c, next_blk(i), 0)),
            pl.BlockSpec((None, A_GROUP_W, V7X_LANES), lambda c, i: (c, 0, prev_blk(i))),
            pl.BlockSpec((None, A_GROUP_W, tqs), lambda c, i: (c, 0, i)),
            pl.BlockSpec((None, A_GROUP_W, V7X_LANES), lambda c, i: (c, 0, next_blk(i))),
        ],
        out_specs=[out_spec, out_spec],
        out_shape=[jax.ShapeDtypeStruct((ncls, length, A_GROUP_W), F32)] * 2,
        compiler_params=_params("parallel", "parallel"),
        name="band",
    )(qt, k, k, k, vt, vt, vt)


def _outproj_even_kernel(x_ref, o0_ref, o1_ref, o2_ref, l0_ref, l1_ref, l2_ref, obt_ref, wa_ref, wb_ref,
                         out_ref, *bufs, tm):
    def natural(blk_ref, buf_ref, dil):
        if dil == 1:
            return blk_ref[0]
        ntile = A_GROUP_W // V7X_LANES
        for r in range(dil):
            for c in range(ntile):
                buf_ref[c, pl.ds(r, tm // dil, stride=dil), :] = blk_ref[r, :, c * V7X_LANES:(c + 1) * V7X_LANES]
        return jnp.concatenate([buf_ref[c] for c in range(ntile)], axis=1)

    dils = [d for _, d in A_PATTERNS]
    o = [natural(ref, bufs[2 * gi], d) for gi, (ref, d) in enumerate(zip((o0_ref, o1_ref, o2_ref), dils))]
    l = [natural(ref, bufs[2 * gi + 1], d) for gi, (ref, d) in enumerate(zip((l0_ref, l1_ref, l2_ref), dils))]
    m = jnp.maximum(jnp.maximum(l[0], l[1]), l[2])
    w0, w1, w2 = jnp.exp(l[0] - m), jnp.exp(l[1] - m), jnp.exp(l[2] - m)
    oa = (w0 * o[0] + w1 * o[1] + w2 * o[2]) / (w0 + w1 + w2)
    acc = jnp.dot(oa.astype(BF16), wa_ref[...], preferred_element_type=F32)
    acc += lax.dot_general(obt_ref[...], wb_ref[...], (((0,), (0,)), ((), ())),
                           preferred_element_type=F32)
    out_ref[...] = x_ref[...] + acc


def _outproj_even(x2d, oa, lse, obt, wa, wb, *, seq, tm):
    t_total = x2d.shape[0]
    tps = seq // tm
    row = lambda i: (i, 0)
    const = lambda i: (0, 0)
    a_specs = [pl.BlockSpec((d, tm // d, A_GROUP_W), lambda i: (i // tps, i % tps, 0)) for _, d in A_PATTERNS]
    return pl.pallas_call(
        functools.partial(_outproj_even_kernel, tm=tm),
        grid=(t_total // tm,),
        in_specs=[pl.BlockSpec((tm, D_MODEL), row)] + a_specs * 2 + [
            pl.BlockSpec((B_V, tm), lambda i: (0, i)),
            pl.BlockSpec((A_GROUP_W, D_MODEL), const),
            pl.BlockSpec((B_V, D_MODEL), const),
        ],
        out_specs=pl.BlockSpec((tm, D_MODEL), row),
        out_shape=jax.ShapeDtypeStruct((t_total, D_MODEL), F32),
        scratch_shapes=[pltpu.VMEM((A_GROUP_W // V7X_LANES, tm, V7X_LANES), F32)] * (2 * A_GROUPS),
        compiler_params=_params("parallel"),
        name="outproj_even",
    )(x2d, *oa, *lse, obt, wa, wb)


def _outproj_odd_kernel(x_ref, ot_ref, w_ref, out_ref):
    acc = lax.dot_general(ot_ref[...], w_ref[...], (((0,), (0,)), ((), ())), preferred_element_type=F32)
    out_ref[...] = x_ref[...] + acc


def _outproj_odd(x2d, ot, w, *, tm):
    t_total = x2d.shape[0]
    n_in = ot.shape[0]
    return pl.pallas_call(
        _outproj_odd_kernel,
        grid=(t_total // tm,),
        in_specs=[
            pl.BlockSpec((tm, D_MODEL), lambda i: (i, 0)),
            pl.BlockSpec((n_in, tm), lambda i: (0, i)),
            pl.BlockSpec((n_in, D_MODEL), lambda i: (0, 0)),
        ],
        out_specs=pl.BlockSpec((tm, D_MODEL), lambda i: (i, 0)),
        out_shape=jax.ShapeDtypeStruct((t_total, D_MODEL), F32),
        compiler_params=_params("parallel"),
        name="outproj_odd",
    )(x2d, ot, w)


FF_HALO = 16


def _ffn_kernel(xp_ref, x_ref, xn_ref, g_ref, wup_ref, cw_ref, cb_ref, wd_ref, out_ref, hs_ref, acc_ref,
                ug0_ref, ug1_ref, uv0_ref, uv1_ref, act0_ref, act1_ref, *, seq, tm):
    i = pl.program_id(0)
    g = g_ref[...]
    has_prev = ((i * tm) % seq != 0).astype(F32)
    has_next = (((i + 1) * tm) % seq != 0).astype(F32)
    hs_ref[0:FF_HALO, :] = (_rms_rows(xp_ref[...], g) * has_prev).astype(BF16)
    hs_ref[FF_HALO:FF_HALO + tm, :] = _rms_rows(x_ref[...], g).astype(BF16)
    hs_ref[FF_HALO + tm:, :] = (_rms_rows(xn_ref[...], g) * has_next).astype(BF16)
    rows = tm + 2 * FF_HALO
    u_bufs = ((ug0_ref, uv0_ref), (ug1_ref, uv1_ref))
    nchunks = D_FF // FF_CHUNK

    def up_proj(c, which):
        u_ref = u_bufs[c % 2][which]
        c0 = which * D_FF + c * FF_CHUNK
        u_ref[...] = jnp.dot(hs_ref[...], wup_ref[:, c0:c0 + FF_CHUNK], preferred_element_type=F32)

    def conv(u_ref, c0, r0, nr):
        u = u_ref[r0:r0 + nr + 2 * FF_HALO, :]
        w = cw_ref[:, c0:c0 + FF_CHUNK]
        b = cb_ref[:, c0:c0 + FF_CHUNK]
        um = pltpu.roll(u, 1, 0)[FF_HALO:FF_HALO + nr]
        up = pltpu.roll(u, nr + 2 * FF_HALO - 1, 0)[FF_HALO:FF_HALO + nr]
        return um * w[0:1] + u[FF_HALO:FF_HALO + nr] * w[1:2] + up * w[2:3] + b

    act_bufs = (act0_ref, act1_ref)
    half = tm // 2

    def activate(c, r0):
        ug_ref, uv_ref = u_bufs[c % 2]
        gate = conv(ug_ref, c * FF_CHUNK, r0, half)
        val = conv(uv_ref, D_FF + c * FF_CHUNK, r0, half)
        lane0 = (c % 2) * FF_CHUNK
        act_bufs[(c // 2) % 2][r0:r0 + half, lane0:lane0 + FF_CHUNK] = (
            gate * jax.nn.sigmoid(gate) * val).astype(BF16)

    def down_proj(p):
        k0 = 2 * p * FF_CHUNK
        width = min(2 * FF_CHUNK, D_FF - k0)
        part = jnp.dot(act_bufs[p % 2][:, 0:width], wd_ref[k0:k0 + width, :], preferred_element_type=F32)
        if p == 0:
            acc_ref[...] = part
        else:
            acc_ref[...] += part

    npairs = (nchunks + 1) // 2
    up_proj(0, 0)
    up_proj(0, 1)
    for c in range(nchunks):
        if c + 1 < nchunks:
            up_proj(c + 1, 0)
        activate(c, 0)
        if c + 1 < nchunks:
            up_proj(c + 1, 1)
        activate(c, half)
        if c >= 2 and c % 2 == 0:
            down_proj(c // 2 - 1)
    down_proj(npairs - 1)
    out_ref[...] = x_ref[...] + acc_ref[...]


def _ffn(x2d, g, wup, cw, cb, wd, *, seq, tm):
    t_total = x2d.shape[0]
    tm = min(tm, seq)
    r = tm // FF_HALO
    last = t_total // FF_HALO - 1
    const = lambda i: (0, 0)
    resident = dict(pipeline_mode=pl.Buffered(1))
    kern = functools.partial(_ffn_kernel, seq=seq, tm=tm)
    return pl.pallas_call(
        kern,
        grid=(t_total // tm,),
        in_specs=[
            pl.BlockSpec((FF_HALO, D_MODEL), lambda i: (jnp.maximum(i * r - 1, 0), 0)),
            pl.BlockSpec((tm, D_MODEL), lambda i: (i, 0)),
            pl.BlockSpec((FF_HALO, D_MODEL), lambda i: (jnp.minimum((i + 1) * r, last), 0)),
            pl.BlockSpec((1, D_MODEL), const),
            pl.BlockSpec((D_MODEL, 2 * D_FF), const, **resident),
            pl.BlockSpec((3, 2 * D_FF), const),
            pl.BlockSpec((1, 2 * D_FF), const),
            pl.BlockSpec((D_FF, D_MODEL), const, **resident),
        ],
        out_specs=pl.BlockSpec((tm, D_MODEL), lambda i: (i, 0)),
        out_shape=jax.ShapeDtypeStruct((t_total, D_MODEL), F32),
        scratch_shapes=[
            pltpu.VMEM((tm + 2 * FF_HALO, D_MODEL), BF16),
            pltpu.VMEM((tm, D_MODEL), F32),
        ] + [pltpu.VMEM((tm + 2 * FF_HALO, FF_CHUNK), F32)] * 4 + [pltpu.VMEM((tm, 2 * FF_CHUNK), BF16)] * 2,
        compiler_params=_params("parallel"),
        name="ffn",
    )(x2d, x2d, x2d, g, wup, cw, cb, wd)


def _rope_tables(pos, dim, theta):
    inv = theta ** (-jnp.arange(0, dim, 2, dtype=F32) / dim)
    ang = pos.astype(F32)[:, None] * inv[None, :]
    return jnp.cos(ang).T, jnp.sin(ang).T


def _class_tiles(table, dil, n):
    rows, seq = table.shape
    return table.reshape(rows, seq // (dil * n), n, dil).transpose(1, 0, 3, 2).reshape(-1, rows, dil * n)


def _lane_bcast(v, width):
    return jnp.broadcast_to(v.astype(F32)[:, None], (v.shape[0], width))


TM = 512
TQ_C = 512
TQ_B = 1024
TKC = 512
TQ_BAND = 512
A_CLASS_TOKENS = (512, 512, 128)
Q_SCALE = HEAD_DIM ** -0.5 * LOG2E


def _trunk(x, norm_mix, norm_ffn, w_in_ab, q_norm_a, k_norm_a, q_norm_b, k_norm_b,
           lambda_q1, lambda_k1, lambda_q2, lambda_k2, subln_b, w_out_ab,
           w_in_c, q_norm_c, k_norm_c, w_out_c, w_up, conv_w, conv_b, w_down):
    batch, seq, _ = x.shape
    t_total = batch * seq
    depth = norm_mix.shape[0]
    x2d = x.reshape(t_total, D_MODEL)

    cos, sin = _rope_tables(jnp.arange(seq), ROT_DIM, ROPE_THETA)
    rows = seq // GRID_W
    row = jnp.repeat(jnp.arange(rows), GRID_W)
    col = jnp.tile(jnp.arange(GRID_W), rows)
    cr, sr = _rope_tables(row, AXIAL_DIM, AXIAL_THETA)
    cc, sc = _rope_tables(col, AXIAL_DIM, AXIAL_THETA)
    cos_ax = jnp.concatenate([cr, cc], axis=0)
    sin_ax = jnp.concatenate([sr, sc], axis=0)

    for i in range(depth):
        j = i // 2
        g_mix = norm_mix[i].reshape(1, D_MODEL)
        if i % 2 == 0:
            lam_init = 0.8 - 0.6 * math.exp(-0.3 * i)
            w = w_in_ab[j]
            oa, lse = [], []
            for gi, (_, dil) in enumerate(A_PATTERNS):
                n_cls = min(A_CLASS_TOKENS[gi], seq // dil)
                width = dil * n_cls
                gains_a = jnp.concatenate([_lane_bcast(q_norm_a[j] * Q_SCALE, width),
                                           _lane_bcast(k_norm_a[j], width)], 0)
                sl = slice(gi * A_GROUP_W, (gi + 1) * A_GROUP_W)
                wt = jnp.concatenate([w[:, sl], w[:, A_QKV:2 * A_QKV][:, sl], w[:, 2 * A_QKV:3 * A_QKV][:, sl]],
                                     axis=1).T.astype(BF16)
                qt, k, vt = _proj_cls(x2d, g_mix, wt, gains_a, _class_tiles(cos, dil, n_cls),
                                      _class_tiles(sin, dil, n_cls), batch=batch, seq=seq, dil=dil, n=n_cls)
                o_g, lse_g = _band(qt, k, vt, batch=batch, seq=seq, dil=dil, tqs=TQ_BAND)
                oa.append(o_g)
                lse.append(lse_g)
            wt_b = w[:, 3 * A_QKV:].T.astype(BF16)
            gains_b = jnp.concatenate([_lane_bcast(q_norm_b[j] * Q_SCALE, TM), _lane_bcast(k_norm_b[j], TM)], 0)
            qt, k, vt = _proj(x2d, g_mix, wt_b, gains_b, cos, sin, seq=seq,
                              nq=2 * B_HEADS, nk=2 * B_HEADS, nv=B_V, axial=False, tm=TM)
            lam = (jnp.exp(jnp.sum(lambda_q1[j].astype(F32) * lambda_k1[j].astype(F32)))
                   - jnp.exp(jnp.sum(lambda_q2[j].astype(F32) * lambda_k2[j].astype(F32))) + lam_init)
            sg = _lane_bcast(subln_b[j] * (1.0 - lam_init), TQ_B)
            obt = _attn_b(lam.reshape(1).astype(F32), qt, k, vt, sg, batch=batch, seq=seq, tq=TQ_B, tkc=TKC)
            wo = w_out_ab[j].astype(BF16)
            x2d = _outproj_even(x2d, oa, lse, obt, wo[:A_GROUP_W], wo[A_GROUP_W:], seq=seq, tm=TM)
        else:
            wt = w_in_c[j].T.astype(BF16)
            gains_c = jnp.concatenate([_lane_bcast(q_norm_c[j] * Q_SCALE, TM), _lane_bcast(k_norm_c[j], TM)], 0)
            qt, k, vt = _proj(x2d, g_mix, wt, gains_c, cos_ax, sin_ax, seq=seq,
                              nq=C_Q_HEADS, nk=C_KV_HEADS, nv=C_KV_HEADS * HEAD_DIM, axial=True, tm=TM)
            ot = _attn_c(qt, k, vt, batch=batch, seq=seq, tq=TQ_C, tkc=TKC)
            x2d = _outproj_odd(x2d, ot, w_out_c[j].astype(BF16), tm=TM)
        x2d = _ffn(x2d, norm_ffn[i].reshape(1, D_MODEL), w_up[i].astype(BF16), conv_w[i],
                   conv_b[i].reshape(1, 2 * D_FF), w_down[i].astype(BF16), seq=seq, tm=TM)
    return x2d.reshape(batch, seq, D_MODEL)


def kernel(x_prompt, x_sample, norm_mix, norm_ffn, w_in_ab, q_norm_a, k_norm_a, q_norm_b, k_norm_b,
           lambda_q1, lambda_k1, lambda_q2, lambda_k2, subln_b, w_out_ab, w_in_c, q_norm_c, k_norm_c,
           w_out_c, w_up, conv_w, conv_b, w_down):
    params = (norm_mix, norm_ffn, w_in_ab, q_norm_a, k_norm_a, q_norm_b, k_norm_b,
              lambda_q1, lambda_k1, lambda_q2, lambda_k2, subln_b, w_out_ab,
              w_in_c, q_norm_c, k_norm_c, w_out_c, w_up, conv_w, conv_b, w_down)
    return (_trunk(x_prompt, *params), _trunk(x_sample, *params))
```

```python
import functools
import math

import jax
import jax.numpy as jnp
from jax import lax
from jax.experimental import pallas as pl
from jax.experimental.pallas import tpu as pltpu

F32 = jnp.float32
BF16 = jnp.bfloat16

D_MODEL = 1024
HEAD_DIM = 64
A_PATTERNS = ((128, 1), (512, 4), (2048, 16))
A_GROUPS = 3
A_HEADS_PER_GROUP = 4
A_QKV = A_GROUPS * A_HEADS_PER_GROUP * HEAD_DIM
A_GROUP_W = A_HEADS_PER_GROUP * HEAD_DIM
B_HEADS = 4
B_QK = B_HEADS * 2 * HEAD_DIM
B_V = B_HEADS * 2 * HEAD_DIM
C_Q_HEADS = 16
C_KV_HEADS = 4
C_GROUP = C_Q_HEADS // C_KV_HEADS
ROPE_THETA = 500000.0
ROT_DIM = HEAD_DIM // 4
AXIAL_THETA = 10000.0
AXIAL_DIM = HEAD_DIM // 2
GRID_W = 64
D_FF = 2816
NORM_EPS = 1e-6
SUBLN_EPS = 1e-5
NEG_INF = -1e30
LOG2E = 1.4426950408889634
LN2 = 0.6931471805599453

V7X_LANES = 128
V7X_VMEM_BYTES = 64 * 1024 * 1024
VMEM_LIMIT = V7X_VMEM_BYTES - 8 * 1024 * 1024
BAND_HALF = 64
BAND_SUB = 128
BAND_WIN = BAND_SUB + 2 * V7X_LANES
FF_CHUNK = 256


def _params(*semantics):
    return pltpu.CompilerParams(dimension_semantics=semantics, vmem_limit_bytes=VMEM_LIMIT)


def _rms_rows(x, g):
    ms = jnp.mean(x * x, axis=-1, keepdims=True)
    return x * lax.rsqrt(ms + NORM_EPS) * g


def _head_norm_rot(y, gain, cos, sin, axial):
    ssq = jnp.sum(y * y, axis=0, keepdims=True)
    yn = y * lax.rsqrt(ssq * (1.0 / HEAD_DIM) + NORM_EPS) * gain
    if axial:
        h = AXIAL_DIM // 2
        a1, a2, b1, b2 = yn[0:h], yn[h:2 * h], yn[2 * h:3 * h], yn[3 * h:4 * h]
        cr, cc = cos[0:h], cos[h:2 * h]
        sr, sc = sin[0:h], sin[h:2 * h]
        return jnp.concatenate(
            [a1 * cr - a2 * sr, a2 * cr + a1 * sr, b1 * cc - b2 * sc, b2 * cc + b1 * sc], axis=0)
    h = ROT_DIM // 2
    x1, x2 = yn[0:h], yn[h:2 * h]
    return jnp.concatenate([x1 * cos - x2 * sin, x2 * cos + x1 * sin, yn[2 * h:]], axis=0)


def _proj_kernel(x_ref, g_ref, wt_ref, gain_ref, cos_ref, sin_ref, qt_ref, k_ref, vt_ref,
                 *, nq, nk, axial):
    h = _rms_rows(x_ref[...], g_ref[...]).astype(BF16)
    yt = lax.dot_general(wt_ref[...], h, (((1,), (1,)), ((), ())),
                         preferred_element_type=F32)
    cos = cos_ref[...]
    sin = sin_ref[...]
    gq = gain_ref[0:HEAD_DIM, :]
    gk = gain_ref[HEAD_DIM:2 * HEAD_DIM, :]
    for i in range(nq):
        y = yt[i * HEAD_DIM:(i + 1) * HEAD_DIM, :]
        qt_ref[i * HEAD_DIM:(i + 1) * HEAD_DIM, :] = _head_norm_rot(y, gq, cos, sin, axial).astype(BF16)
    base = nq * HEAD_DIM
    for p in range(nk // 2):
        pair = []
        for i in (2 * p, 2 * p + 1):
            y = yt[base + i * HEAD_DIM:base + (i + 1) * HEAD_DIM, :]
            pair.append(_head_norm_rot(y, gk, cos, sin, axial))
        kt = jnp.concatenate(pair, axis=0)
        k_ref[:, p * 2 * HEAD_DIM:(p + 1) * 2 * HEAD_DIM] = kt.T.astype(BF16)
    base = (nq + nk) * HEAD_DIM
    vt_ref[...] = yt[base:, :].astype(BF16)


def _proj(x2d, g, wt, gains, cos_t, sin_t, *, seq, nq, nk, nv, axial, tm):
    t_total = x2d.shape[0]
    tm = min(tm, seq)
    nlb = seq // tm
    n_out = (nq + nk) * HEAD_DIM + nv
    rot_rows = cos_t.shape[0]
    kern = functools.partial(_proj_kernel, nq=nq, nk=nk, axial=axial)
    return pl.pallas_call(
        kern,
        grid=(t_total // tm,),
        in_specs=[
            pl.BlockSpec((tm, D_MODEL), lambda i: (i, 0)),
            pl.BlockSpec((1, D_MODEL), lambda i: (0, 0)),
            pl.BlockSpec((n_out, D_MODEL), lambda i: (0, 0)),
            pl.BlockSpec((2 * HEAD_DIM, tm), lambda i: (0, 0)),
            pl.BlockSpec((rot_rows, tm), lambda i: (0, i % nlb)),
            pl.BlockSpec((rot_rows, tm), lambda i: (0, i % nlb)),
        ],
        out_specs=[
            pl.BlockSpec((nq * HEAD_DIM, tm), lambda i: (0, i)),
            pl.BlockSpec((tm, nk * HEAD_DIM), lambda i: (i, 0)),
            pl.BlockSpec((nv, tm), lambda i: (0, i)),
        ],
        out_shape=[
            jax.ShapeDtypeStruct((nq * HEAD_DIM, t_total), BF16),
            jax.ShapeDtypeStruct((t_total, nk * HEAD_DIM), BF16),
            jax.ShapeDtypeStruct((nv, t_total), BF16),
        ],
        compiler_params=_params("parallel"),
        name="proj",
    )(x2d, g, wt, gains[:, :tm], cos_t, sin_t)


def _proj_cls_kernel(x_ref, g_ref, wt_ref, gain_ref, cos_ref, sin_ref, qt_ref, k_ref, vt_ref, hs_ref,
                     stage_ref, *, dil, n):
    h = _rms_rows(x_ref[...], g_ref[...])
    if dil == 1:
        hs_ref[...] = h.astype(BF16)
    else:
        for c in range(D_MODEL // V7X_LANES):
            stage_ref[c] = h[:, c * V7X_LANES:(c + 1) * V7X_LANES]
        for r in range(dil):
            for c in range(D_MODEL // V7X_LANES):
                hs_ref[r * n:(r + 1) * n, c * V7X_LANES:(c + 1) * V7X_LANES] = (
                    stage_ref[c, pl.ds(r, n, stride=dil), :].astype(BF16))
    yt = lax.dot_general(wt_ref[...], hs_ref[...], (((1,), (1,)), ((), ())),
                         preferred_element_type=F32)
    cos = cos_ref[...]
    sin = sin_ref[...]
    gq = gain_ref[0:HEAD_DIM, :]
    gk = gain_ref[HEAD_DIM:2 * HEAD_DIM, :]
    nh = A_HEADS_PER_GROUP
    for i in range(nh):
        q = _head_norm_rot(yt[i * HEAD_DIM:(i + 1) * HEAD_DIM, :], gq, cos, sin, False).astype(BF16)
        for r in range(dil):
            qt_ref[r, i * HEAD_DIM:(i + 1) * HEAD_DIM, :] = q[:, r * n:(r + 1) * n]
    base = nh * HEAD_DIM
    for p in range(nh // 2):
        pair = [_head_norm_rot(yt[base + i * HEAD_DIM:base + (i + 1) * HEAD_DIM, :], gk, cos, sin, False)
                for i in (2 * p, 2 * p + 1)]
        kt = jnp.concatenate(pair, axis=0).T.astype(BF16)
        for r in range(dil):
            k_ref[r, :, p * 2 * HEAD_DIM:(p + 1) * 2 * HEAD_DIM] = kt[r * n:(r + 1) * n, :]
    base = 2 * nh * HEAD_DIM
    v = yt[base:, :].astype(BF16)
    for r in range(dil):
        vt_ref[r] = v[:, r * n:(r + 1) * n]


def _proj_cls(x2d, g, wt, gains, cos_c, sin_c, *, batch, seq, dil, n):
    t_total = x2d.shape[0]
    tmx = dil * n
    length = seq // dil
    tps = seq // tmx
    kern = functools.partial(_proj_cls_kernel, dil=dil, n=n)
    feat = pl.BlockSpec((dil, A_GROUP_W, n), lambda i: (i // tps, 0, i % tps))
    return pl.pallas_call(
        kern,
        grid=(t_total // tmx,),
        in_specs=[
            pl.BlockSpec((tmx, D_MODEL), lambda i: (i, 0)),
            pl.BlockSpec((1, D_MODEL), lambda i: (0, 0)),
            pl.BlockSpec((3 * A_GROUP_W, D_MODEL), lambda i: (0, 0)),
            pl.BlockSpec((2 * HEAD_DIM, tmx), lambda i: (0, 0)),
            pl.BlockSpec((None, ROT_DIM // 2, tmx), lambda i: (i % tps, 0, 0)),
            pl.BlockSpec((None, ROT_DIM // 2, tmx), lambda i: (i % tps, 0, 0)),
        ],
        out_specs=[feat, pl.BlockSpec((dil, n, A_GROUP_W), lambda i: (i // tps, i % tps, 0)), feat],
        out_shape=[
            jax.ShapeDtypeStruct((batch * dil, A_GROUP_W, length), BF16),
            jax.ShapeDtypeStruct((batch * dil, length, A_GROUP_W), BF16),
            jax.ShapeDtypeStruct((batch * dil, A_GROUP_W, length), BF16),
        ],
        scratch_shapes=[pltpu.VMEM((tmx, D_MODEL), BF16),
                        pltpu.VMEM((D_MODEL // V7X_LANES, tmx if dil > 1 else 8, V7X_LANES), F32)],
        compiler_params=_params("parallel"),
        name="proj_cls",
    )(x2d, g, wt, gains, cos_c, sin_c)


FLASH_UNROLL = 4
FLASH_COL_TILE = 256
V_PAD = 16


def _flash_scratch(dv, ncols, seq, tkc):
    return [
        pltpu.VMEM((2, 2 * HEAD_DIM, ncols), BF16),
        pltpu.VMEM((dv + V_PAD, seq), BF16),
        pltpu.VMEM((dv + V_PAD, ncols), F32),
        pltpu.VMEM((1, ncols), F32),
        pltpu.VMEM((tkc, ncols), F32), pltpu.VMEM((tkc, ncols), F32),
        pltpu.VMEM((1, ncols), F32), pltpu.VMEM((1, ncols), F32),
    ]


def _flash_chain(k_ref, vt_ref, scratch, build_qpad, epilogue, *, dv, seq, tq, tkc):
    qpad_ref, vext_ref, acc_ref, m_ref, s0, s1, c0, c1 = scratch
    s_bufs, cm_bufs = (s0, s1), (c0, c1)
    nc = seq // tkc
    nqb = seq // tq
    unroll = FLASH_UNROLL if nc % FLASH_UNROLL == 0 else nc
    ncols = qpad_ref.shape[2]
    tiles = [slice(j * FLASH_COL_TILE, (j + 1) * FLASH_COL_TILE) for j in range(ncols // FLASH_COL_TILE)]

    vext_ref[0:dv, :] = vt_ref[...]
    ones_row = lax.broadcasted_iota(jnp.int32, (V_PAD, seq), 0) == 0
    vext_ref[dv:, :] = jnp.where(ones_row, 1.0, 0.0).astype(BF16)

    def chunk(c):
        return pl.ds(c * tkc if isinstance(c, int) else pl.multiple_of(c * tkc, tkc), tkc)

    def scores(qslot, c, slot, t):
        s = jnp.dot(k_ref[chunk(c), :], qpad_ref[qslot, :, t], preferred_element_type=F32)
        s_bufs[slot][:, t] = s
        cm_bufs[slot][:, t] = jnp.max(s, axis=0, keepdims=True)

    def softmax_pv(c, slot, t, first=False):
        cm = cm_bufs[slot][:, t]
        if first:
            m_new = cm
        else:
            m_old = m_ref[:, t]
            m_new = jnp.maximum(m_old, cm)
        m_ref[:, t] = m_new
        p = jnp.exp2((s_bufs[slot][:, t] - m_new).astype(BF16))
        pv = jnp.dot(vext_ref[:, chunk(c)], p, preferred_element_type=F32)
        acc_ref[:, t] = pv if first else jnp.exp2(m_old - m_new) * acc_ref[:, t] + pv

    def step(c, slot, nxt=None, first=False):
        for t in tiles:
            if nxt is not None:
                scores(nxt[0], nxt[1], 1 - slot, t)
            softmax_pv(c, slot, t, first)

    def block_steps(qslot, after_last):
        if nc > unroll:
            step(0, 0, nxt=(qslot, 1), first=True)
            for c in range(1, unroll):
                step(c, c % 2, nxt=(qslot, c + 1))

            def group(g, inner):
                for u in range(unroll):
                    step(unroll * g + u, u % 2, nxt=(qslot, unroll * g + u + 1))
                return inner

            lax.fori_loop(1, nc // unroll - 1, group, 0)
        tail = unroll if nc > unroll else nc
        for c in range(nc - tail, nc):
            step(c, c % 2, nxt=(qslot, c + 1) if c + 1 < nc else after_last, first=(c == 0))

    if nqb % 2 == 0:
        build_qpad(0, 0)
        for t in tiles:
            scores(0, 0, 0, t)

        def pair(i, carry):
            for par in (0, 1):
                qb = 2 * i + par
                build_qpad(jnp.minimum(qb + 1, nqb - 1), 1 - par)
                block_steps(par, (1 - par, 0))
                epilogue(qb)
            return carry

        lax.fori_loop(0, nqb // 2, pair, 0)
    else:
        def block(qb, carry):
            build_qpad(qb, 0)
            for t in tiles:
                scores(0, 0, 0, t)
            block_steps(0, None)
            epilogue(qb)
            return carry

        lax.fori_loop(0, nqb, block, 0)


def _attn_c_kernel(qt_ref, k_ref, vt_ref, o_ref, *scratch, seq, tq, tkc):
    qpad_ref, acc_ref = scratch[0], scratch[2]
    upper = (pl.program_id(1) % 2).astype(F32)

    def col(qb):
        return pl.ds(pl.multiple_of(qb * tq, tq), tq)

    def build_qpad(qb, slot):
        qcat = jnp.concatenate([qt_ref[g * HEAD_DIM:(g + 1) * HEAD_DIM, col(qb)] for g in range(C_GROUP)],
                               axis=1).astype(F32)
        qpad_ref[slot, 0:HEAD_DIM, :] = (qcat * (1.0 - upper)).astype(BF16)
        qpad_ref[slot, HEAD_DIM:, :] = (qcat * upper).astype(BF16)

    def epilogue(qb):
        o = acc_ref[0:HEAD_DIM, :] * (1.0 / acc_ref[HEAD_DIM:HEAD_DIM + 1, :])
        for g in range(C_GROUP):
            o_ref[g * HEAD_DIM:(g + 1) * HEAD_DIM, col(qb)] = o[:, g * tq:(g + 1) * tq].astype(BF16)

    _flash_chain(k_ref, vt_ref, scratch, build_qpad, epilogue, dv=HEAD_DIM, seq=seq, tq=tq, tkc=tkc)


def _attn_c(qt, k, vt, *, batch, seq, tq, tkc):
    t_total = batch * seq
    tq = min(tq, seq)
    tkc = min(tkc, seq)
    kern = functools.partial(_attn_c_kernel, seq=seq, tq=tq, tkc=tkc)
    return pl.pallas_call(
        kern,
        grid=(batch, C_KV_HEADS),
        in_specs=[
            pl.BlockSpec((C_GROUP * HEAD_DIM, seq), lambda b, n: (n, b)),
            pl.BlockSpec((seq, 2 * HEAD_DIM), lambda b, n: (b, n // 2)),
            pl.BlockSpec((HEAD_DIM, seq), lambda b, n: (n, b)),
        ],
        out_specs=pl.BlockSpec((C_GROUP * HEAD_DIM, seq), lambda b, n: (n, b)),
        out_shape=jax.ShapeDtypeStruct((C_Q_HEADS * HEAD_DIM, t_total), BF16),
        scratch_shapes=_flash_scratch(HEAD_DIM, C_GROUP * tq, seq, tkc),
        compiler_params=_params("parallel", "parallel"),
        name="attn_c",
    )(qt, k, vt)


def _attn_b_kernel(lam_ref, qt_ref, k_ref, vt_ref, sg_ref, o_ref, *scratch, seq, tq, tkc):
    dv = 2 * HEAD_DIM
    qpad_ref, acc_ref = scratch[0], scratch[2]
    zeros = jnp.zeros((HEAD_DIM, tq), BF16)

    def col(qb):
        return pl.ds(pl.multiple_of(qb * tq, tq), tq)

    def build_qpad(qb, slot):
        q = qt_ref[:, col(qb)]
        qpad_ref[slot, 0:HEAD_DIM, 0:tq] = q[0:HEAD_DIM]
        qpad_ref[slot, 0:HEAD_DIM, tq:] = zeros
        qpad_ref[slot, HEAD_DIM:, 0:tq] = zeros
        qpad_ref[slot, HEAD_DIM:, tq:] = q[HEAD_DIM:]

    def epilogue(qb):
        on = acc_ref[0:dv, :] * (1.0 / acc_ref[dv:dv + 1, :])
        o = on[:, 0:tq] - lam_ref[0] * on[:, tq:]
        ms = jnp.mean(o * o, axis=0, keepdims=True)
        o_ref[:, col(qb)] = (o * lax.rsqrt(ms + SUBLN_EPS) * sg_ref[...]).astype(BF16)

    _flash_chain(k_ref, vt_ref, scratch, build_qpad, epilogue, dv=dv, seq=seq, tq=tq, tkc=tkc)


def _attn_b(lam, qt, k, vt, sg, *, batch, seq, tq, tkc):
    t_total = batch * seq
    tq = min(tq, seq)
    tkc = min(tkc, seq)
    kern = functools.partial(_attn_b_kernel, seq=seq, tq=tq, tkc=tkc)
    return pl.pallas_call(
        kern,
        grid=(batch, B_HEADS),
        in_specs=[
            pl.BlockSpec(memory_space=pltpu.SMEM),
            pl.BlockSpec((2 * HEAD_DIM, seq), lambda b, h: (h, b)),
            pl.BlockSpec((seq, 2 * HEAD_DIM), lambda b, h: (b, h)),
            pl.BlockSpec((2 * HEAD_DIM, seq), lambda b, h: (h, b)),
            pl.BlockSpec((2 * HEAD_DIM, tq), lambda b, h: (0, 0)),
        ],
        out_specs=pl.BlockSpec((2 * HEAD_DIM, seq), lambda b, h: (h, b)),
        out_shape=jax.ShapeDtypeStruct((B_V, t_total), BF16),
        scratch_shapes=_flash_scratch(2 * HEAD_DIM, 2 * tq, seq, tkc),
        compiler_params=_params("parallel", "parallel"),
        name="attn_b",
    )(lam, qt, k, vt, sg[:, :tq])


def _band_kernel(qt_ref, kp_ref, km_ref, kn_ref, vp_ref, vm_ref, vn_ref, o_ref, lse_ref, *, length, tqs):
    l0 = pl.program_id(1) * tqs
    kwin = jnp.concatenate([kp_ref[...], km_ref[...], kn_ref[...]], axis=0)
    vwin = jnp.concatenate([vp_ref[...], vm_ref[...], vn_ref[...]], axis=1)
    row = lax.broadcasted_iota(jnp.int32, (BAND_WIN, BAND_SUB), 0)
    col = lax.broadcasted_iota(jnp.int32, (BAND_WIN, BAND_SUB), 1)
    rel = row - V7X_LANES - col
    in_band = jnp.abs(rel) <= BAND_HALF
    zeros = jnp.zeros((HEAD_DIM, BAND_SUB), BF16)
    npair = A_HEADS_PER_GROUP // 2
    units = [(j, p) for j in range(tqs // BAND_SUB) for p in range(npair)]
    scores = {}
    for j, p in units:
        c0 = j * BAND_SUB
        q = qt_ref[:, c0:c0 + BAND_SUB]
        qa = q[2 * p * HEAD_DIM:(2 * p + 1) * HEAD_DIM]
        qb = q[(2 * p + 1) * HEAD_DIM:(2 * p + 2) * HEAD_DIM]
        qpad = jnp.concatenate([jnp.concatenate([qa, zeros], axis=1),
                                jnp.concatenate([zeros, qb], axis=1)], axis=0)
        s = jnp.dot(kwin[c0:c0 + BAND_WIN, p * 2 * HEAD_DIM:(p + 1) * 2 * HEAD_DIM], qpad,
                    preferred_element_type=F32)
        kpos = l0 + (c0 - V7X_LANES) + row
        valid = in_band & (kpos >= 0) & (kpos < length)
        scores[j, p] = jnp.where(jnp.concatenate([valid, valid], axis=1), s, NEG_INF)
    probs = {}
    for u in units:
        m = jnp.max(scores[u], axis=0, keepdims=True)
        pr = jnp.exp2(scores[u] - m)
        probs[u] = (m, jnp.sum(pr, axis=0, keepdims=True), pr.astype(BF16))
    outs = {}
    for j, p in units:
        c0 = j * BAND_SUB
        m, l, pb = probs[j, p]
        va = vwin[2 * p * HEAD_DIM:(2 * p + 1) * HEAD_DIM, c0:c0 + BAND_WIN]
        vb = vwin[(2 * p + 1) * HEAD_DIM:(2 * p + 2) * HEAD_DIM, c0:c0 + BAND_WIN]
        oa = jnp.dot(va, pb[:, :BAND_SUB], preferred_element_type=F32)
        ob = jnp.dot(vb, pb[:, BAND_SUB:], preferred_element_type=F32)
        inv = 1.0 / l
        lse = m * LN2 + jnp.log(l)
        outs[j, p] = ([oa * inv[:, :BAND_SUB], ob * inv[:, BAND_SUB:]],
                      [jnp.broadcast_to(lse[:, :BAND_SUB], (HEAD_DIM, BAND_SUB)),
                       jnp.broadcast_to(lse[:, BAND_SUB:], (HEAD_DIM, BAND_SUB))])
    for j in range(tqs // BAND_SUB):
        c0 = j * BAND_SUB
        o_ref[c0:c0 + BAND_SUB, :] = jnp.concatenate([x for p in range(npair) for x in outs[j, p][0]], axis=0).T
        lse_ref[c0:c0 + BAND_SUB, :] = jnp.concatenate([x for p in range(npair) for x in outs[j, p][1]], axis=0).T


def _band(qt, k, vt, *, batch, seq, dil, tqs):
    length = seq // dil
    tqs = min(tqs, length)
    nqb = length // tqs
    ncls = batch * dil
    nkb = length // V7X_LANES
    r128 = tqs // V7X_LANES

    def prev_blk(i):
        return jnp.maximum(i * r128 - 1, 0)

    def next_blk(i):
        return jnp.minimum((i + 1) * r128, nkb - 1)

    kern = functools.partial(_band_kernel, length=length, tqs=tqs)
    out_spec = pl.BlockSpec((None, tqs, A_GROUP_W), lambda c, i: (c, i, 0))
    return pl.pallas_call(
        kern,
        grid=(ncls, nqb),
        in_specs=[
            pl.BlockSpec((None, A_GROUP_W, tqs), lambda c, i: (c, 0, i)),
            pl.BlockSpec((None, V7X_LANES, A_GROUP_W), lambda c, i: (c, prev_blk(i), 0)),
            pl.BlockSpec((None, tqs, A_GROUP_W), lambda c, i: (c, i, 0)),
            pl.BlockSpec((None, V7X_LANES, A_GROUP_W), lambda c, i: (c, next_blk(i), 0)),
            pl.BlockSpec((None, A_GROUP_W, V7X_LANES), lambda c, i: (c, 0, prev_blk(i))),
            pl.BlockSpec((None, A_GROUP_W, tqs), lambda c, i: (c, 0, i)),
            pl.BlockSpec((None, A_GROUP_W, V7X_LANES), lambda c, i: (c, 0, next_blk(i))),
        ],
        out_specs=[out_spec, out_spec],
        out_shape=[jax.ShapeDtypeStruct((ncls, length, A_GROUP_W), F32)] * 2,
        compiler_params=_params("parallel", "parallel"),
        name="band",
    )(qt, k, k, k, vt, vt, vt)


def _outproj_even_kernel(x_ref, o0_ref, o1_ref, o2_ref, l0_ref, l1_ref, l2_ref, obt_ref, wa_ref, wb_ref,
                         out_ref, *bufs, tm):
    def natural(blk_ref, buf_ref, dil):
        if dil == 1:
            return blk_ref[0]
        ntile = A_GROUP_W // V7X_LANES
        for r in range(dil):
            for c in range(ntile):
                buf_ref[c, pl.ds(r, tm // dil, stride=dil), :] = blk_ref[r, :, c * V7X_LANES:(c + 1) * V7X_LANES]
        return jnp.concatenate([buf_ref[c] for c in range(ntile)], axis=1)

    dils = [d for _, d in A_PATTERNS]
    o = [natural(ref, bufs[2 * gi], d) for gi, (ref, d) in enumerate(zip((o0_ref, o1_ref, o2_ref), dils))]
    l = [natural(ref, bufs[2 * gi + 1], d) for gi, (ref, d) in enumerate(zip((l0_ref, l1_ref, l2_ref), dils))]
    m = jnp.maximum(jnp.maximum(l[0], l[1]), l[2])
    w0, w1, w2 = jnp.exp(l[0] - m), jnp.exp(l[1] - m), jnp.exp(l[2] - m)
    oa = (w0 * o[0] + w1 * o[1] + w2 * o[2]) / (w0 + w1 + w2)
    acc = jnp.dot(oa.astype(BF16), wa_ref[...], preferred_element_type=F32)
    acc += lax.dot_general(obt_ref[...], wb_ref[...], (((0,), (0,)), ((), ())),
                           preferred_element_type=F32)
    out_ref[...] = x_ref[...] + acc


def _outproj_even(x2d, oa, lse, obt, wa, wb, *, seq, tm):
    t_total = x2d.shape[0]
    tps = seq // tm
    row = lambda i: (i, 0)
    const = lambda i: (0, 0)
    a_specs = [pl.BlockSpec((d, tm // d, A_GROUP_W), lambda i: (i // tps, i % tps, 0)) for _, d in A_PATTERNS]
    return pl.pallas_call(
        functools.partial(_outproj_even_kernel, tm=tm),
        grid=(t_total // tm,),
        in_specs=[pl.BlockSpec((tm, D_MODEL), row)] + a_specs * 2 + [
            pl.BlockSpec((B_V, tm), lambda i: (0, i)),
            pl.BlockSpec((A_GROUP_W, D_MODEL), const),
            pl.BlockSpec((B_V, D_MODEL), const),
        ],
        out_specs=pl.BlockSpec((tm, D_MODEL), row),
        out_shape=jax.ShapeDtypeStruct((t_total, D_MODEL), F32),
        scratch_shapes=[pltpu.VMEM((A_GROUP_W // V7X_LANES, tm, V7X_LANES), F32)] * (2 * A_GROUPS),
        compiler_params=_params("parallel"),
        name="outproj_even",
    )(x2d, *oa, *lse, obt, wa, wb)


def _outproj_odd_kernel(x_ref, ot_ref, w_ref, out_ref):
    acc = lax.dot_general(ot_ref[...], w_ref[...], (((0,), (0,)), ((), ())), preferred_element_type=F32)
    out_ref[...] = x_ref[...] + acc


def _outproj_odd(x2d, ot, w, *, tm):
    t_total = x2d.shape[0]
    n_in = ot.shape[0]
    return pl.pallas_call(
        _outproj_odd_kernel,
        grid=(t_total // tm,),
        in_specs=[
            pl.BlockSpec((tm, D_MODEL), lambda i: (i, 0)),
            pl.BlockSpec((n_in, tm), lambda i: (0, i)),
            pl.BlockSpec((n_in, D_MODEL), lambda i: (0, 0)),
        ],
        out_specs=pl.BlockSpec((tm, D_MODEL), lambda i: (i, 0)),
        out_shape=jax.ShapeDtypeStruct((t_total, D_MODEL), F32),
        compiler_params=_params("parallel"),
        name="outproj_odd",
    )(x2d, ot, w)


FF_HALO = 16


def _ffn_kernel(xp_ref, x_ref, xn_ref, g_ref, wup_ref, cw_ref, cb_ref, wd_ref, out_ref, hs_ref, acc_ref,
                ug0_ref, ug1_ref, uv0_ref, uv1_ref, act0_ref, act1_ref, *, seq, tm):
    i = pl.program_id(0)
    g = g_ref[...]
    has_prev = ((i * tm) % seq != 0).astype(F32)
    has_next = (((i + 1) * tm) % seq != 0).astype(F32)
    hs_ref[0:FF_HALO, :] = (_rms_rows(xp_ref[...], g) * has_prev).astype(BF16)
    hs_ref[FF_HALO:FF_HALO + tm, :] = _rms_rows(x_ref[...], g).astype(BF16)
    hs_ref[FF_HALO + tm:, :] = (_rms_rows(xn_ref[...], g) * has_next).astype(BF16)
    rows = tm + 2 * FF_HALO
    u_bufs = ((ug0_ref, uv0_ref), (ug1_ref, uv1_ref))
    nchunks = D_FF // FF_CHUNK

    def up_proj(c, which):
        u_ref = u_bufs[c % 2][which]
        c0 = which * D_FF + c * FF_CHUNK
        u_ref[...] = jnp.dot(hs_ref[...], wup_ref[:, c0:c0 + FF_CHUNK], preferred_element_type=F32)

    def conv(u_ref, c0, r0, nr):
        u = u_ref[r0:r0 + nr + 2 * FF_HALO, :]
        w = cw_ref[:, c0:c0 + FF_CHUNK]
        b = cb_ref[:, c0:c0 + FF_CHUNK]
        um = pltpu.roll(u, 1, 0)[FF_HALO:FF_HALO + nr]
        up = pltpu.roll(u, nr + 2 * FF_HALO - 1, 0)[FF_HALO:FF_HALO + nr]
        return um * w[0:1] + u[FF_HALO:FF_HALO + nr] * w[1:2] + up * w[2:3] + b

    act_bufs = (act0_ref, act1_ref)
    half = tm // 2

    def activate(c, r0):
        ug_ref, uv_ref = u_bufs[c % 2]
        gate = conv(ug_ref, c * FF_CHUNK, r0, half)
        val = conv(uv_ref, D_FF + c * FF_CHUNK, r0, half)
        lane0 = (c % 2) * FF_CHUNK
        act_bufs[(c // 2) % 2][r0:r0 + half, lane0:lane0 + FF_CHUNK] = (
            gate * jax.nn.sigmoid(gate) * val).astype(BF16)

    def down_proj(p):
        k0 = 2 * p * FF_CHUNK
        width = min(2 * FF_CHUNK, D_FF - k0)
        part = jnp.dot(act_bufs[p % 2][:, 0:width], wd_ref[k0:k0 + width, :], preferred_element_type=F32)
        if p == 0:
            acc_ref[...] = part
        else:
            acc_ref[...] += part

    npairs = (nchunks + 1) // 2
    up_proj(0, 0)
    up_proj(0, 1)
    for c in range(nchunks):
        if c + 1 < nchunks:
            up_proj(c + 1, 0)
        activate(c, 0)
        if c + 1 < nchunks:
            up_proj(c + 1, 1)
        activate(c, half)
        if c >= 2 and c % 2 == 0:
            down_proj(c // 2 - 1)
    down_proj(npairs - 1)
    out_ref[...] = x_ref[...] + acc_ref[...]


def _ffn(x2d, g, wup, cw, cb, wd, *, seq, tm):
    t_total = x2d.shape[0]
    tm = min(tm, seq)
    r = tm // FF_HALO
    last = t_total // FF_HALO - 1
    const = lambda i: (0, 0)
    resident = dict(pipeline_mode=pl.Buffered(1))
    kern = functools.partial(_ffn_kernel, seq=seq, tm=tm)
    return pl.pallas_call(
        kern,
        grid=(t_total // tm,),
        in_specs=[
            pl.BlockSpec((FF_HALO, D_MODEL), lambda i: (jnp.maximum(i * r - 1, 0), 0)),
            pl.BlockSpec((tm, D_MODEL), lambda i: (i, 0)),
            pl.BlockSpec((FF_HALO, D_MODEL), lambda i: (jnp.minimum((i + 1) * r, last), 0)),
            pl.BlockSpec((1, D_MODEL), const),
            pl.BlockSpec((D_MODEL, 2 * D_FF), const, **resident),
            pl.BlockSpec((3, 2 * D_FF), const),
            pl.BlockSpec((1, 2 * D_FF), const),
            pl.BlockSpec((D_FF, D_MODEL), const, **resident),
        ],
        out_specs=pl.BlockSpec((tm, D_MODEL), lambda i: (i, 0)),
        out_shape=jax.ShapeDtypeStruct((t_total, D_MODEL), F32),
        scratch_shapes=[
            pltpu.VMEM((tm + 2 * FF_HALO, D_MODEL), BF16),
            pltpu.VMEM((tm, D_MODEL), F32),
        ] + [pltpu.VMEM((tm + 2 * FF_HALO, FF_CHUNK), F32)] * 4 + [pltpu.VMEM((tm, 2 * FF_CHUNK), BF16)] * 2,
        compiler_params=_params("parallel"),
        name="ffn",
    )(x2d, x2d, x2d, g, wup, cw, cb, wd)


def _rope_tables(pos, dim, theta):
    inv = theta ** (-jnp.arange(0, dim, 2, dtype=F32) / dim)
    ang = pos.astype(F32)[:, None] * inv[None, :]
    return jnp.cos(ang).T, jnp.sin(ang).T


def _class_tiles(table, dil, n):
    rows, seq = table.shape
    return table.reshape(rows, seq // (dil * n), n, dil).transpose(1, 0, 3, 2).reshape(-1, rows, dil * n)


def _lane_bcast(v, width):
    return jnp.broadcast_to(v.astype(F32)[:, None], (v.shape[0], width))


TM = 512
TQ_C = 512
TQ_B = 1024
TKC = 512
TQ_BAND = 512
A_CLASS_TOKENS = (512, 512, 128)
Q_SCALE = HEAD_DIM ** -0.5 * LOG2E


def _trunk(x, norm_mix, norm_ffn, w_in_ab, q_norm_a, k_norm_a, q_norm_b, k_norm_b,
           lambda_q1, lambda_k1, lambda_q2, lambda_k2, subln_b, w_out_ab,
           w_in_c, q_norm_c, k_norm_c, w_out_c, w_up, conv_w, conv_b, w_down):
    batch, seq, _ = x.shape
    t_total = batch * seq
    depth = norm_mix.shape[0]
    x2d = x.reshape(t_total, D_MODEL)

    cos, sin = _rope_tables(jnp.arange(seq), ROT_DIM, ROPE_THETA)
    rows = seq // GRID_W
    row = jnp.repeat(jnp.arange(rows), GRID_W)
    col = jnp.tile(jnp.arange(GRID_W), rows)
    cr, sr = _rope_tables(row, AXIAL_DIM, AXIAL_THETA)
    cc, sc = _rope_tables(col, AXIAL_DIM, AXIAL_THETA)
    cos_ax = jnp.concatenate([cr, cc], axis=0)
    sin_ax = jnp.concatenate([sr, sc], axis=0)

    for i in range(depth):
        j = i // 2
        g_mix = norm_mix[i].reshape(1, D_MODEL)
        if i % 2 == 0:
            lam_init = 0.8 - 0.6 * math.exp(-0.3 * i)
            w = w_in_ab[j]
            oa, lse = [], []
            for gi, (_, dil) in enumerate(A_PATTERNS):
                n_cls = min(A_CLASS_TOKENS[gi], seq // dil)
                width = dil * n_cls
                gains_a = jnp.concatenate([_lane_bcast(q_norm_a[j] * Q_SCALE, width),
                                           _lane_bcast(k_norm_a[j], width)], 0)
                sl = slice(gi * A_GROUP_W, (gi + 1) * A_GROUP_W)
                wt = jnp.concatenate([w[:, sl], w[:, A_QKV:2 * A_QKV][:, sl], w[:, 2 * A_QKV:3 * A_QKV][:, sl]],
                                     axis=1).T.astype(BF16)
                qt, k, vt = _proj_cls(x2d, g_mix, wt, gains_a, _class_tiles(cos, dil, n_cls),
                                      _class_tiles(sin, dil, n_cls), batch=batch, seq=seq, dil=dil, n=n_cls)
                o_g, lse_g = _band(qt, k, vt, batch=batch, seq=seq, dil=dil, tqs=TQ_BAND)
                oa.append(o_g)
                lse.append(lse_g)
            wt_b = w[:, 3 * A_QKV:].T.astype(BF16)
            gains_b = jnp.concatenate([_lane_bcast(q_norm_b[j] * Q_SCALE, TM), _lane_bcast(k_norm_b[j], TM)], 0)
            qt, k, vt = _proj(x2d, g_mix, wt_b, gains_b, cos, sin, seq=seq,
                              nq=2 * B_HEADS, nk=2 * B_HEADS, nv=B_V, axial=False, tm=TM)
            lam = (jnp.exp(jnp.sum(lambda_q1[j].astype(F32) * lambda_k1[j].astype(F32)))
                   - jnp.exp(jnp.sum(lambda_q2[j].astype(F32) * lambda_k2[j].astype(F32))) + lam_init)
            sg = _lane_bcast(subln_b[j] * (1.0 - lam_init), TQ_B)
            obt = _attn_b(lam.reshape(1).astype(F32), qt, k, vt, sg, batch=batch, seq=seq, tq=TQ_B, tkc=TKC)
            wo = w_out_ab[j].astype(BF16)
            x2d = _outproj_even(x2d, oa, lse, obt, wo[:A_GROUP_W], wo[A_GROUP_W:], seq=seq, tm=TM)
        else:
            wt = w_in_c[j].T.astype(BF16)
            gains_c = jnp.concatenate([_lane_bcast(q_norm_c[j] * Q_SCALE, TM), _lane_bcast(k_norm_c[j], TM)], 0)
            qt, k, vt = _proj(x2d, g_mix, wt, gains_c, cos_ax, sin_ax, seq=seq,
                              nq=C_Q_HEADS, nk=C_KV_HEADS, nv=C_KV_HEADS * HEAD_DIM, axial=True, tm=TM)
            ot = _attn_c(qt, k, vt, batch=batch, seq=seq, tq=TQ_C, tkc=TKC)
            x2d = _outproj_odd(x2d, ot, w_out_c[j].astype(BF16), tm=TM)
        x2d = _ffn(x2d, norm_ffn[i].reshape(1, D_MODEL), w_up[i].astype(BF16), conv_w[i],
                   conv_b[i].reshape(1, 2 * D_FF), w_down[i].astype(BF16), seq=seq, tm=TM)
    return x2d.reshape(batch, seq, D_MODEL)


def kernel(x_prompt, x_sample, norm_mix, norm_ffn, w_in_ab, q_norm_a, k_norm_a, q_norm_b, k_norm_b,
           lambda_q1, lambda_k1, lambda_q2, lambda_k2, subln_b, w_out_ab, w_in_c, q_norm_c, k_norm_c,
           w_out_c, w_up, conv_w, conv_b, w_down):
    params = (norm_mix, norm_ffn, w_in_ab, q_norm_a, k_norm_a, q_norm_b, k_norm_b,
              lambda_q1, lambda_k1, lambda_q2, lambda_k2, subln_b, w_out_ab,
              w_in_c, q_norm_c, k_norm_c, w_out_c, w_up, conv_w, conv_b, w_down)
    return (_trunk(x_prompt, *params), _trunk(x_sample, *params))
```

```python
import functools
import math

import jax
import jax.numpy as jnp
from jax import lax
from jax.experimental import pallas as pl
from jax.experimental.pallas import tpu as pltpu

F32 = jnp.float32
BF16 = jnp.bfloat16

D_MODEL = 1024
HEAD_DIM = 64
A_PATTERNS = ((128, 1), (512, 4), (2048, 16))
A_GROUPS = 3
A_HEADS_PER_GROUP = 4
A_QKV = A_GROUPS * A_HEADS_PER_GROUP * HEAD_DIM
A_GROUP_W = A_HEADS_PER_GROUP * HEAD_DIM
B_HEADS = 4
B_QK = B_HEADS * 2 * HEAD_DIM
B_V = B_HEADS * 2 * HEAD_DIM
C_Q_HEADS = 16
C_KV_HEADS = 4
C_GROUP = C_Q_HEADS // C_KV_HEADS
ROPE_THETA = 500000.0
ROT_DIM = HEAD_DIM // 4
AXIAL_THETA = 10000.0
AXIAL_DIM = HEAD_DIM // 2
GRID_W = 64
D_FF = 2816
NORM_EPS = 1e-6
SUBLN_EPS = 1e-5
NEG_INF = -1e30
LOG2E = 1.4426950408889634
LN2 = 0.6931471805599453

V7X_LANES = 128
V7X_VMEM_BYTES = 64 * 1024 * 1024
VMEM_LIMIT = V7X_VMEM_BYTES - 8 * 1024 * 1024
BAND_HALF = 64
BAND_SUB = 128
BAND_WIN = BAND_SUB + 2 * V7X_LANES
FF_CHUNK = 256
FF_DOWN_GROUP = 4


def _params(*semantics):
    return pltpu.CompilerParams(dimension_semantics=semantics, vmem_limit_bytes=VMEM_LIMIT)


def _rms_rows(x, g):
    ms = jnp.mean(x * x, axis=-1, keepdims=True)
    return x * lax.rsqrt(ms + NORM_EPS) * g


def _head_norm_rot(y, gain, cos, sin, axial):
    ssq = jnp.sum(y * y, axis=0, keepdims=True)
    yn = y * lax.rsqrt(ssq * (1.0 / HEAD_DIM) + NORM_EPS) * gain
    if axial:
        h = AXIAL_DIM // 2
        a1, a2, b1, b2 = yn[0:h], yn[h:2 * h], yn[2 * h:3 * h], yn[3 * h:4 * h]
        cr, cc = cos[0:h], cos[h:2 * h]
        sr, sc = sin[0:h], sin[h:2 * h]
        return jnp.concatenate(
            [a1 * cr - a2 * sr, a2 * cr + a1 * sr, b1 * cc - b2 * sc, b2 * cc + b1 * sc], axis=0)
    h = ROT_DIM // 2
    x1, x2 = yn[0:h], yn[h:2 * h]
    return jnp.concatenate([x1 * cos - x2 * sin, x2 * cos + x1 * sin, yn[2 * h:]], axis=0)


def _proj_kernel(x_ref, g_ref, wt_ref, gain_ref, cos_ref, sin_ref, qt_ref, k_ref, vt_ref,
                 *, nq, nk, axial):
    h = _rms_rows(x_ref[...], g_ref[...]).astype(BF16)
    yt = lax.dot_general(wt_ref[...], h, (((1,), (1,)), ((), ())),
                         preferred_element_type=F32)
    cos = cos_ref[...]
    sin = sin_ref[...]
    gq = gain_ref[0:HEAD_DIM, :]
    gk = gain_ref[HEAD_DIM:2 * HEAD_DIM, :]
    for i in range(nq):
        y = yt[i * HEAD_DIM:(i + 1) * HEAD_DIM, :]
        qt_ref[i * HEAD_DIM:(i + 1) * HEAD_DIM, :] = _head_norm_rot(y, gq, cos, sin, axial).astype(BF16)
    base = nq * HEAD_DIM
    for p in range(nk // 2):
        pair = []
        for i in (2 * p, 2 * p + 1):
            y = yt[base + i * HEAD_DIM:base + (i + 1) * HEAD_DIM, :]
            pair.append(_head_norm_rot(y, gk, cos, sin, axial))
        kt = jnp.concatenate(pair, axis=0)
        k_ref[:, p * 2 * HEAD_DIM:(p + 1) * 2 * HEAD_DIM] = kt.T.astype(BF16)
    base = (nq + nk) * HEAD_DIM
    vt_ref[...] = yt[base:, :].astype(BF16)


def _proj(x2d, g, wt, gains, cos_t, sin_t, *, seq, nq, nk, nv, axial, tm):
    t_total = x2d.shape[0]
    tm = min(tm, seq)
    nlb = seq // tm
    n_out = (nq + nk) * HEAD_DIM + nv
    rot_rows = cos_t.shape[0]
    kern = functools.partial(_proj_kernel, nq=nq, nk=nk, axial=axial)
    return pl.pallas_call(
        kern,
        grid=(t_total // tm,),
        in_specs=[
            pl.BlockSpec((tm, D_MODEL), lambda i: (i, 0)),
            pl.BlockSpec((1, D_MODEL), lambda i: (0, 0)),
            pl.BlockSpec((n_out, D_MODEL), lambda i: (0, 0)),
            pl.BlockSpec((2 * HEAD_DIM, tm), lambda i: (0, 0)),
            pl.BlockSpec((rot_rows, tm), lambda i: (0, i % nlb)),
            pl.BlockSpec((rot_rows, tm), lambda i: (0, i % nlb)),
        ],
        out_specs=[
            pl.BlockSpec((nq * HEAD_DIM, tm), lambda i: (0, i)),
            pl.BlockSpec((tm, nk * HEAD_DIM), lambda i: (i, 0)),
            pl.BlockSpec((nv, tm), lambda i: (0, i)),
        ],
        out_shape=[
            jax.ShapeDtypeStruct((nq * HEAD_DIM, t_total), BF16),
            jax.ShapeDtypeStruct((t_total, nk * HEAD_DIM), BF16),
            jax.ShapeDtypeStruct((nv, t_total), BF16),
        ],
        compiler_params=_params("parallel"),
        name="proj",
    )(x2d, g, wt, gains[:, :tm], cos_t, sin_t)


def _proj_cls_kernel(x_ref, g_ref, wt_ref, gain_ref, cos_ref, sin_ref, qt_ref, k_ref, vt_ref, hs_ref,
                     stage_ref, *, dil, n):
    h = _rms_rows(x_ref[...], g_ref[...])
    if dil == 1:
        hs_ref[...] = h.astype(BF16)
    else:
        for c in range(D_MODEL // V7X_LANES):
            stage_ref[c] = h[:, c * V7X_LANES:(c + 1) * V7X_LANES]
        for r in range(dil):
            for c in range(D_MODEL // V7X_LANES):
                hs_ref[r * n:(r + 1) * n, c * V7X_LANES:(c + 1) * V7X_LANES] = (
                    stage_ref[c, pl.ds(r, n, stride=dil), :].astype(BF16))
    yt = lax.dot_general(wt_ref[...], hs_ref[...], (((1,), (1,)), ((), ())),
                         preferred_element_type=F32)
    cos = cos_ref[...]
    sin = sin_ref[...]
    gq = gain_ref[0:HEAD_DIM, :]
    gk = gain_ref[HEAD_DIM:2 * HEAD_DIM, :]
    nh = A_HEADS_PER_GROUP
    for i in range(nh):
        q = _head_norm_rot(yt[i * HEAD_DIM:(i + 1) * HEAD_DIM, :], gq, cos, sin, False).astype(BF16)
        for r in range(dil):
            qt_ref[r, i * HEAD_DIM:(i + 1) * HEAD_DIM, :] = q[:, r * n:(r + 1) * n]
    base = nh * HEAD_DIM
    for p in range(nh // 2):
        pair = [_head_norm_rot(yt[base + i * HEAD_DIM:base + (i + 1) * HEAD_DIM, :], gk, cos, sin, False)
                for i in (2 * p, 2 * p + 1)]
        kt = jnp.concatenate(pair, axis=0).T.astype(BF16)
        for r in range(dil):
            k_ref[r, :, p * 2 * HEAD_DIM:(p + 1) * 2 * HEAD_DIM] = kt[r * n:(r + 1) * n, :]
    base = 2 * nh * HEAD_DIM
    v = yt[base:, :].astype(BF16)
    for r in range(dil):
        vt_ref[r] = v[:, r * n:(r + 1) * n]


def _proj_cls(x2d, g, wt, gains, cos_c, sin_c, *, batch, seq, dil, n):
    t_total = x2d.shape[0]
    tmx = dil * n
    length = seq // dil
    tps = seq // tmx
    kern = functools.partial(_proj_cls_kernel, dil=dil, n=n)
    feat = pl.BlockSpec((dil, A_GROUP_W, n), lambda i: (i // tps, 0, i % tps))
    return pl.pallas_call(
        kern,
        grid=(t_total // tmx,),
        in_specs=[
            pl.BlockSpec((tmx, D_MODEL), lambda i: (i, 0)),
            pl.BlockSpec((1, D_MODEL), lambda i: (0, 0)),
            pl.BlockSpec((3 * A_GROUP_W, D_MODEL), lambda i: (0, 0)),
            pl.BlockSpec((2 * HEAD_DIM, tmx), lambda i: (0, 0)),
            pl.BlockSpec((None, ROT_DIM // 2, tmx), lambda i: (i % tps, 0, 0)),
            pl.BlockSpec((None, ROT_DIM // 2, tmx), lambda i: (i % tps, 0, 0)),
        ],
        out_specs=[feat, pl.BlockSpec((dil, n, A_GROUP_W), lambda i: (i // tps, i % tps, 0)), feat],
        out_shape=[
            jax.ShapeDtypeStruct((batch * dil, A_GROUP_W, length), BF16),
            jax.ShapeDtypeStruct((batch * dil, length, A_GROUP_W), BF16),
            jax.ShapeDtypeStruct((batch * dil, A_GROUP_W, length), BF16),
        ],
        scratch_shapes=[pltpu.VMEM((tmx, D_MODEL), BF16),
                        pltpu.VMEM((D_MODEL // V7X_LANES, tmx if dil > 1 else 8, V7X_LANES), F32)],
        compiler_params=_params("parallel"),
        name="proj_cls",
    )(x2d, g, wt, gains, cos_c, sin_c)


FLASH_UNROLL = 4
FLASH_COL_TILE = 256
V_PAD = 16


def _flash_scratch(dv, ncols, seq, tkc):
    return [
        pltpu.VMEM((2, 2 * HEAD_DIM, ncols), BF16),
        pltpu.VMEM((dv + V_PAD, seq), BF16),
        pltpu.VMEM((dv + V_PAD, ncols), F32),
        pltpu.VMEM((1, ncols), F32),
        pltpu.VMEM((tkc, ncols), F32), pltpu.VMEM((tkc, ncols), F32),
        pltpu.VMEM((1, ncols), F32), pltpu.VMEM((1, ncols), F32),
    ]


def _flash_chain(k_ref, vt_ref, scratch, build_qpad, epilogue, *, dv, seq, tq, tkc):
    qpad_ref, vext_ref, acc_ref, m_ref, s0, s1, c0, c1 = scratch
    s_bufs, cm_bufs = (s0, s1), (c0, c1)
    nc = seq // tkc
    nqb = seq // tq
    unroll = FLASH_UNROLL if nc % FLASH_UNROLL == 0 else nc
    ncols = qpad_ref.shape[2]
    tiles = [slice(j * FLASH_COL_TILE, (j + 1) * FLASH_COL_TILE) for j in range(ncols // FLASH_COL_TILE)]

    vext_ref[0:dv, :] = vt_ref[...]
    ones_row = lax.broadcasted_iota(jnp.int32, (V_PAD, seq), 0) == 0
    vext_ref[dv:, :] = jnp.where(ones_row, 1.0, 0.0).astype(BF16)

    def chunk(c):
        return pl.ds(c * tkc if isinstance(c, int) else pl.multiple_of(c * tkc, tkc), tkc)

    def scores(qslot, c, slot, t):
        s = jnp.dot(k_ref[chunk(c), :], qpad_ref[qslot, :, t], preferred_element_type=F32)
        s_bufs[slot][:, t] = s
        cm_bufs[slot][:, t] = jnp.max(s, axis=0, keepdims=True)

    def softmax_pv(c, slot, t, first=False):
        cm = cm_bufs[slot][:, t]
        if first:
            m_new = cm
        else:
            m_old = m_ref[:, t]
            m_new = jnp.maximum(m_old, cm)
        m_ref[:, t] = m_new
        p = jnp.exp2((s_bufs[slot][:, t] - m_new).astype(BF16))
        pv = jnp.dot(vext_ref[:, chunk(c)], p, preferred_element_type=F32)
        acc_ref[:, t] = pv if first else jnp.exp2(m_old - m_new) * acc_ref[:, t] + pv

    def step(c, slot, nxt=None, first=False):
        for t in tiles:
            if nxt is not None:
                scores(nxt[0], nxt[1], 1 - slot, t)
            softmax_pv(c, slot, t, first)

    def block_steps(qslot, after_last):
        if nc > unroll:
            step(0, 0, nxt=(qslot, 1), first=True)
            for c in range(1, unroll):
                step(c, c % 2, nxt=(qslot, c + 1))

            def group(g, inner):
                for u in range(unroll):
                    step(unroll * g + u, u % 2, nxt=(qslot, unroll * g + u + 1))
                return inner

            lax.fori_loop(1, nc // unroll - 1, group, 0)
        tail = unroll if nc > unroll else nc
        for c in range(nc - tail, nc):
            step(c, c % 2, nxt=(qslot, c + 1) if c + 1 < nc else after_last, first=(c == 0))

    if nqb % 2 == 0:
        build_qpad(0, 0)
        for t in tiles:
            scores(0, 0, 0, t)

        def pair(i, carry):
            for par in (0, 1):
                qb = 2 * i + par
                build_qpad(jnp.minimum(qb + 1, nqb - 1), 1 - par)
                block_steps(par, (1 - par, 0))
                epilogue(qb)
            return carry

        lax.fori_loop(0, nqb // 2, pair, 0)
    else:
        def block(qb, carry):
            build_qpad(qb, 0)
            for t in tiles:
                scores(0, 0, 0, t)
            block_steps(0, None)
            epilogue(qb)
            return carry

        lax.fori_loop(0, nqb, block, 0)


def _attn_c_kernel(qt_ref, k_ref, vt_ref, o_ref, *scratch, seq, tq, tkc):
    qpad_ref, acc_ref = scratch[0], scratch[2]
    upper = (pl.program_id(1) % 2).astype(F32)

    def col(qb):
        return pl.ds(pl.multiple_of(qb * tq, tq), tq)

    def build_qpad(qb, slot):
        qcat = jnp.concatenate([qt_ref[g * HEAD_DIM:(g + 1) * HEAD_DIM, col(qb)] for g in range(C_GROUP)],
                               axis=1).astype(F32)
        qpad_ref[slot, 0:HEAD_DIM, :] = (qcat * (1.0 - upper)).astype(BF16)
        qpad_ref[slot, HEAD_DIM:, :] = (qcat * upper).astype(BF16)

    def epilogue(qb):
        o = acc_ref[0:HEAD_DIM, :] * (1.0 / acc_ref[HEAD_DIM:HEAD_DIM + 1, :])
        for g in range(C_GROUP):
            o_ref[g * HEAD_DIM:(g + 1) * HEAD_DIM, col(qb)] = o[:, g * tq:(g + 1) * tq].astype(BF16)

    _flash_chain(k_ref, vt_ref, scratch, build_qpad, epilogue, dv=HEAD_DIM, seq=seq, tq=tq, tkc=tkc)


def _attn_c(qt, k, vt, *, batch, seq, tq, tkc):
    t_total = batch * seq
    tq = min(tq, seq)
    tkc = min(tkc, seq)
    kern = functools.partial(_attn_c_kernel, seq=seq, tq=tq, tkc=tkc)
    return pl.pallas_call(
        kern,
        grid=(batch, C_KV_HEADS),
        in_specs=[
            pl.BlockSpec((C_GROUP * HEAD_DIM, seq), lambda b, n: (n, b)),
            pl.BlockSpec((seq, 2 * HEAD_DIM), lambda b, n: (b, n // 2)),
            pl.BlockSpec((HEAD_DIM, seq), lambda b, n: (n, b)),
        ],
        out_specs=pl.BlockSpec((C_GROUP * HEAD_DIM, seq), lambda b, n: (n, b)),
        out_shape=jax.ShapeDtypeStruct((C_Q_HEADS * HEAD_DIM, t_total), BF16),
        scratch_shapes=_flash_scratch(HEAD_DIM, C_GROUP * tq, seq, tkc),
        compiler_params=_params("parallel", "parallel"),
        name="attn_c",
    )(qt, k, vt)


def _attn_b_kernel(lam_ref, qt_ref, k_ref, vt_ref, sg_ref, o_ref, *scratch, seq, tq, tkc):
    dv = 2 * HEAD_DIM
    qpad_ref, acc_ref = scratch[0], scratch[2]
    zeros = jnp.zeros((HEAD_DIM, tq), BF16)

    def col(qb):
        return pl.ds(pl.multiple_of(qb * tq, tq), tq)

    def build_qpad(qb, slot):
        q = qt_ref[:, col(qb)]
        qpad_ref[slot, 0:HEAD_DIM, 0:tq] = q[0:HEAD_DIM]
        qpad_ref[slot, 0:HEAD_DIM, tq:] = zeros
        qpad_ref[slot, HEAD_DIM:, 0:tq] = zeros
        qpad_ref[slot, HEAD_DIM:, tq:] = q[HEAD_DIM:]

    def epilogue(qb):
        on = acc_ref[0:dv, :] * (1.0 / acc_ref[dv:dv + 1, :])
        o = on[:, 0:tq] - lam_ref[0] * on[:, tq:]
        ms = jnp.mean(o * o, axis=0, keepdims=True)
        o_ref[:, col(qb)] = (o * lax.rsqrt(ms + SUBLN_EPS) * sg_ref[...]).astype(BF16)

    _flash_chain(k_ref, vt_ref, scratch, build_qpad, epilogue, dv=dv, seq=seq, tq=tq, tkc=tkc)


def _attn_b(lam, qt, k, vt, sg, *, batch, seq, tq, tkc):
    t_total = batch * seq
    tq = min(tq, seq)
    tkc = min(tkc, seq)
    kern = functools.partial(_attn_b_kernel, seq=seq, tq=tq, tkc=tkc)
    return pl.pallas_call(
        kern,
        grid=(batch, B_HEADS),
        in_specs=[
            pl.BlockSpec(memory_space=pltpu.SMEM),
            pl.BlockSpec((2 * HEAD_DIM, seq), lambda b, h: (h, b)),
            pl.BlockSpec((seq, 2 * HEAD_DIM), lambda b, h: (b, h)),
            pl.BlockSpec((2 * HEAD_DIM, seq), lambda b, h: (h, b)),
            pl.BlockSpec((2 * HEAD_DIM, tq), lambda b, h: (0, 0)),
        ],
        out_specs=pl.BlockSpec((2 * HEAD_DIM, seq), lambda b, h: (h, b)),
        out_shape=jax.ShapeDtypeStruct((B_V, t_total), BF16),
        scratch_shapes=_flash_scratch(2 * HEAD_DIM, 2 * tq, seq, tkc),
        compiler_params=_params("parallel", "parallel"),
        name="attn_b",
    )(lam, qt, k, vt, sg[:, :tq])


def _band_kernel(qt_ref, kp_ref, km_ref, kn_ref, vp_ref, vm_ref, vn_ref, o_ref, lse_ref, *, length, tqs):
    l0 = pl.program_id(1) * tqs
    kwin = jnp.concatenate([kp_ref[...], km_ref[...], kn_ref[...]], axis=0)
    vwin = jnp.concatenate([vp_ref[...], vm_ref[...], vn_ref[...]], axis=1)
    row = lax.broadcasted_iota(jnp.int32, (BAND_WIN, BAND_SUB), 0)
    col = lax.broadcasted_iota(jnp.int32, (BAND_WIN, BAND_SUB), 1)
    rel = row - V7X_LANES - col
    in_band = jnp.abs(rel) <= BAND_HALF
    zeros = jnp.zeros((HEAD_DIM, BAND_SUB), BF16)
    npair = A_HEADS_PER_GROUP // 2
    units = [(j, p) for j in range(tqs // BAND_SUB) for p in range(npair)]
    scores = {}
    for j, p in units:
        c0 = j * BAND_SUB
        q = qt_ref[:, c0:c0 + BAND_SUB]
        qa = q[2 * p * HEAD_DIM:(2 * p + 1) * HEAD_DIM]
        qb = q[(2 * p + 1) * HEAD_DIM:(2 * p + 2) * HEAD_DIM]
        qpad = jnp.concatenate([jnp.concatenate([qa, zeros], axis=1),
                                jnp.concatenate([zeros, qb], axis=1)], axis=0)
        s = jnp.dot(kwin[c0:c0 + BAND_WIN, p * 2 * HEAD_DIM:(p + 1) * 2 * HEAD_DIM], qpad,
                    preferred_element_type=F32)
        kpos = l0 + (c0 - V7X_LANES) + row
        valid = in_band & (kpos >= 0) & (kpos < length)
        scores[j, p] = jnp.where(jnp.concatenate([valid, valid], axis=1), s, NEG_INF)
    probs = {}
    for u in units:
        m = jnp.max(scores[u], axis=0, keepdims=True)
        pr = jnp.exp2(scores[u] - m)
        probs[u] = (m, jnp.sum(pr, axis=0, keepdims=True), pr.astype(BF16))
    outs = {}
    for j, p in units:
        c0 = j * BAND_SUB
        m, l, pb = probs[j, p]
        va = vwin[2 * p * HEAD_DIM:(2 * p + 1) * HEAD_DIM, c0:c0 + BAND_WIN]
        vb = vwin[(2 * p + 1) * HEAD_DIM:(2 * p + 2) * HEAD_DIM, c0:c0 + BAND_WIN]
        oa = jnp.dot(va, pb[:, :BAND_SUB], preferred_element_type=F32)
        ob = jnp.dot(vb, pb[:, BAND_SUB:], preferred_element_type=F32)
        inv = 1.0 / l
        lse = m * LN2 + jnp.log(l)
        outs[j, p] = ([oa * inv[:, :BAND_SUB], ob * inv[:, BAND_SUB:]],
                      [jnp.broadcast_to(lse[:, :BAND_SUB], (HEAD_DIM, BAND_SUB)),
                       jnp.broadcast_to(lse[:, BAND_SUB:], (HEAD_DIM, BAND_SUB))])
    for j in range(tqs // BAND_SUB):
        c0 = j * BAND_SUB
        o_ref[c0:c0 + BAND_SUB, :] = jnp.concatenate([x for p in range(npair) for x in outs[j, p][0]], axis=0).T
        lse_ref[c0:c0 + BAND_SUB, :] = jnp.concatenate([x for p in range(npair) for x in outs[j, p][1]], axis=0).T


def _band(qt, k, vt, *, batch, seq, dil, tqs):
    length = seq // dil
    tqs = min(tqs, length)
    nqb = length // tqs
    ncls = batch * dil
    nkb = length // V7X_LANES
    r128 = tqs // V7X_LANES

    def prev_blk(i):
        return jnp.maximum(i * r128 - 1, 0)

    def next_blk(i):
        return jnp.minimum((i + 1) * r128, nkb - 1)

    kern = functools.partial(_band_kernel, length=length, tqs=tqs)
    out_spec = pl.BlockSpec((None, tqs, A_GROUP_W), lambda c, i: (c, i, 0))
    return pl.pallas_call(
        kern,
        grid=(ncls, nqb),
        in_specs=[
            pl.BlockSpec((None, A_GROUP_W, tqs), lambda c, i: (c, 0, i)),
            pl.BlockSpec((None, V7X_LANES, A_GROUP_W), lambda c, i: (c, prev_blk(i), 0)),
            pl.BlockSpec((None, tqs, A_GROUP_W), lambda c, i: (c, i, 0)),
            pl.BlockSpec((None, V7X_LANES, A_GROUP_W), lambda c, i: (c, next_blk(i), 0)),
            pl.BlockSpec((None, A_GROUP_W, V7X_LANES), lambda c, i: (c, 0, prev_blk(i))),
            pl.BlockSpec((None, A_GROUP_W, tqs), lambda c, i: (c, 0, i)),
            pl.BlockSpec((None, A_GROUP_W, V7X_LANES), lambda c, i: (c, 0, next_blk(i))),
        ],
        out_specs=[out_spec, out_spec],
        out_shape=[jax.ShapeDtypeStruct((ncls, length, A_GROUP_W), F32)] * 2,
        compiler_params=_params("parallel", "parallel"),
        name="band",
    )(qt, k, k, k, vt, vt, vt)


def _outproj_even_kernel(x_ref, o0_ref, o1_ref, o2_ref, l0_ref, l1_ref, l2_ref, obt_ref, wa_ref, wb_ref,
                         out_ref, *bufs, tm):
    def natural(blk_ref, buf_ref, dil):
        if dil == 1:
            return blk_ref[0]
        ntile = A_GROUP_W // V7X_LANES
        for r in range(dil):
            for c in range(ntile):
                buf_ref[c, pl.ds(r, tm // dil, stride=dil), :] = blk_ref[r, :, c * V7X_LANES:(c + 1) * V7X_LANES]
        return jnp.concatenate([buf_ref[c] for c in range(ntile)], axis=1)

    dils = [d for _, d in A_PATTERNS]
    o = [natural(ref, bufs[2 * gi], d) for gi, (ref, d) in enumerate(zip((o0_ref, o1_ref, o2_ref), dils))]
    l = [natural(ref, bufs[2 * gi + 1], d) for gi, (ref, d) in enumerate(zip((l0_ref, l1_ref, l2_ref), dils))]
    m = jnp.maximum(jnp.maximum(l[0], l[1]), l[2])
    w0, w1, w2 = jnp.exp(l[0] - m), jnp.exp(l[1] - m), jnp.exp(l[2] - m)
    oa = (w0 * o[0] + w1 * o[1] + w2 * o[2]) / (w0 + w1 + w2)
    acc = jnp.dot(oa.astype(BF16), wa_ref[...], preferred_element_type=F32)
    acc += lax.dot_general(obt_ref[...], wb_ref[...], (((0,), (0,)), ((), ())),
                           preferred_element_type=F32)
    out_ref[...] = x_ref[...] + acc


def _outproj_even(x2d, oa, lse, obt, wa, wb, *, seq, tm):
    t_total = x2d.shape[0]
    tps = seq // tm
    row = lambda i: (i, 0)
    const = lambda i: (0, 0)
    a_specs = [pl.BlockSpec((d, tm // d, A_GROUP_W), lambda i: (i // tps, i % tps, 0)) for _, d in A_PATTERNS]
    return pl.pallas_call(
        functools.partial(_outproj_even_kernel, tm=tm),
        grid=(t_total // tm,),
        in_specs=[pl.BlockSpec((tm, D_MODEL), row)] + a_specs * 2 + [
            pl.BlockSpec((B_V, tm), lambda i: (0, i)),
            pl.BlockSpec((A_GROUP_W, D_MODEL), const),
            pl.BlockSpec((B_V, D_MODEL), const),
        ],
        out_specs=pl.BlockSpec((tm, D_MODEL), row),
        out_shape=jax.ShapeDtypeStruct((t_total, D_MODEL), F32),
        scratch_shapes=[pltpu.VMEM((A_GROUP_W // V7X_LANES, tm, V7X_LANES), F32)] * (2 * A_GROUPS),
        compiler_params=_params("parallel"),
        name="outproj_even",
    )(x2d, *oa, *lse, obt, wa, wb)


def _outproj_odd_kernel(x_ref, ot_ref, w_ref, out_ref):
    acc = lax.dot_general(ot_ref[...], w_ref[...], (((0,), (0,)), ((), ())), preferred_element_type=F32)
    out_ref[...] = x_ref[...] + acc


def _outproj_odd(x2d, ot, w, *, tm):
    t_total = x2d.shape[0]
    n_in = ot.shape[0]
    return pl.pallas_call(
        _outproj_odd_kernel,
        grid=(t_total // tm,),
        in_specs=[
            pl.BlockSpec((tm, D_MODEL), lambda i: (i, 0)),
            pl.BlockSpec((n_in, tm), lambda i: (0, i)),
            pl.BlockSpec((n_in, D_MODEL), lambda i: (0, 0)),
        ],
        out_specs=pl.BlockSpec((tm, D_MODEL), lambda i: (i, 0)),
        out_shape=jax.ShapeDtypeStruct((t_total, D_MODEL), F32),
        compiler_params=_params("parallel"),
        name="outproj_odd",
    )(x2d, ot, w)


FF_HALO = 16


def _ffn_kernel(xp_ref, x_ref, xn_ref, g_ref, wup_ref, cw_ref, cb_ref, wd_ref, out_ref, hs_ref, acc_ref,
                ug0_ref, ug1_ref, uv0_ref, uv1_ref, act0_ref, act1_ref, *, seq, tm):
    i = pl.program_id(0)
    g = g_ref[...]
    has_prev = ((i * tm) % seq != 0).astype(F32)
    has_next = (((i + 1) * tm) % seq != 0).astype(F32)
    hs_ref[0:FF_HALO, :] = (_rms_rows(xp_ref[...], g) * has_prev).astype(BF16)
    hs_ref[FF_HALO:FF_HALO + tm, :] = _rms_rows(x_ref[...], g).astype(BF16)
    hs_ref[FF_HALO + tm:, :] = (_rms_rows(xn_ref[...], g) * has_next).astype(BF16)
    rows = tm + 2 * FF_HALO
    u_bufs = ((ug0_ref, uv0_ref), (ug1_ref, uv1_ref))
    nchunks = D_FF // FF_CHUNK

    def up_proj(c, which):
        u_ref = u_bufs[c % 2][which]
        c0 = which * D_FF + c * FF_CHUNK
        u_ref[...] = jnp.dot(hs_ref[...], wup_ref[:, c0:c0 + FF_CHUNK], preferred_element_type=F32)

    def conv(u_ref, c0, r0, nr):
        u = u_ref[r0:r0 + nr + 2 * FF_HALO, :]
        w = cw_ref[:, c0:c0 + FF_CHUNK]
        b = cb_ref[:, c0:c0 + FF_CHUNK]
        um = pltpu.roll(u, 1, 0)[FF_HALO:FF_HALO + nr]
        up = pltpu.roll(u, nr + 2 * FF_HALO - 1, 0)[FF_HALO:FF_HALO + nr]
        return um * w[0:1] + u[FF_HALO:FF_HALO + nr] * w[1:2] + up * w[2:3] + b

    act_bufs = (act0_ref, act1_ref)
    half = tm // 2

    bounds = list(range(0, nchunks, FF_DOWN_GROUP)) + [nchunks]
    if bounds[-1] - bounds[-2] > 1:
        bounds.insert(-1, nchunks - 1)
    groups = list(zip(bounds[:-1], bounds[1:]))
    group_of = {c: j for j, (lo, hi) in enumerate(groups) for c in range(lo, hi)}

    def activate(c, r0):
        j = group_of[c]
        ug_ref, uv_ref = u_bufs[c % 2]
        gate = conv(ug_ref, c * FF_CHUNK, r0, half)
        val = conv(uv_ref, D_FF + c * FF_CHUNK, r0, half)
        lane0 = (c - groups[j][0]) * FF_CHUNK
        act_bufs[j % 2][r0:r0 + half, lane0:lane0 + FF_CHUNK] = (gate * jax.nn.sigmoid(gate) * val).astype(BF16)

    def down_proj(j):
        lo, hi = groups[j]
        width = (hi - lo) * FF_CHUNK
        part = jnp.dot(act_bufs[j % 2][:, 0:width], wd_ref[lo * FF_CHUNK:hi * FF_CHUNK, :],
                       preferred_element_type=F32)
        if j == 0:
            acc_ref[...] = part
        else:
            acc_ref[...] += part

    up_proj(0, 0)
    up_proj(0, 1)
    for c in range(nchunks):
        if c + 1 < nchunks:
            up_proj(c + 1, 0)
        activate(c, 0)
        if c + 1 < nchunks:
            up_proj(c + 1, 1)
        activate(c, half)
        if c > 0 and group_of[c] != group_of[c - 1]:
            down_proj(group_of[c - 1])
    down_proj(len(groups) - 1)
    out_ref[...] = x_ref[...] + acc_ref[...]


def _ffn(x2d, g, wup, cw, cb, wd, *, seq, tm):
    t_total = x2d.shape[0]
    tm = min(tm, seq)
    r = tm // FF_HALO
    last = t_total // FF_HALO - 1
    const = lambda i: (0, 0)
    resident = dict(pipeline_mode=pl.Buffered(1))
    kern = functools.partial(_ffn_kernel, seq=seq, tm=tm)
    return pl.pallas_call(
        kern,
        grid=(t_total // tm,),
        in_specs=[
            pl.BlockSpec((FF_HALO, D_MODEL), lambda i: (jnp.maximum(i * r - 1, 0), 0)),
            pl.BlockSpec((tm, D_MODEL), lambda i: (i, 0)),
            pl.BlockSpec((FF_HALO, D_MODEL), lambda i: (jnp.minimum((i + 1) * r, last), 0)),
            pl.BlockSpec((1, D_MODEL), const),
            pl.BlockSpec((D_MODEL, 2 * D_FF), const, **resident),
            pl.BlockSpec((3, 2 * D_FF), const),
            pl.BlockSpec((1, 2 * D_FF), const),
            pl.BlockSpec((D_FF, D_MODEL), const, **resident),
        ],
        out_specs=pl.BlockSpec((tm, D_MODEL), lambda i: (i, 0)),
        out_shape=jax.ShapeDtypeStruct((t_total, D_MODEL), F32),
        scratch_shapes=[
            pltpu.VMEM((tm + 2 * FF_HALO, D_MODEL), BF16),
            pltpu.VMEM((tm, D_MODEL), F32),
        ] + [pltpu.VMEM((tm + 2 * FF_HALO, FF_CHUNK), F32)] * 4 + [pltpu.VMEM((tm, FF_DOWN_GROUP * FF_CHUNK), BF16)] * 2,
        compiler_params=_params("parallel"),
        name="ffn",
    )(x2d, x2d, x2d, g, wup, cw, cb, wd)


def _rope_tables(pos, dim, theta):
    inv = theta ** (-jnp.arange(0, dim, 2, dtype=F32) / dim)
    ang = pos.astype(F32)[:, None] * inv[None, :]
    return jnp.cos(ang).T, jnp.sin(ang).T


def _class_tiles(table, dil, n):
    rows, seq = table.shape
    return table.reshape(rows, seq // (dil * n), n, dil).transpose(1, 0, 3, 2).reshape(-1, rows, dil * n)


def _lane_bcast(v, width):
    return jnp.broadcast_to(v.astype(F32)[:, None], (v.shape[0], width))


TM = 512
TQ_C = 512
TQ_B = 1024
TKC = 512
TQ_BAND = 512
A_CLASS_TOKENS = (512, 512, 128)
Q_SCALE = HEAD_DIM ** -0.5 * LOG2E


def _trunk(x, norm_mix, norm_ffn, w_in_ab, q_norm_a, k_norm_a, q_norm_b, k_norm_b,
           lambda_q1, lambda_k1, lambda_q2, lambda_k2, subln_b, w_out_ab,
           w_in_c, q_norm_c, k_norm_c, w_out_c, w_up, conv_w, conv_b, w_down):
    batch, seq, _ = x.shape
    t_total = batch * seq
    depth = norm_mix.shape[0]
    x2d = x.reshape(t_total, D_MODEL)

    cos, sin = _rope_tables(jnp.arange(seq), ROT_DIM, ROPE_THETA)
    rows = seq // GRID_W
    row = jnp.repeat(jnp.arange(rows), GRID_W)
    col = jnp.tile(jnp.arange(GRID_W), rows)
    cr, sr = _rope_tables(row, AXIAL_DIM, AXIAL_THETA)
    cc, sc = _rope_tables(col, AXIAL_DIM, AXIAL_THETA)
    cos_ax = jnp.concatenate([cr, cc], axis=0)
    sin_ax = jnp.concatenate([sr, sc], axis=0)

    for i in range(depth):
        j = i // 2
        g_mix = norm_mix[i].reshape(1, D_MODEL)
        if i % 2 == 0:
            lam_init = 0.8 - 0.6 * math.exp(-0.3 * i)
            w = w_in_ab[j]
            oa, lse = [], []
            for gi, (_, dil) in enumerate(A_PATTERNS):
                n_cls = min(A_CLASS_TOKENS[gi], seq // dil)
                width = dil * n_cls
                gains_a = jnp.concatenate([_lane_bcast(q_norm_a[j] * Q_SCALE, width),
                                           _lane_bcast(k_norm_a[j], width)], 0)
                sl = slice(gi * A_GROUP_W, (gi + 1) * A_GROUP_W)
                wt = jnp.concatenate([w[:, sl], w[:, A_QKV:2 * A_QKV][:, sl], w[:, 2 * A_QKV:3 * A_QKV][:, sl]],
                                     axis=1).T.astype(BF16)
                qt, k, vt = _proj_cls(x2d, g_mix, wt, gains_a, _class_tiles(cos, dil, n_cls),
                                      _class_tiles(sin, dil, n_cls), batch=batch, seq=seq, dil=dil, n=n_cls)
                o_g, lse_g = _band(qt, k, vt, batch=batch, seq=seq, dil=dil, tqs=TQ_BAND)
                oa.append(o_g)
                lse.append(lse_g)
            wt_b = w[:, 3 * A_QKV:].T.astype(BF16)
            gains_b = jnp.concatenate([_lane_bcast(q_norm_b[j] * Q_SCALE, TM), _lane_bcast(k_norm_b[j], TM)], 0)
            qt, k, vt = _proj(x2d, g_mix, wt_b, gains_b, cos, sin, seq=seq,
                              nq=2 * B_HEADS, nk=2 * B_HEADS, nv=B_V, axial=False, tm=TM)
            lam = (jnp.exp(jnp.sum(lambda_q1[j].astype(F32) * lambda_k1[j].astype(F32)))
                   - jnp.exp(jnp.sum(lambda_q2[j].astype(F32) * lambda_k2[j].astype(F32))) + lam_init)
            sg = _lane_bcast(subln_b[j] * (1.0 - lam_init), TQ_B)
            obt = _attn_b(lam.reshape(1).astype(F32), qt, k, vt, sg, batch=batch, seq=seq, tq=TQ_B, tkc=TKC)
            wo = w_out_ab[j].astype(BF16)
            x2d = _outproj_even(x2d, oa, lse, obt, wo[:A_GROUP_W], wo[A_GROUP_W:], seq=seq, tm=TM)
        else:
            wt = w_in_c[j].T.astype(BF16)
            gains_c = jnp.concatenate([_lane_bcast(q_norm_c[j] * Q_SCALE, TM), _lane_bcast(k_norm_c[j], TM)], 0)
            qt, k, vt = _proj(x2d, g_mix, wt, gains_c, cos_ax, sin_ax, seq=seq,
                              nq=C_Q_HEADS, nk=C_KV_HEADS, nv=C_KV_HEADS * HEAD_DIM, axial=True, tm=TM)
            ot = _attn_c(qt, k, vt, batch=batch, seq=seq, tq=TQ_C, tkc=TKC)
            x2d = _outproj_odd(x2d, ot, w_out_c[j].astype(BF16), tm=TM)
        x2d = _ffn(x2d, norm_ffn[i].reshape(1, D_MODEL), w_up[i].astype(BF16), conv_w[i],
                   conv_b[i].reshape(1, 2 * D_FF), w_down[i].astype(BF16), seq=seq, tm=TM)
    return x2d.reshape(batch, seq, D_MODEL)


def kernel(x_prompt, x_sample, norm_mix, norm_ffn, w_in_ab, q_norm_a, k_norm_a, q_norm_b, k_norm_b,
           lambda_q1, lambda_k1, lambda_q2, lambda_k2, subln_b, w_out_ab, w_in_c, q_norm_c, k_norm_c,
           w_out_c, w_up, conv_w, conv_b, w_down):
    params = (norm_mix, norm_ffn, w_in_ab, q_norm_a, k_norm_a, q_norm_b, k_norm_b,
              lambda_q1, lambda_k1, lambda_q2, lambda_k2, subln_b, w_out_ab,
              w_in_c, q_norm_c, k_norm_c, w_out_c, w_up, conv_w, conv_b, w_down)
    return (_trunk(x_prompt, *params), _trunk(x_sample, *params))
```

```python
import functools
import math

import jax
import jax.numpy as jnp
from jax import lax
from jax.experimental import pallas as pl
from jax.experimental.pallas import tpu as pltpu

F32 = jnp.float32
BF16 = jnp.bfloat16

D_MODEL = 1024
HEAD_DIM = 64
A_PATTERNS = ((128, 1), (512, 4), (2048, 16))
A_GROUPS = 3
A_HEADS_PER_GROUP = 4
A_QKV = A_GROUPS * A_HEADS_PER_GROUP * HEAD_DIM
A_GROUP_W = A_HEADS_PER_GROUP * HEAD_DIM
B_HEADS = 4
B_QK = B_HEADS * 2 * HEAD_DIM
B_V = B_HEADS * 2 * HEAD_DIM
C_Q_HEADS = 16
C_KV_HEADS = 4
C_GROUP = C_Q_HEADS // C_KV_HEADS
ROPE_THETA = 500000.0
ROT_DIM = HEAD_DIM // 4
AXIAL_THETA = 10000.0
AXIAL_DIM = HEAD_DIM // 2
GRID_W = 64
D_FF = 2816
NORM_EPS = 1e-6
SUBLN_EPS = 1e-5
NEG_INF = -1e30
LOG2E = 1.4426950408889634
LN2 = 0.6931471805599453

V7X_LANES = 128
V7X_VMEM_BYTES = 64 * 1024 * 1024
VMEM_LIMIT = V7X_VMEM_BYTES - 8 * 1024 * 1024
BAND_HALF = 64
BAND_SUB = 128
BAND_WIN = BAND_SUB + 2 * V7X_LANES
FF_CHUNK = 256
FF_DOWN_GROUP = 5


def _params(*semantics):
    return pltpu.CompilerParams(dimension_semantics=semantics, vmem_limit_bytes=VMEM_LIMIT)


def _rms_rows(x, g):
    ms = jnp.mean(x * x, axis=-1, keepdims=True)
    return x * lax.rsqrt(ms + NORM_EPS) * g


def _head_norm_rot(y, gain, cos, sin, axial):
    ssq = jnp.sum(y * y, axis=0, keepdims=True)
    yn = y * lax.rsqrt(ssq * (1.0 / HEAD_DIM) + NORM_EPS) * gain
    if axial:
        h = AXIAL_DIM // 2
        a1, a2, b1, b2 = yn[0:h], yn[h:2 * h], yn[2 * h:3 * h], yn[3 * h:4 * h]
        cr, cc = cos[0:h], cos[h:2 * h]
        sr, sc = sin[0:h], sin[h:2 * h]
        return jnp.concatenate(
            [a1 * cr - a2 * sr, a2 * cr + a1 * sr, b1 * cc - b2 * sc, b2 * cc + b1 * sc], axis=0)
    h = ROT_DIM // 2
    x1, x2 = yn[0:h], yn[h:2 * h]
    return jnp.concatenate([x1 * cos - x2 * sin, x2 * cos + x1 * sin, yn[2 * h:]], axis=0)


def _proj_kernel(x_ref, g_ref, wt_ref, gain_ref, cos_ref, sin_ref, qt_ref, k_ref, vt_ref,
                 *, nq, nk, axial):
    h = _rms_rows(x_ref[...], g_ref[...]).astype(BF16)
    yt = lax.dot_general(wt_ref[...], h, (((1,), (1,)), ((), ())),
                         preferred_element_type=F32)
    cos = cos_ref[...]
    sin = sin_ref[...]
    gq = gain_ref[0:HEAD_DIM, :]
    gk = gain_ref[HEAD_DIM:2 * HEAD_DIM, :]
    for i in range(nq):
        y = yt[i * HEAD_DIM:(i + 1) * HEAD_DIM, :]
        qt_ref[i * HEAD_DIM:(i + 1) * HEAD_DIM, :] = _head_norm_rot(y, gq, cos, sin, axial).astype(BF16)
    base = nq * HEAD_DIM
    for p in range(nk // 2):
        pair = []
        for i in (2 * p, 2 * p + 1):
            y = yt[base + i * HEAD_DIM:base + (i + 1) * HEAD_DIM, :]
            pair.append(_head_norm_rot(y, gk, cos, sin, axial))
        kt = jnp.concatenate(pair, axis=0)
        k_ref[:, p * 2 * HEAD_DIM:(p + 1) * 2 * HEAD_DIM] = kt.T.astype(BF16)
    base = (nq + nk) * HEAD_DIM
    vt_ref[...] = yt[base:, :].astype(BF16)


def _proj(x2d, g, wt, gains, cos_t, sin_t, *, seq, nq, nk, nv, axial, tm):
    t_total = x2d.shape[0]
    tm = min(tm, seq)
    nlb = seq // tm
    n_out = (nq + nk) * HEAD_DIM + nv
    rot_rows = cos_t.shape[0]
    kern = functools.partial(_proj_kernel, nq=nq, nk=nk, axial=axial)
    return pl.pallas_call(
        kern,
        grid=(t_total // tm,),
        in_specs=[
            pl.BlockSpec((tm, D_MODEL), lambda i: (i, 0)),
            pl.BlockSpec((1, D_MODEL), lambda i: (0, 0)),
            pl.BlockSpec((n_out, D_MODEL), lambda i: (0, 0)),
            pl.BlockSpec((2 * HEAD_DIM, tm), lambda i: (0, 0)),
            pl.BlockSpec((rot_rows, tm), lambda i: (0, i % nlb)),
            pl.BlockSpec((rot_rows, tm), lambda i: (0, i % nlb)),
        ],
        out_specs=[
            pl.BlockSpec((nq * HEAD_DIM, tm), lambda i: (0, i)),
            pl.BlockSpec((tm, nk * HEAD_DIM), lambda i: (i, 0)),
            pl.BlockSpec((nv, tm), lambda i: (0, i)),
        ],
        out_shape=[
            jax.ShapeDtypeStruct((nq * HEAD_DIM, t_total), BF16),
            jax.ShapeDtypeStruct((t_total, nk * HEAD_DIM), BF16),
            jax.ShapeDtypeStruct((nv, t_total), BF16),
        ],
        compiler_params=_params("parallel"),
        name="proj",
    )(x2d, g, wt, gains[:, :tm], cos_t, sin_t)


def _proj_cls_kernel(x_ref, g_ref, wt_ref, gain_ref, cos_ref, sin_ref, qt_ref, k_ref, vt_ref, hs_ref,
                     stage_ref, *, dil, n):
    h = _rms_rows(x_ref[...], g_ref[...])
    if dil == 1:
        hs_ref[...] = h.astype(BF16)
    else:
        for c in range(D_MODEL // V7X_LANES):
            stage_ref[c] = h[:, c * V7X_LANES:(c + 1) * V7X_LANES]
        for r in range(dil):
            for c in range(D_MODEL // V7X_LANES):
                hs_ref[r * n:(r + 1) * n, c * V7X_LANES:(c + 1) * V7X_LANES] = (
                    stage_ref[c, pl.ds(r, n, stride=dil), :].astype(BF16))
    yt = lax.dot_general(wt_ref[...], hs_ref[...], (((1,), (1,)), ((), ())),
                         preferred_element_type=F32)
    cos = cos_ref[...]
    sin = sin_ref[...]
    gq = gain_ref[0:HEAD_DIM, :]
    gk = gain_ref[HEAD_DIM:2 * HEAD_DIM, :]
    nh = A_HEADS_PER_GROUP
    for i in range(nh):
        q = _head_norm_rot(yt[i * HEAD_DIM:(i + 1) * HEAD_DIM, :], gq, cos, sin, False).astype(BF16)
        for r in range(dil):
            qt_ref[r, i * HEAD_DIM:(i + 1) * HEAD_DIM, :] = q[:, r * n:(r + 1) * n]
    base = nh * HEAD_DIM
    for p in range(nh // 2):
        pair = [_head_norm_rot(yt[base + i * HEAD_DIM:base + (i + 1) * HEAD_DIM, :], gk, cos, sin, False)
                for i in (2 * p, 2 * p + 1)]
        kt = jnp.concatenate(pair, axis=0).T.astype(BF16)
        for r in range(dil):
            k_ref[r, :, p * 2 * HEAD_DIM:(p + 1) * 2 * HEAD_DIM] = kt[r * n:(r + 1) * n, :]
    base = 2 * nh * HEAD_DIM
    v = yt[base:, :].astype(BF16)
    for r in range(dil):
        vt_ref[r] = v[:, r * n:(r + 1) * n]


def _proj_cls(x2d, g, wt, gains, cos_c, sin_c, *, batch, seq, dil, n):
    t_total = x2d.shape[0]
    tmx = dil * n
    length = seq // dil
    tps = seq // tmx
    kern = functools.partial(_proj_cls_kernel, dil=dil, n=n)
    feat = pl.BlockSpec((dil, A_GROUP_W, n), lambda i: (i // tps, 0, i % tps))
    return pl.pallas_call(
        kern,
        grid=(t_total // tmx,),
        in_specs=[
            pl.BlockSpec((tmx, D_MODEL), lambda i: (i, 0)),
            pl.BlockSpec((1, D_MODEL), lambda i: (0, 0)),
            pl.BlockSpec((3 * A_GROUP_W, D_MODEL), lambda i: (0, 0)),
            pl.BlockSpec((2 * HEAD_DIM, tmx), lambda i: (0, 0)),
            pl.BlockSpec((None, ROT_DIM // 2, tmx), lambda i: (i % tps, 0, 0)),
            pl.BlockSpec((None, ROT_DIM // 2, tmx), lambda i: (i % tps, 0, 0)),
        ],
        out_specs=[feat, pl.BlockSpec((dil, n, A_GROUP_W), lambda i: (i // tps, i % tps, 0)), feat],
        out_shape=[
            jax.ShapeDtypeStruct((batch * dil, A_GROUP_W, length), BF16),
            jax.ShapeDtypeStruct((batch * dil, length, A_GROUP_W), BF16),
            jax.ShapeDtypeStruct((batch * dil, A_GROUP_W, length), BF16),
        ],
        scratch_shapes=[pltpu.VMEM((tmx, D_MODEL), BF16),
                        pltpu.VMEM((D_MODEL // V7X_LANES, tmx if dil > 1 else 8, V7X_LANES), F32)],
        compiler_params=_params("parallel"),
        name="proj_cls",
    )(x2d, g, wt, gains, cos_c, sin_c)


FLASH_UNROLL = 4
FLASH_COL_TILE = 256
V_PAD = 16


def _flash_scratch(dv, ncols, seq, tkc):
    return [
        pltpu.VMEM((2, 2 * HEAD_DIM, ncols), BF16),
        pltpu.VMEM((dv + V_PAD, seq), BF16),
        pltpu.VMEM((dv + V_PAD, ncols), F32),
        pltpu.VMEM((1, ncols), F32),
        pltpu.VMEM((tkc, ncols), F32), pltpu.VMEM((tkc, ncols), F32),
        pltpu.VMEM((1, ncols), F32), pltpu.VMEM((1, ncols), F32),
    ]


def _flash_chain(k_ref, vt_ref, scratch, build_qpad, epilogue, *, dv, seq, tq, tkc):
    qpad_ref, vext_ref, acc_ref, m_ref, s0, s1, c0, c1 = scratch
    s_bufs, cm_bufs = (s0, s1), (c0, c1)
    nc = seq // tkc
    nqb = seq // tq
    unroll = FLASH_UNROLL if nc % FLASH_UNROLL == 0 else nc
    ncols = qpad_ref.shape[2]
    tiles = [slice(j * FLASH_COL_TILE, (j + 1) * FLASH_COL_TILE) for j in range(ncols // FLASH_COL_TILE)]

    vext_ref[0:dv, :] = vt_ref[...]
    ones_row = lax.broadcasted_iota(jnp.int32, (V_PAD, seq), 0) == 0
    vext_ref[dv:, :] = jnp.where(ones_row, 1.0, 0.0).astype(BF16)

    def chunk(c):
        return pl.ds(c * tkc if isinstance(c, int) else pl.multiple_of(c * tkc, tkc), tkc)

    def scores(qslot, c, slot, t):
        s = jnp.dot(k_ref[chunk(c), :], qpad_ref[qslot, :, t], preferred_element_type=F32)
        s_bufs[slot][:, t] = s
        cm_bufs[slot][:, t] = jnp.max(s, axis=0, keepdims=True)

    def softmax_pv(c, slot, t, first=False):
        cm = cm_bufs[slot][:, t]
        if first:
            m_new = cm
        else:
            m_old = m_ref[:, t]
            m_new = jnp.maximum(m_old, cm)
        m_ref[:, t] = m_new
        p = jnp.exp2((s_bufs[slot][:, t] - m_new).astype(BF16))
        pv = jnp.dot(vext_ref[:, chunk(c)], p, preferred_element_type=F32)
        acc_ref[:, t] = pv if first else jnp.exp2(m_old - m_new) * acc_ref[:, t] + pv

    def step(c, slot, nxt=None, first=False):
        for t in tiles:
            if nxt is not None:
                scores(nxt[0], nxt[1], 1 - slot, t)
            softmax_pv(c, slot, t, first)

    def block_steps(qslot, after_last):
        if nc > unroll:
            step(0, 0, nxt=(qslot, 1), first=True)
            for c in range(1, unroll):
                step(c, c % 2, nxt=(qslot, c + 1))

            def group(g, inner):
                for u in range(unroll):
                    step(unroll * g + u, u % 2, nxt=(qslot, unroll * g + u + 1))
                return inner

            lax.fori_loop(1, nc // unroll - 1, group, 0)
        tail = unroll if nc > unroll else nc
        for c in range(nc - tail, nc):
            step(c, c % 2, nxt=(qslot, c + 1) if c + 1 < nc else after_last, first=(c == 0))

    if nqb % 2 == 0:
        build_qpad(0, 0)
        for t in tiles:
            scores(0, 0, 0, t)

        def pair(i, carry):
            for par in (0, 1):
                qb = 2 * i + par
                build_qpad(jnp.minimum(qb + 1, nqb - 1), 1 - par)
                block_steps(par, (1 - par, 0))
                epilogue(qb)
            return carry

        lax.fori_loop(0, nqb // 2, pair, 0)
    else:
        def block(qb, carry):
            build_qpad(qb, 0)
            for t in tiles:
                scores(0, 0, 0, t)
            block_steps(0, None)
            epilogue(qb)
            return carry

        lax.fori_loop(0, nqb, block, 0)


def _attn_c_kernel(qt_ref, k_ref, vt_ref, o_ref, *scratch, seq, tq, tkc):
    qpad_ref, acc_ref = scratch[0], scratch[2]
    upper = (pl.program_id(1) % 2).astype(F32)

    def col(qb):
        return pl.ds(pl.multiple_of(qb * tq, tq), tq)

    def build_qpad(qb, slot):
        qcat = jnp.concatenate([qt_ref[g * HEAD_DIM:(g + 1) * HEAD_DIM, col(qb)] for g in range(C_GROUP)],
                               axis=1).astype(F32)
        qpad_ref[slot, 0:HEAD_DIM, :] = (qcat * (1.0 - upper)).astype(BF16)
        qpad_ref[slot, HEAD_DIM:, :] = (qcat * upper).astype(BF16)

    def epilogue(qb):
        o = acc_ref[0:HEAD_DIM, :] * (1.0 / acc_ref[HEAD_DIM:HEAD_DIM + 1, :])
        for g in range(C_GROUP):
            o_ref[g * HEAD_DIM:(g + 1) * HEAD_DIM, col(qb)] = o[:, g * tq:(g + 1) * tq].astype(BF16)

    _flash_chain(k_ref, vt_ref, scratch, build_qpad, epilogue, dv=HEAD_DIM, seq=seq, tq=tq, tkc=tkc)


def _attn_c(qt, k, vt, *, batch, seq, tq, tkc):
    t_total = batch * seq
    tq = min(tq, seq)
    tkc = min(tkc, seq)
    kern = functools.partial(_attn_c_kernel, seq=seq, tq=tq, tkc=tkc)
    return pl.pallas_call(
        kern,
        grid=(batch, C_KV_HEADS),
        in_specs=[
            pl.BlockSpec((C_GROUP * HEAD_DIM, seq), lambda b, n: (n, b)),
            pl.BlockSpec((seq, 2 * HEAD_DIM), lambda b, n: (b, n // 2)),
            pl.BlockSpec((HEAD_DIM, seq), lambda b, n: (n, b)),
        ],
        out_specs=pl.BlockSpec((C_GROUP * HEAD_DIM, seq), lambda b, n: (n, b)),
        out_shape=jax.ShapeDtypeStruct((C_Q_HEADS * HEAD_DIM, t_total), BF16),
        scratch_shapes=_flash_scratch(HEAD_DIM, C_GROUP * tq, seq, tkc),
        compiler_params=_params("parallel", "parallel"),
        name="attn_c",
    )(qt, k, vt)


def _attn_b_kernel(lam_ref, qt_ref, k_ref, vt_ref, sg_ref, o_ref, *scratch, seq, tq, tkc):
    dv = 2 * HEAD_DIM
    qpad_ref, acc_ref = scratch[0], scratch[2]
    zeros = jnp.zeros((HEAD_DIM, tq), BF16)

    def col(qb):
        return pl.ds(pl.multiple_of(qb * tq, tq), tq)

    def build_qpad(qb, slot):
        q = qt_ref[:, col(qb)]
        qpad_ref[slot, 0:HEAD_DIM, 0:tq] = q[0:HEAD_DIM]
        qpad_ref[slot, 0:HEAD_DIM, tq:] = zeros
        qpad_ref[slot, HEAD_DIM:, 0:tq] = zeros
        qpad_ref[slot, HEAD_DIM:, tq:] = q[HEAD_DIM:]

    def epilogue(qb):
        on = acc_ref[0:dv, :] * (1.0 / acc_ref[dv:dv + 1, :])
        o = on[:, 0:tq] - lam_ref[0] * on[:, tq:]
        ms = jnp.mean(o * o, axis=0, keepdims=True)
        o_ref[:, col(qb)] = (o * lax.rsqrt(ms + SUBLN_EPS) * sg_ref[...]).astype(BF16)

    _flash_chain(k_ref, vt_ref, scratch, build_qpad, epilogue, dv=dv, seq=seq, tq=tq, tkc=tkc)


def _attn_b(lam, qt, k, vt, sg, *, batch, seq, tq, tkc):
    t_total = batch * seq
    tq = min(tq, seq)
    tkc = min(tkc, seq)
    kern = functools.partial(_attn_b_kernel, seq=seq, tq=tq, tkc=tkc)
    return pl.pallas_call(
        kern,
        grid=(batch, B_HEADS),
        in_specs=[
            pl.BlockSpec(memory_space=pltpu.SMEM),
            pl.BlockSpec((2 * HEAD_DIM, seq), lambda b, h: (h, b)),
            pl.BlockSpec((seq, 2 * HEAD_DIM), lambda b, h: (b, h)),
            pl.BlockSpec((2 * HEAD_DIM, seq), lambda b, h: (h, b)),
            pl.BlockSpec((2 * HEAD_DIM, tq), lambda b, h: (0, 0)),
        ],
        out_specs=pl.BlockSpec((2 * HEAD_DIM, seq), lambda b, h: (h, b)),
        out_shape=jax.ShapeDtypeStruct((B_V, t_total), BF16),
        scratch_shapes=_flash_scratch(2 * HEAD_DIM, 2 * tq, seq, tkc),
        compiler_params=_params("parallel", "parallel"),
        name="attn_b",
    )(lam, qt, k, vt, sg[:, :tq])


def _band_kernel(qt_ref, kp_ref, km_ref, kn_ref, vp_ref, vm_ref, vn_ref, o_ref, lse_ref, *, length, tqs):
    l0 = pl.program_id(1) * tqs
    kwin = jnp.concatenate([kp_ref[...], km_ref[...], kn_ref[...]], axis=0)
    vwin = jnp.concatenate([vp_ref[...], vm_ref[...], vn_ref[...]], axis=1)
    row = lax.broadcasted_iota(jnp.int32, (BAND_WIN, BAND_SUB), 0)
    col = lax.broadcasted_iota(jnp.int32, (BAND_WIN, BAND_SUB), 1)
    rel = row - V7X_LANES - col
    in_band = jnp.abs(rel) <= BAND_HALF
    zeros = jnp.zeros((HEAD_DIM, BAND_SUB), BF16)
    npair = A_HEADS_PER_GROUP // 2
    units = [(j, p) for j in range(tqs // BAND_SUB) for p in range(npair)]
    scores = {}
    for j, p in units:
        c0 = j * BAND_SUB
        q = qt_ref[:, c0:c0 + BAND_SUB]
        qa = q[2 * p * HEAD_DIM:(2 * p + 1) * HEAD_DIM]
        qb = q[(2 * p + 1) * HEAD_DIM:(2 * p + 2) * HEAD_DIM]
        qpad = jnp.concatenate([jnp.concatenate([qa, zeros], axis=1),
                                jnp.concatenate([zeros, qb], axis=1)], axis=0)
        s = jnp.dot(kwin[c0:c0 + BAND_WIN, p * 2 * HEAD_DIM:(p + 1) * 2 * HEAD_DIM], qpad,
                    preferred_element_type=F32)
        kpos = l0 + (c0 - V7X_LANES) + row
        valid = in_band & (kpos >= 0) & (kpos < length)
        scores[j, p] = jnp.where(jnp.concatenate([valid, valid], axis=1), s, NEG_INF)
    probs = {}
    for u in units:
        m = jnp.max(scores[u], axis=0, keepdims=True)
        pr = jnp.exp2(scores[u] - m)
        probs[u] = (m, jnp.sum(pr, axis=0, keepdims=True), pr.astype(BF16))
    outs = {}
    for j, p in units:
        c0 = j * BAND_SUB
        m, l, pb = probs[j, p]
        va = vwin[2 * p * HEAD_DIM:(2 * p + 1) * HEAD_DIM, c0:c0 + BAND_WIN]
        vb = vwin[(2 * p + 1) * HEAD_DIM:(2 * p + 2) * HEAD_DIM, c0:c0 + BAND_WIN]
        oa = jnp.dot(va, pb[:, :BAND_SUB], preferred_element_type=F32)
        ob = jnp.dot(vb, pb[:, BAND_SUB:], preferred_element_type=F32)
        inv = 1.0 / l
        lse = m * LN2 + jnp.log(l)
        outs[j, p] = ([oa * inv[:, :BAND_SUB], ob * inv[:, BAND_SUB:]],
                      [jnp.broadcast_to(lse[:, :BAND_SUB], (HEAD_DIM, BAND_SUB)),
                       jnp.broadcast_to(lse[:, BAND_SUB:], (HEAD_DIM, BAND_SUB))])
    for j in range(tqs // BAND_SUB):
        c0 = j * BAND_SUB
        o_ref[c0:c0 + BAND_SUB, :] = jnp.concatenate([x for p in range(npair) for x in outs[j, p][0]], axis=0).T
        lse_ref[c0:c0 + BAND_SUB, :] = jnp.concatenate([x for p in range(npair) for x in outs[j, p][1]], axis=0).T


def _band(qt, k, vt, *, batch, seq, dil, tqs):
    length = seq // dil
    tqs = min(tqs, length)
    nqb = length // tqs
    ncls = batch * dil
    nkb = length // V7X_LANES
    r128 = tqs // V7X_LANES

    def prev_blk(i):
        return jnp.maximum(i * r128 - 1, 0)

    def next_blk(i):
        return jnp.minimum((i + 1) * r128, nkb - 1)

    kern = functools.partial(_band_kernel, length=length, tqs=tqs)
    out_spec = pl.BlockSpec((None, tqs, A_GROUP_W), lambda c, i: (c, i, 0))
    return pl.pallas_call(
        kern,
        grid=(ncls, nqb),
        in_specs=[
            pl.BlockSpec((None, A_GROUP_W, tqs), lambda c, i: (c, 0, i)),
            pl.BlockSpec((None, V7X_LANES, A_GROUP_W), lambda c, i: (c, prev_blk(i), 0)),
            pl.BlockSpec((None, tqs, A_GROUP_W), lambda c, i: (c, i, 0)),
            pl.BlockSpec((None, V7X_LANES, A_GROUP_W), lambda c, i: (c, next_blk(i), 0)),
            pl.BlockSpec((None, A_GROUP_W, V7X_LANES), lambda c, i: (c, 0, prev_blk(i))),
            pl.BlockSpec((None, A_GROUP_W, tqs), lambda c, i: (c, 0, i)),
            pl.BlockSpec((None, A_GROUP_W, V7X_LANES), lambda c, i: (c, 0, next_blk(i))),
        ],
        out_specs=[out_spec, out_spec],
        out_shape=[jax.ShapeDtypeStruct((ncls, length, A_GROUP_W), F32)] * 2,
        compiler_params=_params("parallel", "parallel"),
        name="band",
    )(qt, k, k, k, vt, vt, vt)


def _outproj_even_kernel(x_ref, o0_ref, o1_ref, o2_ref, l0_ref, l1_ref, l2_ref, obt_ref, wa_ref, wb_ref,
                         out_ref, *bufs, tm):
    def natural(blk_ref, buf_ref, dil):
        if dil == 1:
            return blk_ref[0]
        ntile = A_GROUP_W // V7X_LANES
        for r in range(dil):
            for c in range(ntile):
                buf_ref[c, pl.ds(r, tm // dil, stride=dil), :] = blk_ref[r, :, c * V7X_LANES:(c + 1) * V7X_LANES]
        return jnp.concatenate([buf_ref[c] for c in range(ntile)], axis=1)

    dils = [d for _, d in A_PATTERNS]
    o = [natural(ref, bufs[2 * gi], d) for gi, (ref, d) in enumerate(zip((o0_ref, o1_ref, o2_ref), dils))]
    l = [natural(ref, bufs[2 * gi + 1], d) for gi, (ref, d) in enumerate(zip((l0_ref, l1_ref, l2_ref), dils))]
    m = jnp.maximum(jnp.maximum(l[0], l[1]), l[2])
    w0, w1, w2 = jnp.exp(l[0] - m), jnp.exp(l[1] - m), jnp.exp(l[2] - m)
    oa = (w0 * o[0] + w1 * o[1] + w2 * o[2]) / (w0 + w1 + w2)
    acc = jnp.dot(oa.astype(BF16), wa_ref[...], preferred_element_type=F32)
    acc += lax.dot_general(obt_ref[...], wb_ref[...], (((0,), (0,)), ((), ())),
                           preferred_element_type=F32)
    out_ref[...] = x_ref[...] + acc


def _outproj_even(x2d, oa, lse, obt, wa, wb, *, seq, tm):
    t_total = x2d.shape[0]
    tps = seq // tm
    row = lambda i: (i, 0)
    const = lambda i: (0, 0)
    a_specs = [pl.BlockSpec((d, tm // d, A_GROUP_W), lambda i: (i // tps, i % tps, 0)) for _, d in A_PATTERNS]
    return pl.pallas_call(
        functools.partial(_outproj_even_kernel, tm=tm),
        grid=(t_total // tm,),
        in_specs=[pl.BlockSpec((tm, D_MODEL), row)] + a_specs * 2 + [
            pl.BlockSpec((B_V, tm), lambda i: (0, i)),
            pl.BlockSpec((A_GROUP_W, D_MODEL), const),
            pl.BlockSpec((B_V, D_MODEL), const),
        ],
        out_specs=pl.BlockSpec((tm, D_MODEL), row),
        out_shape=jax.ShapeDtypeStruct((t_total, D_MODEL), F32),
        scratch_shapes=[pltpu.VMEM((A_GROUP_W // V7X_LANES, tm, V7X_LANES), F32)] * (2 * A_GROUPS),
        compiler_params=_params("parallel"),
        name="outproj_even",
    )(x2d, *oa, *lse, obt, wa, wb)


def _outproj_odd_kernel(x_ref, ot_ref, w_ref, out_ref):
    acc = lax.dot_general(ot_ref[...], w_ref[...], (((0,), (0,)), ((), ())), preferred_element_type=F32)
    out_ref[...] = x_ref[...] + acc


def _outproj_odd(x2d, ot, w, *, tm):
    t_total = x2d.shape[0]
    n_in = ot.shape[0]
    return pl.pallas_call(
        _outproj_odd_kernel,
        grid=(t_total // tm,),
        in_specs=[
            pl.BlockSpec((tm, D_MODEL), lambda i: (i, 0)),
            pl.BlockSpec((n_in, tm), lambda i: (0, i)),
            pl.BlockSpec((n_in, D_MODEL), lambda i: (0, 0)),
        ],
        out_specs=pl.BlockSpec((tm, D_MODEL), lambda i: (i, 0)),
        out_shape=jax.ShapeDtypeStruct((t_total, D_MODEL), F32),
        compiler_params=_params("parallel"),
        name="outproj_odd",
    )(x2d, ot, w)


FF_HALO = 16


def _ffn_kernel(xp_ref, x_ref, xn_ref, g_ref, wup_ref, cw_ref, cb_ref, wd_ref, out_ref, hs_ref, acc_ref,
                ug0_ref, ug1_ref, uv0_ref, uv1_ref, act0_ref, act1_ref, *, seq, tm):
    i = pl.program_id(0)
    g = g_ref[...]
    has_prev = ((i * tm) % seq != 0).astype(F32)
    has_next = (((i + 1) * tm) % seq != 0).astype(F32)
    hs_ref[0:FF_HALO, :] = (_rms_rows(xp_ref[...], g) * has_prev).astype(BF16)
    hs_ref[FF_HALO:FF_HALO + tm, :] = _rms_rows(x_ref[...], g).astype(BF16)
    hs_ref[FF_HALO + tm:, :] = (_rms_rows(xn_ref[...], g) * has_next).astype(BF16)
    rows = tm + 2 * FF_HALO
    u_bufs = ((ug0_ref, uv0_ref), (ug1_ref, uv1_ref))
    nchunks = D_FF // FF_CHUNK

    def up_proj(c, which):
        u_ref = u_bufs[c % 2][which]
        c0 = which * D_FF + c * FF_CHUNK
        u_ref[...] = jnp.dot(hs_ref[...], wup_ref[:, c0:c0 + FF_CHUNK], preferred_element_type=F32)

    def conv(u_ref, c0, r0, nr):
        u = u_ref[r0:r0 + nr + 2 * FF_HALO, :]
        w = cw_ref[:, c0:c0 + FF_CHUNK]
        b = cb_ref[:, c0:c0 + FF_CHUNK]
        um = pltpu.roll(u, 1, 0)[FF_HALO:FF_HALO + nr]
        up = pltpu.roll(u, nr + 2 * FF_HALO - 1, 0)[FF_HALO:FF_HALO + nr]
        return um * w[0:1] + u[FF_HALO:FF_HALO + nr] * w[1:2] + up * w[2:3] + b

    act_bufs = (act0_ref, act1_ref)
    half = tm // 2

    bounds = list(range(0, nchunks, FF_DOWN_GROUP)) + [nchunks]
    if bounds[-1] - bounds[-2] > 1:
        bounds.insert(-1, nchunks - 1)
    groups = list(zip(bounds[:-1], bounds[1:]))
    group_of = {c: j for j, (lo, hi) in enumerate(groups) for c in range(lo, hi)}

    def activate(c, r0):
        j = group_of[c]
        ug_ref, uv_ref = u_bufs[c % 2]
        gate = conv(ug_ref, c * FF_CHUNK, r0, half)
        val = conv(uv_ref, D_FF + c * FF_CHUNK, r0, half)
        lane0 = (c - groups[j][0]) * FF_CHUNK
        act_bufs[j % 2][r0:r0 + half, lane0:lane0 + FF_CHUNK] = (gate * jax.nn.sigmoid(gate) * val).astype(BF16)

    def down_proj(j):
        lo, hi = groups[j]
        width = (hi - lo) * FF_CHUNK
        part = jnp.dot(act_bufs[j % 2][:, 0:width], wd_ref[lo * FF_CHUNK:hi * FF_CHUNK, :],
                       preferred_element_type=F32)
        if j == 0:
            acc_ref[...] = part
        else:
            acc_ref[...] += part

    up_proj(0, 0)
    up_proj(0, 1)
    for c in range(nchunks):
        if c + 1 < nchunks:
            up_proj(c + 1, 0)
        activate(c, 0)
        if c + 1 < nchunks:
            up_proj(c + 1, 1)
        activate(c, half)
        if c > 0 and group_of[c] != group_of[c - 1]:
            down_proj(group_of[c - 1])
    down_proj(len(groups) - 1)
    out_ref[...] = x_ref[...] + acc_ref[...]


def _ffn(x2d, g, wup, cw, cb, wd, *, seq, tm):
    t_total = x2d.shape[0]
    tm = min(tm, seq)
    r = tm // FF_HALO
    last = t_total // FF_HALO - 1
    const = lambda i: (0, 0)
    resident = dict(pipeline_mode=pl.Buffered(1))
    kern = functools.partial(_ffn_kernel, seq=seq, tm=tm)
    return pl.pallas_call(
        kern,
        grid=(t_total // tm,),
        in_specs=[
            pl.BlockSpec((FF_HALO, D_MODEL), lambda i: (jnp.maximum(i * r - 1, 0), 0)),
            pl.BlockSpec((tm, D_MODEL), lambda i: (i, 0)),
            pl.BlockSpec((FF_HALO, D_MODEL), lambda i: (jnp.minimum((i + 1) * r, last), 0)),
            pl.BlockSpec((1, D_MODEL), const),
            pl.BlockSpec((D_MODEL, 2 * D_FF), const, **resident),
            pl.BlockSpec((3, 2 * D_FF), const),
            pl.BlockSpec((1, 2 * D_FF), const),
            pl.BlockSpec((D_FF, D_MODEL), const, **resident),
        ],
        out_specs=pl.BlockSpec((tm, D_MODEL), lambda i: (i, 0)),
        out_shape=jax.ShapeDtypeStruct((t_total, D_MODEL), F32),
        scratch_shapes=[
            pltpu.VMEM((tm + 2 * FF_HALO, D_MODEL), BF16),
            pltpu.VMEM((tm, D_MODEL), F32),
        ] + [pltpu.VMEM((tm + 2 * FF_HALO, FF_CHUNK), F32)] * 4 + [pltpu.VMEM((tm, FF_DOWN_GROUP * FF_CHUNK), BF16)] * 2,
        compiler_params=_params("parallel"),
        name="ffn",
    )(x2d, x2d, x2d, g, wup, cw, cb, wd)


def _rope_tables(pos, dim, theta):
    inv = theta ** (-jnp.arange(0, dim, 2, dtype=F32) / dim)
    ang = pos.astype(F32)[:, None] * inv[None, :]
    return jnp.cos(ang).T, jnp.sin(ang).T


def _class_tiles(table, dil, n):
    rows, seq = table.shape
    return table.reshape(rows, seq // (dil * n), n, dil).transpose(1, 0, 3, 2).reshape(-1, rows, dil * n)


def _lane_bcast(v, width):
    return jnp.broadcast_to(v.astype(F32)[:, None], (v.shape[0], width))


TM = 512
TQ_C = 512
TQ_B = 1024
TKC = 512
TQ_BAND = 512
A_CLASS_TOKENS = (512, 512, 128)
Q_SCALE = HEAD_DIM ** -0.5 * LOG2E


def _trunk(x, norm_mix, norm_ffn, w_in_ab, q_norm_a, k_norm_a, q_norm_b, k_norm_b,
           lambda_q1, lambda_k1, lambda_q2, lambda_k2, subln_b, w_out_ab,
           w_in_c, q_norm_c, k_norm_c, w_out_c, w_up, conv_w, conv_b, w_down):
    batch, seq, _ = x.shape
    t_total = batch * seq
    depth = norm_mix.shape[0]
    x2d = x.reshape(t_total, D_MODEL)

    cos, sin = _rope_tables(jnp.arange(seq), ROT_DIM, ROPE_THETA)
    rows = seq // GRID_W
    row = jnp.repeat(jnp.arange(rows), GRID_W)
    col = jnp.tile(jnp.arange(GRID_W), rows)
    cr, sr = _rope_tables(row, AXIAL_DIM, AXIAL_THETA)
    cc, sc = _rope_tables(col, AXIAL_DIM, AXIAL_THETA)
    cos_ax = jnp.concatenate([cr, cc], axis=0)
    sin_ax = jnp.concatenate([sr, sc], axis=0)

    for i in range(depth):
        j = i // 2
        g_mix = norm_mix[i].reshape(1, D_MODEL)
        if i % 2 == 0:
            lam_init = 0.8 - 0.6 * math.exp(-0.3 * i)
            w = w_in_ab[j]
            oa, lse = [], []
            for gi, (_, dil) in enumerate(A_PATTERNS):
                n_cls = min(A_CLASS_TOKENS[gi], seq // dil)
                width = dil * n_cls
                gains_a = jnp.concatenate([_lane_bcast(q_norm_a[j] * Q_SCALE, width),
                                           _lane_bcast(k_norm_a[j], width)], 0)
                sl = slice(gi * A_GROUP_W, (gi + 1) * A_GROUP_W)
                wt = jnp.concatenate([w[:, sl], w[:, A_QKV:2 * A_QKV][:, sl], w[:, 2 * A_QKV:3 * A_QKV][:, sl]],
                                     axis=1).T.astype(BF16)
                qt, k, vt = _proj_cls(x2d, g_mix, wt, gains_a, _class_tiles(cos, dil, n_cls),
                                      _class_tiles(sin, dil, n_cls), batch=batch, seq=seq, dil=dil, n=n_cls)
                o_g, lse_g = _band(qt, k, vt, batch=batch, seq=seq, dil=dil, tqs=TQ_BAND)
                oa.append(o_g)
                lse.append(lse_g)
            wt_b = w[:, 3 * A_QKV:].T.astype(BF16)
            gains_b = jnp.concatenate([_lane_bcast(q_norm_b[j] * Q_SCALE, TM), _lane_bcast(k_norm_b[j], TM)], 0)
            qt, k, vt = _proj(x2d, g_mix, wt_b, gains_b, cos, sin, seq=seq,
                              nq=2 * B_HEADS, nk=2 * B_HEADS, nv=B_V, axial=False, tm=TM)
            lam = (jnp.exp(jnp.sum(lambda_q1[j].astype(F32) * lambda_k1[j].astype(F32)))
                   - jnp.exp(jnp.sum(lambda_q2[j].astype(F32) * lambda_k2[j].astype(F32))) + lam_init)
            sg = _lane_bcast(subln_b[j] * (1.0 - lam_init), TQ_B)
            obt = _attn_b(lam.reshape(1).astype(F32), qt, k, vt, sg, batch=batch, seq=seq, tq=TQ_B, tkc=TKC)
            wo = w_out_ab[j].astype(BF16)
            x2d = _outproj_even(x2d, oa, lse, obt, wo[:A_GROUP_W], wo[A_GROUP_W:], seq=seq, tm=TM)
        else:
            wt = w_in_c[j].T.astype(BF16)
            gains_c = jnp.concatenate([_lane_bcast(q_norm_c[j] * Q_SCALE, TM), _lane_bcast(k_norm_c[j], TM)], 0)
            qt, k, vt = _proj(x2d, g_mix, wt, gains_c, cos_ax, sin_ax, seq=seq,
                              nq=C_Q_HEADS, nk=C_KV_HEADS, nv=C_KV_HEADS * HEAD_DIM, axial=True, tm=TM)
            ot = _attn_c(qt, k, vt, batch=batch, seq=seq, tq=TQ_C, tkc=TKC)
            x2d = _outproj_odd(x2d, ot, w_out_c[j].astype(BF16), tm=TM)
        x2d = _ffn(x2d, norm_ffn[i].reshape(1, D_MODEL), w_up[i].astype(BF16), conv_w[i],
                   conv_b[i].reshape(1, 2 * D_FF), w_down[i].astype(BF16), seq=seq, tm=TM)
    return x2d.reshape(batch, seq, D_MODEL)


def kernel(x_prompt, x_sample, norm_mix, norm_ffn, w_in_ab, q_norm_a, k_norm_a, q_norm_b, k_norm_b,
           lambda_q1, lambda_k1, lambda_q2, lambda_k2, subln_b, w_out_ab, w_in_c, q_norm_c, k_norm_c,
           w_out_c, w_up, conv_w, conv_b, w_down):
    params = (norm_mix, norm_ffn, w_in_ab, q_norm_a, k_norm_a, q_norm_b, k_norm_b,
              lambda_q1, lambda_k1, lambda_q2, lambda_k2, subln_b, w_out_ab,
              w_in_c, q_norm_c, k_norm_c, w_out_c, w_up, conv_w, conv_b, w_down)
    return (_trunk(x_prompt, *params), _trunk(x_sample, *params))
```
